```python
import jax, jax.numpy as jnp
from jax import lax
import numpy as np

D_MODEL = 4096
BATCH = 4
SEQ = 2048
DEPTH = 2
DEC_BATCH = 8
DEC_SEQ = 1
PAST_LEN = 16384
PAGE_SIZE = 128

N_A = DEPTH // 2
N_B = DEPTH - N_A
LRU_WIDTH = D_MODEL
LRU_BLOCKS = 16
LRU_BW = LRU_WIDTH // LRU_BLOCKS
CONV_W = 4
LRU_C = 8.0
D_FF = -((-8 * D_MODEL) // (3 * 256)) * 256
HEAD_DIM = 128
N_HEADS = D_MODEL // HEAD_DIM
N_KV = 4
HPG = N_HEADS // N_KV
N_BRANCH = 3
CMP_BLOCK = 32
CMP_STRIDE = 16
CMP_HIDDEN = HEAD_DIM
SLC_BLOCK = 64
N_SEL = 16
WINDOW = 512
WIN_Q_BLOCK = 128
SLC_Q_CHUNK = 32
EPS = 1e-6
NEG = -1e30
FORCE = 1e4

kernel_name = 'yoco_rglru_nsa_decoder_step'


def _rms(x, w):
    xf = x.astype(jnp.float32)
    y = xf * lax.rsqrt(jnp.mean(xf * xf, axis=-1, keepdims=True) + EPS)
    return (y * w.astype(jnp.float32)).astype(x.dtype)


def _ada(c, w, b, n):
    m = jax.nn.silu(c) @ w + b
    return [t[:, None, :] for t in jnp.split(m, n, axis=-1)]


def _modulate(x, w_norm, shift, scale):
    return _rms(x, w_norm) * (1.0 + scale) + shift


def _masked_softmax(s, mask):
    s = jnp.where(mask, s.astype(jnp.float32), NEG)
    m = jnp.max(s, axis=-1, keepdims=True)
    e = jnp.where(mask, jnp.exp(s - m), 0.0)
    return e / jnp.maximum(jnp.sum(e, axis=-1, keepdims=True), 1e-30)


def _swiglu(u, w13, w2):
    g, v = jnp.split(u @ w13, 2, axis=-1)
    return (jax.nn.silu(g) * v) @ w2


def _linear_scan(a, b, h0):
    def comb(l, r):
        return (l[0] * r[0], r[0] * l[1] + r[1])
    a_cum, b_cum = lax.associative_scan(comb, (a, b), axis=1)
    return a_cum * h0[:, None, :] + b_cum


def _rglru_mixer(u, conv_prev, h_prev, w_in, b_in, conv_w, conv_b, gate_w, gate_b, lam, w_out, b_out):
    bn, t, _ = u.shape
    y_br, x_br = jnp.split(u @ w_in + b_in, 2, axis=-1)
    xin = jnp.concatenate([conv_prev.astype(x_br.dtype), x_br], axis=1)
    xc = conv_b + sum(xin[:, k:k + t] * conv_w[k] for k in range(CONV_W))
    new_conv = xin[:, t:]
    xb = xc.reshape(bn, t, LRU_BLOCKS, LRU_BW)
    gates = jnp.einsum('btnc,gncd->gbtnd', xb, gate_w).reshape(2, bn, t, LRU_WIDTH) + gate_b[:, None, None, :]
    r = jax.nn.sigmoid(gates[0].astype(jnp.float32))
    i = jax.nn.sigmoid(gates[1].astype(jnp.float32))
    log_a = -LRU_C * r * jax.nn.softplus(-lam.astype(jnp.float32))
    a = jnp.exp(log_a)
    mult = jnp.sqrt(-jnp.expm1(2.0 * log_a))
    h = _linear_scan(a, mult * i * xc.astype(jnp.float32), h_prev.astype(jnp.float32))
    out = (h.astype(u.dtype) * jax.nn.gelu(y_br)) @ w_out + b_out
    return out, new_conv, h[:, -1]


def _shared_kv(h, c, kv_ada_w, kv_ada_b, kv_norm_w, w_kv, k_norm_w):
    shift, scale = _ada(c, kv_ada_w, kv_ada_b, 2)
    s = _modulate(h, kv_norm_w, shift, scale)
    bn, t, _ = h.shape
    kv = (s @ w_kv).reshape(bn, t, N_BRANCH, 2, N_KV, HEAD_DIM)
    cmp_kv = kv[:, :, 0]
    slc_kv = jnp.stack([_rms(kv[:, :, 1, 0], k_norm_w[1]), kv[:, :, 1, 1]], axis=2)
    win_kv = jnp.stack([_rms(kv[:, :, 2, 0], k_norm_w[2]), kv[:, :, 2, 1]], axis=2)
    return cmp_kv, slc_kv, win_kv


def _pad_rows(rows, mult):
    pad = (-rows.shape[1]) % mult
    return jnp.pad(rows, ((0, 0), (0, pad)) + ((0, 0),) * (rows.ndim - 2))


def _gather_pages(cache, page_table):
    g = cache[page_table]
    return g.reshape(g.shape[0], -1, *cache.shape[2:])


def _nsa_query(u, w_qg, q_norm_w):
    bn, t, _ = u.shape
    qg = u @ w_qg
    q = qg[..., :N_HEADS * HEAD_DIM].reshape(bn, t, N_KV, HPG, HEAD_DIM)
    q = _rms(q, q_norm_w) * (HEAD_DIM ** -0.5)
    g = jax.nn.sigmoid(qg[..., N_HEADS * HEAD_DIM:].astype(jnp.float32)).reshape(bn, t, N_KV, HPG, N_BRANCH)
    return q, g


def _compress(rows, pe, w1, b1, w2, b2):
    bn, ln, g, d = rows.shape
    nch = ln // CMP_STRIDE
    ch = rows.reshape(bn, nch, CMP_STRIDE, g, d).transpose(0, 1, 3, 2, 4).reshape(bn, nch, g, CMP_STRIDE * d)
    half = CMP_STRIDE * d
    hid = ch[:, :-1] @ w1[:half] + ch[:, 1:] @ w1[half:] + (pe.reshape(-1) @ w1 + b1)
    return jax.nn.gelu(hid) @ w2 + b2


def _slc_attend(q, qpos, kb, vb, idx):
    bn, tq = q.shape[:2]
    k_sel = idx.shape[-1]
    bi = jnp.arange(bn)[:, None, None, None]
    gi = jnp.arange(N_KV)[None, None, :, None]
    ks = kb[bi, gi, idx].reshape(bn, tq, N_KV, k_sel * SLC_BLOCK, HEAD_DIM)
    vs = vb[bi, gi, idx].reshape(bn, tq, N_KV, k_sel * SLC_BLOCK, HEAD_DIM)
    kpos = (idx[..., None] * SLC_BLOCK + jnp.arange(SLC_BLOCK)).reshape(bn, tq, N_KV, 1, k_sel * SLC_BLOCK)
    s = jnp.einsum('bqghd,bqgkd->bqghk', q, ks)
    p = _masked_softmax(s, kpos <= qpos[None, :, None, None, None])
    return jnp.einsum('bqghk,bqgkd->bqghd', p.astype(vs.dtype), vs)


def _slc_branch(q, qpos, kb, vb, idx):
    tq = q.shape[1]
    if tq <= SLC_Q_CHUNK or tq % SLC_Q_CHUNK != 0:
        return _slc_attend(q, qpos, kb, vb, idx)
    nc = tq // SLC_Q_CHUNK

    def chunk(a):
        return jnp.moveaxis(a.reshape(a.shape[0], nc, SLC_Q_CHUNK, *a.shape[2:]), 1, 0)
    out = lax.map(lambda xs: _slc_attend(xs[0], xs[1], kb, vb, xs[2]),
                  (chunk(q), qpos.reshape(nc, SLC_Q_CHUNK), chunk(idx)))
    return jnp.moveaxis(out, 0, 1).reshape(q.shape)


def _win_attend(q, qpos, k, v, kpos):
    s = jnp.einsum('bqghd,bkgd->bqghk', q, k)
    dlt = qpos[:, None] - kpos[None, :]
    mask = (dlt >= 0) & (dlt < WINDOW) & (kpos[None, :] >= 0)
    p = _masked_softmax(s, mask[None, :, None, None, :])
    return jnp.einsum('bqghk,bkgd->bqghd', p.astype(v.dtype), v)


def _win_prompt(q, win_rows):
    t = q.shape[1]
    nb = t // WIN_Q_BLOCK
    kp = jnp.pad(win_rows, ((0, 0), (WINDOW, 0), (0, 0), (0, 0), (0, 0)))

    def blk(b):
        start = b * WIN_Q_BLOCK
        qb = lax.dynamic_slice_in_dim(q, start, WIN_Q_BLOCK, axis=1)
        kvb = lax.dynamic_slice_in_dim(kp, start, WIN_Q_BLOCK + WINDOW, axis=1)
        qpos = start + jnp.arange(WIN_Q_BLOCK)
        kpos = start - WINDOW + jnp.arange(WIN_Q_BLOCK + WINDOW)
        return _win_attend(qb, qpos, kvb[:, :, 0], kvb[:, :, 1], kpos)
    out = lax.map(blk, jnp.arange(nb))
    return jnp.moveaxis(out, 0, 1).reshape(q.shape)


def _nsa_mixer(q, g, qpos, cmp_rows, slc_rows, o_win, cmp_pe, cmp_w1, cmp_b1, cmp_w2, cmp_b2, k_norm_w, w_o):
    bn, tq = q.shape[:2]
    ln = cmp_rows.shape[1]
    kc = _rms(_compress(cmp_rows[:, :, 0], cmp_pe[0], cmp_w1[0], cmp_b1[0], cmp_w2[0], cmp_b2[0]), k_norm_w[0])
    vc = _compress(cmp_rows[:, :, 1], cmp_pe[1], cmp_w1[1], cmp_b1[1], cmp_w2[1], cmp_b2[1])
    n_cmp = kc.shape[1]
    cmp_end = jnp.arange(n_cmp) * CMP_STRIDE + CMP_BLOCK - 1
    s = jnp.einsum('bqghd,bngd->bqghn', q, kc)
    p = _masked_softmax(s, cmp_end <= qpos[:, None, None, None])
    o_cmp = jnp.einsum('bqghn,bngd->bqghd', p.astype(vc.dtype), vc)
    n_slc = ln // SLC_BLOCK
    ratio = SLC_BLOCK // CMP_STRIDE
    p_grp = jnp.pad(p.sum(axis=3), ((0, 0), (0, 0), (0, 0), (1, 1)))
    p_slc = p_grp[..., :ratio * n_slc].reshape(bn, tq, N_KV, n_slc, ratio).sum(-1) + p_grp[..., ratio::ratio]
    blk = jnp.arange(n_slc)
    cur = (qpos // SLC_BLOCK)[:, None]
    forced = (blk == 0) | (blk == cur) | (blk == cur - 1)
    valid = blk * SLC_BLOCK <= qpos[:, None]
    score = jnp.where(forced[None, :, None, :], FORCE, jnp.where(valid[None, :, None, :], p_slc, -1.0))
    _, idx = lax.top_k(score, min(N_SEL, n_slc))
    kb = slc_rows[:, :, 0].reshape(bn, n_slc, SLC_BLOCK, N_KV, HEAD_DIM).transpose(0, 3, 1, 2, 4)
    vb = slc_rows[:, :, 1].reshape(bn, n_slc, SLC_BLOCK, N_KV, HEAD_DIM).transpose(0, 3, 1, 2, 4)
    o_slc = _slc_branch(q, qpos, kb, vb, idx)
    o = g[..., 0, None] * o_cmp + g[..., 1, None] * o_slc + g[..., 2, None] * o_win
    return o.astype(q.dtype).reshape(bn, tq, N_HEADS * HEAD_DIM) @ w_o


def setup_inputs(seed: int = 0) -> dict:
    key = jax.random.key(seed)
    keys = jax.random.split(key, 48)
    kit = iter([keys[i] for i in range(48)])
    f32 = jnp.float32

    def nrm(shape, scale=1.0):
        return jax.random.normal(next(kit), shape, f32) * scale

    n_pages = PAST_LEN // PAGE_SIZE
    n_pool = -((-5 * DEC_BATCH * n_pages) // 4)
    win_buf = min(WINDOW, PAST_LEN)
    kv_cols = N_BRANCH * 2 * N_KV * HEAD_DIM
    x_prompt = nrm((BATCH, SEQ, D_MODEL))
    x_sample = nrm((DEC_BATCH, DEC_SEQ, D_MODEL))
    c_prompt = nrm((BATCH, D_MODEL))
    c_sample = nrm((DEC_BATCH, D_MODEL))
    state_lru_h = nrm((N_A, DEC_BATCH, LRU_WIDTH), 0.5)
    state_conv = nrm((N_A, DEC_BATCH, CONV_W - 1, LRU_WIDTH))
    cache_cmp_kv = nrm((n_pool, PAGE_SIZE, 2, N_KV, HEAD_DIM))
    cache_slc_kv = nrm((n_pool, PAGE_SIZE, 2, N_KV, HEAD_DIM))
    cache_win_kv = nrm((DEC_BATCH, win_buf, 2, N_KV, HEAD_DIM))
    page_table = jax.random.permutation(next(kit), n_pool)[:DEC_BATCH * n_pages].reshape(DEC_BATCH, n_pages).astype(jnp.int32)
    ada_w = nrm((DEPTH, D_MODEL, 6 * D_MODEL), 0.5 * D_MODEL ** -0.5)
    ada_b = nrm((DEPTH, 6 * D_MODEL), 0.01)
    norm1_w = 1.0 + nrm((DEPTH, D_MODEL), 0.02)
    norm2_w = 1.0 + nrm((DEPTH, D_MODEL), 0.02)
    lru_w_in = nrm((N_A, D_MODEL, 2 * LRU_WIDTH), D_MODEL ** -0.5)
    lru_b_in = nrm((N_A, 2 * LRU_WIDTH), 0.01)
    lru_conv_w = nrm((N_A, CONV_W, LRU_WIDTH), CONV_W ** -0.5)
    lru_conv_b = nrm((N_A, LRU_WIDTH), 0.01)
    lru_gate_w = nrm((N_A, 2, LRU_BLOCKS, LRU_BW, LRU_BW), LRU_BW ** -0.5)
    lru_gate_b = nrm((N_A, 2, LRU_WIDTH), 0.01)
    a_c = jax.random.uniform(next(kit), (N_A, LRU_WIDTH), f32, 0.9, 0.999)
    a_base = a_c ** (1.0 / LRU_C)
    lru_lambda = jnp.log(a_base) - jnp.log1p(-a_base)
    lru_w_out = nrm((N_A, LRU_WIDTH, D_MODEL), LRU_WIDTH ** -0.5)
    lru_b_out = nrm((N_A, D_MODEL), 0.01)
    ffn_w13 = nrm((DEPTH, D_MODEL, 2 * D_FF), D_MODEL ** -0.5)
    ffn_w2 = nrm((DEPTH, D_FF, D_MODEL), D_FF ** -0.5)
    kv_ada_w = nrm((D_MODEL, 2 * D_MODEL), 0.5 * D_MODEL ** -0.5)
    kv_ada_b = nrm((2 * D_MODEL,), 0.01)
    kv_norm_w = 1.0 + nrm((D_MODEL,), 0.02)
    w_kv = nrm((D_MODEL, kv_cols), D_MODEL ** -0.5)
    k_norm_w = 1.0 + nrm((N_BRANCH, HEAD_DIM), 0.02)
    cmp_pe = nrm((2, CMP_BLOCK, HEAD_DIM), 0.1)
    cmp_w1 = nrm((2, CMP_BLOCK * HEAD_DIM, CMP_HIDDEN), (CMP_BLOCK * HEAD_DIM) ** -0.5)
    cmp_b1 = nrm((2, CMP_HIDDEN), 0.01)
    cmp_w2 = nrm((2, CMP_HIDDEN, HEAD_DIM), CMP_HIDDEN ** -0.5)
    cmp_b2 = nrm((2, HEAD_DIM), 0.01)
    w_qg = nrm((N_B, D_MODEL, N_HEADS * HEAD_DIM + N_BRANCH * N_HEADS), D_MODEL ** -0.5)
    q_norm_w = 1.0 + nrm((N_B, HEAD_DIM), 0.02)
    w_o = nrm((N_B, N_HEADS * HEAD_DIM, D_MODEL), (N_HEADS * HEAD_DIM) ** -0.5)
    return {'x_prompt': x_prompt, 'x_sample': x_sample, 'c_prompt': c_prompt, 'c_sample': c_sample,
            'state_lru_h': state_lru_h, 'state_conv': state_conv,
            'cache_cmp_kv': cache_cmp_kv, 'cache_slc_kv': cache_slc_kv, 'cache_win_kv': cache_win_kv,
            'page_table': page_table, 'ada_w': ada_w, 'ada_b': ada_b, 'norm1_w': norm1_w, 'norm2_w': norm2_w,
            'lru_w_in': lru_w_in, 'lru_b_in': lru_b_in, 'lru_conv_w': lru_conv_w, 'lru_conv_b': lru_conv_b,
            'lru_gate_w': lru_gate_w, 'lru_gate_b': lru_gate_b, 'lru_lambda': lru_lambda,
            'lru_w_out': lru_w_out, 'lru_b_out': lru_b_out, 'ffn_w13': ffn_w13, 'ffn_w2': ffn_w2,
            'kv_ada_w': kv_ada_w, 'kv_ada_b': kv_ada_b, 'kv_norm_w': kv_norm_w, 'w_kv': w_kv, 'k_norm_w': k_norm_w,
            'cmp_pe': cmp_pe, 'cmp_w1': cmp_w1, 'cmp_b1': cmp_b1, 'cmp_w2': cmp_w2, 'cmp_b2': cmp_b2,
            'w_qg': w_qg, 'q_norm_w': q_norm_w, 'w_o': w_o}


def reference(x_prompt, x_sample, c_prompt, c_sample, state_lru_h, state_conv,
              cache_cmp_kv, cache_slc_kv, cache_win_kv, page_table,
              ada_w, ada_b, norm1_w, norm2_w,
              lru_w_in, lru_b_in, lru_conv_w, lru_conv_b, lru_gate_w, lru_gate_b,
              lru_lambda, lru_w_out, lru_b_out, ffn_w13, ffn_w2,
              kv_ada_w, kv_ada_b, kv_norm_w, w_kv, k_norm_w,
              cmp_pe, cmp_w1, cmp_b1, cmp_w2, cmp_b2, w_qg, q_norm_w, w_o):
    past_len = page_table.shape[1] * PAGE_SIZE
    pos_p = jnp.arange(x_prompt.shape[1])
    pos_s = past_len + jnp.arange(x_sample.shape[1])
    win_buf = cache_win_kv.shape[1]
    hp, hs = x_prompt, x_sample
    lru_h_p, lru_h_s, conv_p, conv_s = [], [], [], []
    for l in range(DEPTH):
        mp = _ada(c_prompt, ada_w[l], ada_b[l], 6)
        ms = _ada(c_sample, ada_w[l], ada_b[l], 6)
        up = _modulate(hp, norm1_w[l], mp[0], mp[1])
        us = _modulate(hs, norm1_w[l], ms[0], ms[1])
        if l < N_A:
            zc = jnp.zeros((hp.shape[0], CONV_W - 1, LRU_WIDTH), hp.dtype)
            zh = jnp.zeros((hp.shape[0], LRU_WIDTH), jnp.float32)
            op, cp, hlp = _rglru_mixer(up, zc, zh, lru_w_in[l], lru_b_in[l], lru_conv_w[l], lru_conv_b[l],
                                       lru_gate_w[l], lru_gate_b[l], lru_lambda[l], lru_w_out[l], lru_b_out[l])
            osm, csm, hls = _rglru_mixer(us, state_conv[l], state_lru_h[l], lru_w_in[l], lru_b_in[l], lru_conv_w[l],
                                         lru_conv_b[l], lru_gate_w[l], lru_gate_b[l], lru_lambda[l], lru_w_out[l],
                                         lru_b_out[l])
            conv_p.append(cp)
            conv_s.append(csm)
            lru_h_p.append(hlp)
            lru_h_s.append(hls)
        else:
            j = l - N_A
            qp, gp = _nsa_query(up, w_qg[j], q_norm_w[j])
            op = _nsa_mixer(qp, gp, pos_p, rows_cmp_p, rows_slc_p, _win_prompt(qp, win_p),
                            cmp_pe, cmp_w1, cmp_b1, cmp_w2, cmp_b2, k_norm_w, w_o[j])
            qs, gs = _nsa_query(us, w_qg[j], q_norm_w[j])
            o_win_s = _win_attend(qs, pos_s, win_rows_s[:, :, 0], win_rows_s[:, :, 1], win_kpos_s)
            osm = _nsa_mixer(qs, gs, pos_s, rows_cmp_s, rows_slc_s, o_win_s,
                             cmp_pe, cmp_w1, cmp_b1, cmp_w2, cmp_b2, k_norm_w, w_o[j])
        hp = hp + mp[2] * op
        hs = hs + ms[2] * osm
        hp = hp + mp[5] * _swiglu(_modulate(hp, norm2_w[l], mp[3], mp[4]), ffn_w13[l], ffn_w2[l])
        hs = hs + ms[5] * _swiglu(_modulate(hs, norm2_w[l], ms[3], ms[4]), ffn_w13[l], ffn_w2[l])
        if l == N_A - 1:
            cmp_p, slc_p, win_p = _shared_kv(hp, c_prompt, kv_ada_w, kv_ada_b, kv_norm_w, w_kv, k_norm_w)
            cmp_s, slc_s, win_s = _shared_kv(hs, c_sample, kv_ada_w, kv_ada_b, kv_norm_w, w_kv, k_norm_w)
            rows_cmp_p = _pad_rows(cmp_p, SLC_BLOCK)
            rows_slc_p = _pad_rows(slc_p, SLC_BLOCK)
            rows_cmp_s = _pad_rows(jnp.concatenate([_gather_pages(cache_cmp_kv, page_table), cmp_s], axis=1), SLC_BLOCK)
            rows_slc_s = _pad_rows(jnp.concatenate([_gather_pages(cache_slc_kv, page_table), slc_s], axis=1), SLC_BLOCK)
            win_rows_s = jnp.concatenate([cache_win_kv, win_s], axis=1)
            win_kpos_s = past_len - win_buf + jnp.arange(win_rows_s.shape[1])
    y_prompt = hp
    y_sample = hs
    new_lru_h_p = jnp.stack(lru_h_p)
    new_lru_h_s = jnp.stack(lru_h_s)
    new_conv_p = jnp.stack(conv_p)
    new_conv_s = jnp.stack(conv_s)
    new_win_p = win_p[:, -WINDOW:]
    new_win_s = win_rows_s[:, -win_buf:]
    return (y_prompt, y_sample, new_lru_h_p, new_lru_h_s, new_conv_p, new_conv_s,
            cmp_p, cmp_s, slc_p, slc_s, new_win_p, new_win_s)
```

```python
import functools

import jax
import jax.numpy as jnp
from jax import lax
from jax.experimental import pallas as pl
from jax.experimental.pallas import tpu as pltpu

F32 = jnp.float32
BF16 = jnp.bfloat16

HEAD_DIM = 128
N_KV = 4
N_BRANCH = 3
CMP_BLOCK = 32
CMP_STRIDE = 16
SLC_BLOCK = 64
N_SEL = 16
WINDOW = 512
CONV_W = 4
LRU_C = 8.0
PAGE_SIZE = 128
EPS = 1e-6
NEG = -1e30
FORCE = 1e4
TINY = 1e-30

KV_ROW = 2 * N_KV * HEAD_DIM
CHUNK_COLS = CMP_STRIDE * KV_ROW
SLC_SHIFT = SLC_BLOCK.bit_length() - 1
SLC_PER_CHUNK = 4
KEY_CHUNK = SLC_PER_CHUNK * SLC_BLOCK
VMEM_LIMIT = 56 * 1024 * 1024


def _cparams(sem, vmem=None):
    return pltpu.CompilerParams(dimension_semantics=sem, vmem_limit_bytes=vmem or VMEM_LIMIT)


def _pick(n, cands):
    for c in cands:
        if n % c == 0:
            return c
    return n


def _dot(a, b):
    return jnp.dot(a, b, preferred_element_type=F32)


def _dot_nt(a, b):
    return lax.dot_general(a, b, (((1,), (1,)), ((), ())), preferred_element_type=F32)


def _split3(x):
    hi = x.astype(BF16)
    r1 = x - hi.astype(F32)
    mid = r1.astype(BF16)
    lo = (r1 - mid.astype(F32)).astype(BF16)
    return hi, mid, lo


def _norm_mod_kernel(x_ref, w_ref, shift_ref, scale_ref, o_ref):
    x = x_ref[...]
    y = x * lax.rsqrt(jnp.mean(x * x, axis=-1, keepdims=True) + EPS)
    y = y * w_ref[...]
    o_ref[...] = (y * (1.0 + scale_ref[0]) + shift_ref[0]).astype(o_ref.dtype)


def _norm_mod(x, w, shift, scale, out_dtype):
    m, d = x.shape
    nb, rb, _ = shift.shape
    rows_per_nb = m // nb
    tr = rows_per_nb if rb > 1 else _pick(rows_per_nb, (256, 128, 64, 32, 16, 8))
    tiles_per_nb = rows_per_nb // tr
    mod_spec = pl.BlockSpec((1, rb, d), lambda i: (i // tiles_per_nb, 0, 0))
    return pl.pallas_call(
        _norm_mod_kernel,
        grid=(m // tr,),
        in_specs=[pl.BlockSpec((tr, d), lambda i: (i, 0)),
                  pl.BlockSpec((1, d), lambda i: (0, 0)), mod_spec, mod_spec],
        out_specs=pl.BlockSpec((tr, d), lambda i: (i, 0)),
        out_shape=jax.ShapeDtypeStruct((m, d), out_dtype),
        compiler_params=_cparams(("parallel",)),
        name="norm_mod",
    )(x, w.reshape(1, d), shift, scale)


def _head_rms(acc, w, post_scale):
    outs = []
    for h in range(acc.shape[1] // HEAD_DIM):
        sl = acc[:, h * HEAD_DIM:(h + 1) * HEAD_DIM]
        y = sl * lax.rsqrt(jnp.mean(sl * sl, axis=-1, keepdims=True) + EPS)
        y = y * w
        if post_scale is not None:
            y = y * post_scale
        outs.append(y)
    return jnp.concatenate(outs, axis=1)


def _mm_kernel(*refs, nk, a_silu, has_bias, epi, n_norm_tiles, post_scale):
    it = iter(refs)
    a_ref, w_ref = next(it), next(it)
    w2_ref = next(it) if epi == "swiglu" else None
    b_ref = next(it) if has_bias else None
    res_ref, gate_ref = (next(it), next(it)) if epi == "residual" else (None, None)
    hn_ref = next(it) if epi == "headnorm" else None
    o_ref = next(it)
    acc_ref = next(it) if nk > 1 else None

    a = a_ref[...]
    if a_silu:
        a = jax.nn.silu(a)
    a = a.astype(BF16)
    part = _dot(a, w_ref[...].astype(BF16))

    def finish(acc):
        if has_bias:
            acc = acc + b_ref[...]
        if epi == "gelu":
            acc = jax.nn.gelu(acc)
        elif epi == "swiglu":
            acc = jax.nn.silu(acc) * _dot(a, w2_ref[...].astype(BF16))
        elif epi == "residual":
            acc = res_ref[...] + gate_ref[0] * acc
        elif epi == "headnorm":
            normed = _head_rms(acc, hn_ref[...], post_scale)
            acc = jnp.where(pl.program_id(1) < n_norm_tiles, normed, acc)
        o_ref[...] = acc.astype(o_ref.dtype)

    if nk == 1:
        finish(part)
    else:
        k = pl.program_id(2)

        @pl.when(k == 0)
        def _():
            acc_ref[...] = part

        @pl.when(k > 0)
        def _():
            acc_ref[...] += part

        @pl.when(k == nk - 1)
        def _():
            finish(acc_ref[...])


def _matmul(a, w, *, row0=0, col0=0, ncols=None, col0_b=None, bias=None, epi="none", res=None,
            gate=None, hn_w=None, n_norm_tiles=0, post_scale=None, a_silu=False, out_dtype=F32,
            tm=None, tn=None, tk=None, name="matmul"):
    m, kdim = a.shape
    ncols = ncols if ncols is not None else w.shape[1]
    tm = tm or min(m, 1024, m // gate.shape[0] if gate is not None and gate.shape[1] == 1 else m)
    tn = tn or _pick(ncols, (512, 256, 128))
    tk = tk or kdim
    nk = kdim // tk
    assert m % tm == 0 and ncols % tn == 0 and kdim % tk == 0 and col0 % tn == 0 and row0 % tk == 0
    assert epi != "swiglu" or (nk == 1 and col0_b % tn == 0)
    jo = col0 // tn
    ko = row0 // tk
    in_specs = [pl.BlockSpec((tm, tk), lambda i, j, k: (i, k)),
                pl.BlockSpec((tk, tn), lambda i, j, k: (k + ko, j + jo))]
    args = [a, w]
    if epi == "swiglu":
        jb = col0_b // tn
        in_specs.append(pl.BlockSpec((tk, tn), lambda i, j, k: (k + ko, j + jb)))
        args.append(w)
    if bias is not None:
        in_specs.append(pl.BlockSpec((1, tn), lambda i, j, k: (0, j + jo)))
        args.append(bias.reshape(1, -1))
    if epi == "residual":
        nb, rb, _ = gate.shape
        tiles_per_nb = (m // nb) // tm
        assert (m // nb) % tm == 0
        assert tiles_per_nb >= 1 and (rb == 1 or rb == tm)
        in_specs += [pl.BlockSpec((tm, tn), lambda i, j, k: (i, j)),
                     pl.BlockSpec((1, rb, tn), lambda i, j, k: (i // tiles_per_nb, 0, j))]
        args += [res, gate]
    if epi == "headnorm":
        in_specs.append(pl.BlockSpec((1, HEAD_DIM), lambda i, j, k: (0, 0)))
        args.append(hn_w.reshape(1, HEAD_DIM))
    kern = functools.partial(_mm_kernel, nk=nk, a_silu=a_silu, has_bias=bias is not None, epi=epi,
                             n_norm_tiles=n_norm_tiles, post_scale=post_scale)
    return pl.pallas_call(
        kern,
        grid=(m // tm, ncols // tn, nk),
        in_specs=in_specs,
        out_specs=pl.BlockSpec((tm, tn), lambda i, j, k: (i, j)),
        out_shape=jax.ShapeDtypeStruct((m, ncols), out_dtype),
        scratch_shapes=[pltpu.VMEM((tm, tn), F32)] if nk > 1 else [],
        compiler_params=_cparams(("parallel", "parallel", "arbitrary")),
        name=name,
    )(*args)


def _gate_kernel(wg_ref, u_ref, o_ref):
    o_ref[...] = jax.nn.sigmoid(_dot_nt(wg_ref[...].astype(BF16), u_ref[...].astype(BF16)))


def _branch_gates_t(u, wg_t):
    m, kdim = u.shape
    r = wg_t.shape[0]
    tm = _pick(m, (1024, 512, 256, 128))
    return pl.pallas_call(
        _gate_kernel,
        grid=(m // tm,),
        in_specs=[pl.BlockSpec((r, kdim), lambda i: (0, 0)),
                  pl.BlockSpec((tm, kdim), lambda i: (i, 0))],
        out_specs=pl.BlockSpec((r, tm), lambda i: (0, i)),
        out_shape=jax.ShapeDtypeStruct((r, m), F32),
        compiler_params=_cparams(("parallel",)),
        name="branch_gates",
    )(wg_t, u)


def _lru_coeffs(xb, gw_r, gw_i, gb, lam):
    xb16 = xb.astype(BF16)
    r = jax.nn.sigmoid(_dot(xb16, gw_r) + gb[0:1, :])
    gi = jax.nn.sigmoid(_dot(xb16, gw_i) + gb[1:2, :])
    z = -lam
    softplus = jnp.maximum(z, 0.0) + jnp.log1p(jnp.exp(-jnp.abs(z)))
    log_a = -LRU_C * r * softplus
    a = jnp.exp(log_a)
    return a, jnp.sqrt(-jnp.tanh(log_a) * (1.0 + a * a)) * gi * xb


def _lru_step_kernel(x_ref, gy_ref, cprev_ref, hprev_ref, cw_ref, cb_ref, gw_ref, gb_ref, lam_ref,
                     hg_ref, nconv_ref, h_ref, *, bw, nbp):
    hist = CONV_W - 1
    x = x_ref[...]
    conv = cprev_ref[0] * cw_ref[0:1, :]
    for k in range(1, hist):
        conv = conv + cprev_ref[k] * cw_ref[k:k + 1, :]
    xc = cb_ref[...] + (conv + x * cw_ref[hist:hist + 1, :])
    for blk in range(nbp):
        cols = slice(blk * bw, (blk + 1) * bw)
        a, b = _lru_coeffs(xc[:, cols], gw_ref[0, blk].astype(BF16), gw_ref[1, blk].astype(BF16),
                           gb_ref[:, cols], lam_ref[:, cols])
        h = a * hprev_ref[:, cols] + b
        h_ref[:, cols] = h
        hg_ref[:, cols] = (h * gy_ref[:, cols].astype(F32)).astype(hg_ref.dtype)
    for k in range(hist - 1):
        nconv_ref[k] = cprev_ref[k + 1]
    nconv_ref[hist - 1] = x


def _lru_step(x, gy, conv_prev, h_prev, conv_w, conv_b, gate_w, gate_b, lam):
    b, w = x.shape
    nblk, bw = gate_w.shape[1], gate_w.shape[2]
    nbp = _pick(nblk, (4, 2, 1))
    cw = nbp * bw
    hist = CONV_W - 1
    rows = pl.BlockSpec((b, cw), lambda c: (0, c))
    hrows = pl.BlockSpec((hist, b, cw), lambda c: (0, 0, c))
    chan = lambda r: pl.BlockSpec((r, cw), lambda c: (0, c))
    return pl.pallas_call(
        functools.partial(_lru_step_kernel, bw=bw, nbp=nbp),
        grid=(w // cw,),
        in_specs=[rows, rows, hrows, rows, chan(CONV_W), chan(1),
                  pl.BlockSpec((2, nbp, bw, bw), lambda c: (0, c, 0, 0)), chan(2), chan(1)],
        out_specs=[rows, hrows, rows],
        out_shape=[jax.ShapeDtypeStruct((b, w), gy.dtype), jax.ShapeDtypeStruct((hist, b, w), F32),
                   jax.ShapeDtypeStruct((b, w), F32)],
        compiler_params=_cparams(("parallel",)),
        name="rglru_step",
    )(x, gy, conv_prev, h_prev, conv_w, conv_b.reshape(1, w), gate_w, gate_b, lam.reshape(1, w))


def _lru_kernel(x_ref, gy_ref, cprev_ref, hprev_ref, cw_ref, cb_ref, gw_ref, gb_ref, lam_ref,
                hg_ref, nconv_ref, hlast_ref,
                xbuf, gw_scr, a_scr, b_scr, hs_scr, h_scr, *, tt, nt, bw, nbp):
    t = pl.program_id(2)
    hist = CONV_W - 1
    base = 8

    @pl.when(t == 0)
    def _():
        xbuf[base - hist:base, :] = cprev_ref[0]
        h_scr[...] = hprev_ref[0]
        gw_scr[...] = gw_ref[...].astype(BF16)

    @pl.when(t > 0)
    def _():
        xbuf[base - hist:base, :] = xbuf[base + tt - hist:base + tt, :]

    xbuf[base:base + tt, :] = x_ref[0]
    conv = xbuf[base - hist:base - hist + tt, :] * cw_ref[0:1, :]
    for k in range(1, CONV_W):
        conv = conv + xbuf[base - hist + k:base - hist + k + tt, :] * cw_ref[k:k + 1, :]
    xc = cb_ref[...] + conv

    for blk in range(nbp):
        cols = slice(blk * bw, (blk + 1) * bw)
        a_scr[:, cols], b_scr[:, cols] = _lru_coeffs(
            xc[:, cols], gw_scr[0, blk], gw_scr[1, blk], gb_ref[:, cols], lam_ref[:, cols])

    def step(i, h):
        h = a_scr[pl.ds(i, 1), :] * h + b_scr[pl.ds(i, 1), :]
        hs_scr[pl.ds(i, 1), :] = h
        return h

    h_scr[...] = lax.fori_loop(0, tt, step, h_scr[...], unroll=min(tt, 8))
    hg_ref[0] = (hs_scr[...] * gy_ref[0].astype(F32)).astype(hg_ref.dtype)

    @pl.when(t == nt - 1)
    def _():
        nconv_ref[0] = xbuf[base + tt - hist:base + tt, :]
        hlast_ref[0] = h_scr[...]


def _lru(x, gy, conv_prev, h_prev, conv_w, conv_b, gate_w, gate_b, lam):
    b, t, w = x.shape
    nblk, bw = gate_w.shape[1], gate_w.shape[2]
    nbp = _pick(nblk, (4, 2, 1))
    cw = nbp * bw
    tt = _pick(t, (256, 128, 64, 32, 16, 8))
    nt = t // tt
    hist = CONV_W - 1
    kern = functools.partial(_lru_kernel, tt=tt, nt=nt, bw=bw, nbp=nbp)
    row = lambda bi, c, ti: (bi, ti, c)
    fixed = lambda bi, c, ti: (bi, 0, c)
    chan = lambda bi, c, ti: (0, c)
    hg, nconv, hlast = pl.pallas_call(
        kern,
        grid=(b, w // cw, nt),
        in_specs=[pl.BlockSpec((1, tt, cw), row), pl.BlockSpec((1, tt, cw), row),
                  pl.BlockSpec((1, hist, cw), fixed), pl.BlockSpec((1, 1, cw), fixed),
                  pl.BlockSpec((CONV_W, cw), chan), pl.BlockSpec((1, cw), chan),
                  pl.BlockSpec((2, nbp, bw, bw), lambda bi, c, ti: (0, c, 0, 0)),
                  pl.BlockSpec((2, cw), chan), pl.BlockSpec((1, cw), chan)],
        out_specs=[pl.BlockSpec((1, tt, cw), row), pl.BlockSpec((1, hist, cw), fixed),
                   pl.BlockSpec((1, 1, cw), fixed)],
        out_shape=[jax.ShapeDtypeStruct((b, t, w), BF16),
                   jax.ShapeDtypeStruct((b, hist, w), F32),
                   jax.ShapeDtypeStruct((b, 1, w), F32)],
        scratch_shapes=[pltpu.VMEM((8 + tt, cw), F32), pltpu.VMEM((2, nbp, bw, bw), BF16),
                        pltpu.VMEM((tt, cw), F32), pltpu.VMEM((tt, cw), F32),
                        pltpu.VMEM((tt, cw), F32), pltpu.VMEM((1, cw), F32)],
        compiler_params=_cparams(("parallel", "parallel", "arbitrary")),
        name="rglru",
    )(x, gy, conv_prev, h_prev.reshape(b, 1, w), conv_w, conv_b.reshape(1, w), gate_w, gate_b,
      lam.reshape(1, w))
    return hg, nconv, hlast.reshape(b, w)


def _cmp_pq_kernel(*refs, n_src):
    refs = refs[len(refs) - (n_src + 2):]
    src = refs[:n_src]
    w1_ref, o_ref = refs[n_src], refs[n_src + 1]
    half = CMP_STRIDE * HEAD_DIM
    nc = n_src * src[0].shape[1]
    for kv in range(2):
        w_pq = jnp.concatenate([w1_ref[kv, :half, :], w1_ref[kv, half:, :]], axis=1).astype(BF16)
        rows = []
        for g in range(N_KV):
            off = kv * N_KV * HEAD_DIM + g * HEAD_DIM
            for s in src:
                rows.append(jnp.concatenate(
                    [s[0, :, r * KV_ROW + off:r * KV_ROW + off + HEAD_DIM].astype(BF16)
                     for r in range(CMP_STRIDE)], axis=1))
        pq = _dot(jnp.concatenate(rows, axis=0), w_pq)
        for g in range(N_KV):
            c0 = (kv * N_KV + g) * 2 * HEAD_DIM
            o_ref[0, :, c0:c0 + 2 * HEAD_DIM] = pq[g * nc:(g + 1) * nc, :]


def _cmp_pq(rows_view, w1, *, page_table=None, pages_per_step=16):
    out_cols = 2 * N_KV * 2 * HEAD_DIM
    w_spec_shape = w1.shape
    if page_table is None:
        nb, nch, _ = rows_view.shape
        tc = _pick(nch, (128, 64, 32, 16, 8))
        return pl.pallas_call(
            functools.partial(_cmp_pq_kernel, n_src=1),
            grid=(nb, nch // tc),
            in_specs=[pl.BlockSpec((1, tc, CHUNK_COLS), lambda b, c: (b, c, 0)),
                      pl.BlockSpec(w_spec_shape, lambda b, c: (0, 0, 0))],
            out_specs=pl.BlockSpec((1, tc, out_cols), lambda b, c: (b, c, 0)),
            out_shape=jax.ShapeDtypeStruct((nb, nch, out_cols), F32),
            compiler_params=_cparams(("parallel", "parallel")),
            name="cmp_pq",
        )(rows_view, w1)
    nb, n_pages = page_table.shape
    cpp = rows_view.shape[1]
    pps = _pick(n_pages, (pages_per_step, 8, 4, 2, 1))
    src_specs = [pl.BlockSpec((1, cpp, CHUNK_COLS),
                              functools.partial(lambda b, s, pt, p: (pt[b, s * pps + p], 0, 0), p=p))
                 for p in range(pps)]
    grid_spec = pltpu.PrefetchScalarGridSpec(
        num_scalar_prefetch=1,
        grid=(nb, n_pages // pps),
        in_specs=src_specs + [pl.BlockSpec(w_spec_shape, lambda b, s, pt: (0, 0, 0))],
        out_specs=pl.BlockSpec((1, pps * cpp, out_cols), lambda b, s, pt: (b, s, 0)),
    )
    return pl.pallas_call(
        functools.partial(_cmp_pq_kernel, n_src=pps),
        grid_spec=grid_spec,
        out_shape=jax.ShapeDtypeStruct((nb, n_pages * cpp, out_cols), F32),
        compiler_params=_cparams(("parallel", "parallel")),
        name="cmp_pq_paged",
    )(page_table, *([rows_view] * pps), w1)


def _cmp_finish_kernel(pq_ref, pe_ref, w1_ref, b1_ref, w2_ref, b2_ref, kn_ref, kc_ref, vct_ref, *, nch):
    is_block = lax.broadcasted_iota(jnp.int32, (nch, 1), 0) < nch - 1
    for kv in range(2):
        pe8 = jnp.broadcast_to(pe_ref[kv], (8, pe_ref.shape[2])).astype(BF16)
        const = _dot(pe8, w1_ref[kv].astype(BF16))[0:1, :] + b1_ref[kv]
        w2 = w2_ref[kv].astype(BF16)
        for g in range(N_KV):
            c0 = (kv * N_KV + g) * 2 * HEAD_DIM
            nxt = jnp.concatenate([pq_ref[0, 1:nch, c0 + HEAD_DIM:c0 + 2 * HEAD_DIM],
                                   jnp.zeros((1, HEAD_DIM), F32)], axis=0)
            hid = pq_ref[0, :, c0:c0 + HEAD_DIM] + nxt + const
            out = _dot(jax.nn.gelu(hid).astype(BF16), w2) + b2_ref[kv]
            if kv == 0:
                out = out * lax.rsqrt(jnp.mean(out * out, axis=-1, keepdims=True) + EPS) * kn_ref[...]
            out = jnp.where(is_block, out, 0.0)
            if kv == 0:
                kc_ref[0, g] = out.astype(kc_ref.dtype)
            else:
                vct_ref[0, g] = out.T.astype(vct_ref.dtype)


def _cmp_finish(pq, pe, w1, b1, w2, b2, k_norm):
    nb, nch, cols = pq.shape
    full = lambda shape: pl.BlockSpec(shape, lambda b: (0,) * len(shape))
    return pl.pallas_call(
        functools.partial(_cmp_finish_kernel, nch=nch),
        grid=(nb,),
        in_specs=[pl.BlockSpec((1, nch, cols), lambda b: (b, 0, 0)),
                  full((2, 1, CMP_BLOCK * HEAD_DIM)), full(w1.shape), full((2, 1, HEAD_DIM)),
                  full(w2.shape), full((2, 1, HEAD_DIM)), full((1, HEAD_DIM))],
        out_specs=[pl.BlockSpec((1, N_KV, nch, HEAD_DIM), lambda b: (b, 0, 0, 0)),
                   pl.BlockSpec((1, N_KV, HEAD_DIM, nch), lambda b: (b, 0, 0, 0))],
        out_shape=[jax.ShapeDtypeStruct((nb, N_KV, nch, HEAD_DIM), BF16),
                   jax.ShapeDtypeStruct((nb, N_KV, HEAD_DIM, nch), BF16)],
        compiler_params=_cparams(("parallel",)),
        name="cmp_finish",
    )(pq, pe.reshape(2, 1, -1), w1, b1.reshape(2, 1, -1), w2, b2.reshape(2, 1, -1),
      k_norm.reshape(1, HEAD_DIM))


def _online_step(s, carry, v_t):
    m, l, acc = carry
    m_new = jnp.maximum(m, jnp.max(s, axis=0, keepdims=True))
    alpha = jnp.exp(m - m_new)
    p = jnp.exp(s - m_new)
    l = alpha * l + jnp.sum(p, axis=0, keepdims=True)
    acc = alpha * acc + _dot(v_t, p.astype(BF16))
    return m_new, l, acc


def _nsa_prompt_kernel(q_ref, gt_ref, kc_ref, vct_ref, ks_ref, vs_ref, kw_ref, vw_ref, o_ref,
                       ks_scr, vst_scr, kw_scr, vwt_scr, qt_scr, ocmp_scr, score_scr, sel_scr,
                       bias_scr, wbias_scr, ot_scr, *, tq, t_len, hpg, n_cmp_pad):
    g = pl.program_id(1)
    qi = pl.program_id(2)
    n_chunks = t_len // KEY_CHUNK
    n_slc = t_len // SLC_BLOCK
    win_chunks = WINDOW // KEY_CHUNK

    @pl.when(qi == 0)
    def _():
        for c in range(n_chunks):
            rows = slice(c * KEY_CHUNK, (c + 1) * KEY_CHUNK)
            ks_scr[rows, :] = ks_ref[0, rows, :].astype(BF16)
            kw_scr[rows, :] = kw_ref[0, rows, :].astype(BF16)
            vst_scr[:, rows] = vs_ref[0, rows, :].T.astype(BF16)
            vwt_scr[:, rows] = vw_ref[0, rows, :].T.astype(BF16)

    qt_scr[...] = q_ref[...].astype(F32).T.astype(BF16)
    qpos = qi * tq + lax.broadcasted_iota(jnp.int32, (1, tq), 1)

    cmp_idx = lax.broadcasted_iota(jnp.int32, (n_cmp_pad, tq), 0)
    cmp_mask = cmp_idx * CMP_STRIDE + (CMP_BLOCK - 1) <= qpos
    kc = kc_ref[0, 0]
    vct = vct_ref[0, 0]

    def cmp_head(h, p_grp):
        rows = pl.ds(pl.multiple_of(h * HEAD_DIM, HEAD_DIM), HEAD_DIM)
        s = jnp.where(cmp_mask, _dot(kc, qt_scr[rows, :]), NEG)
        m = jnp.max(s, axis=0, keepdims=True)
        e = jnp.where(cmp_mask, jnp.exp(s - m), 0.0)
        p = e / jnp.maximum(jnp.sum(e, axis=0, keepdims=True), TINY)
        ocmp_scr[rows, :] = _dot(vct, p.astype(BF16))
        return p_grp + p

    p_grp = lax.fori_loop(0, hpg, cmp_head, jnp.zeros((n_cmp_pad, tq), F32))

    ratio = SLC_BLOCK // CMP_STRIDE
    jj = lax.broadcasted_iota(jnp.int32, (n_slc, n_cmp_pad), 0)
    mm = lax.broadcasted_iota(jnp.int32, (n_slc, n_cmp_pad), 1)
    pool = ((mm >= ratio * jj - 1) & (mm <= ratio * jj + ratio - 1)).astype(BF16)
    p_slc = sum(_dot(pool, piece) for piece in _split3(p_grp))
    blk = lax.broadcasted_iota(jnp.int32, (n_slc, tq), 0)
    cur = lax.shift_right_arithmetic(qpos, SLC_SHIFT)
    forced = (blk == 0) | (blk == cur) | (blk == cur - 1)
    valid = blk * SLC_BLOCK <= qpos
    score = jnp.where(forced, FORCE, jnp.where(valid, p_slc, -1.0))
    score_scr[...] = score

    def rank_step(j, rank):
        other = score_scr[pl.ds(j, 1), :]
        beats = (other > score) | ((other == score) & (j < blk))
        return rank + beats.astype(jnp.int32)

    rank = lax.fori_loop(0, n_slc, rank_step, jnp.zeros((n_slc, tq), jnp.int32))
    sel_scr[...] = (rank < min(N_SEL, n_slc)).astype(F32)

    def slc_bias(c, _):
        for r in range(SLC_PER_CHUNK):
            sel_row = sel_scr[pl.ds(c * SLC_PER_CHUNK + r, 1), :]
            kpos = c * KEY_CHUNK + r * SLC_BLOCK + lax.broadcasted_iota(jnp.int32, (SLC_BLOCK, tq), 0)
            ok = (sel_row > 0.5) & (kpos <= qpos)
            bias_scr[pl.ds(pl.multiple_of(c * KEY_CHUNK + r * SLC_BLOCK, SLC_BLOCK), SLC_BLOCK), :] = (
                jnp.where(ok, 0.0, NEG))
        return 0

    n_live = (qi * tq + tq + KEY_CHUNK - 1) // KEY_CHUNK
    lax.fori_loop(0, n_live, slc_bias, 0)

    kl = lax.broadcasted_iota(jnp.int32, (KEY_CHUNK, tq), 0)
    for c in range(tq // KEY_CHUNK + win_chunks):
        dlt = (win_chunks - c) * KEY_CHUNK + lax.broadcasted_iota(jnp.int32, (KEY_CHUNK, tq), 1) - kl
        wbias_scr[c] = jnp.where((dlt >= 0) & (dlt < WINDOW), 0.0, NEG)

    first_key_chunk = qi * (tq // KEY_CHUNK) - win_chunks

    def head(h, _):
        rows = pl.ds(pl.multiple_of(h * HEAD_DIM, HEAD_DIM), HEAD_DIM)
        q_t = qt_scr[rows, :]
        init = (jnp.full((1, tq), NEG, F32), jnp.zeros((1, tq), F32), jnp.zeros((HEAD_DIM, tq), F32))

        def slc_step(c, carry):
            keys = pl.ds(pl.multiple_of(c * KEY_CHUNK, KEY_CHUNK), KEY_CHUNK)
            s = _dot(ks_scr[keys, :], q_t) + bias_scr[keys, :]
            return _online_step(s, carry, vst_scr[:, keys])

        _, l_s, acc_s = lax.fori_loop(0, n_live, slc_step, init)

        def win_step(cc, carry):
            c = tq // KEY_CHUNK + win_chunks - 1 - cc
            kc_idx = first_key_chunk + c
            keys = pl.ds(pl.multiple_of(kc_idx * KEY_CHUNK, KEY_CHUNK), KEY_CHUNK)
            s = _dot(kw_scr[keys, :], q_t) + wbias_scr[c]
            return _online_step(s, carry, vwt_scr[:, keys])

        n_win = jnp.minimum(first_key_chunk, 0) + tq // KEY_CHUNK + win_chunks
        _, l_w, acc_w = lax.fori_loop(0, n_win, win_step, init)

        grow = (g * hpg + h) * N_BRANCH
        o = (gt_ref[pl.ds(grow, 1), :] * ocmp_scr[rows, :]
             + gt_ref[pl.ds(grow + 1, 1), :] * (acc_s / jnp.maximum(l_s, TINY))
             + gt_ref[pl.ds(grow + 2, 1), :] * (acc_w / jnp.maximum(l_w, TINY)))
        ot_scr[rows, :] = o
        return 0

    lax.fori_loop(0, hpg, head, 0)
    o_ref[...] = ot_scr[...].T.astype(o_ref.dtype)


def _nsa_prompt(q, gates_t, kc, vct, slc_rows, win_rows, bsz, t_len):
    m, qcols = q.shape
    hpg = qcols // (N_KV * HEAD_DIM)
    gw = hpg * HEAD_DIM
    tq = KEY_CHUNK
    assert t_len % KEY_CHUNK == 0 and WINDOW % KEY_CHUNK == 0 and tq % KEY_CHUNK == 0
    nq = t_len // tq
    n_cmp_pad = kc.shape[2]
    n_slc = t_len // SLC_BLOCK
    n_wb = tq // KEY_CHUNK + WINDOW // KEY_CHUNK
    kern = functools.partial(_nsa_prompt_kernel, tq=tq, t_len=t_len, hpg=hpg, n_cmp_pad=n_cmp_pad)
    kv_k = lambda b, g, i: (b, 0, g)
    kv_v = lambda b, g, i: (b, 0, N_KV + g)
    return pl.pallas_call(
        kern,
        grid=(bsz, N_KV, nq),
        in_specs=[pl.BlockSpec((tq, gw), lambda b, g, i: (b * nq + i, g)),
                  pl.BlockSpec((gates_t.shape[0], tq), lambda b, g, i: (0, b * nq + i)),
                  pl.BlockSpec((1, 1, n_cmp_pad, HEAD_DIM), lambda b, g, i: (b, g, 0, 0)),
                  pl.BlockSpec((1, 1, HEAD_DIM, n_cmp_pad), lambda b, g, i: (b, g, 0, 0)),
                  pl.BlockSpec((1, t_len, HEAD_DIM), kv_k), pl.BlockSpec((1, t_len, HEAD_DIM), kv_v),
                  pl.BlockSpec((1, t_len, HEAD_DIM), kv_k), pl.BlockSpec((1, t_len, HEAD_DIM), kv_v)],
        out_specs=pl.BlockSpec((tq, gw), lambda b, g, i: (b * nq + i, g)),
        out_shape=jax.ShapeDtypeStruct((m, qcols), BF16),
        scratch_shapes=[pltpu.VMEM((t_len, HEAD_DIM), BF16), pltpu.VMEM((HEAD_DIM, t_len), BF16),
                        pltpu.VMEM((t_len, HEAD_DIM), BF16), pltpu.VMEM((HEAD_DIM, t_len), BF16),
                        pltpu.VMEM((gw, tq), BF16), pltpu.VMEM((gw, tq), F32),
                        pltpu.VMEM((n_slc, tq), F32), pltpu.VMEM((n_slc, tq), F32),
                        pltpu.VMEM((t_len, tq), F32), pltpu.VMEM((n_wb, KEY_CHUNK, tq), F32),
                        pltpu.VMEM((gw, tq), F32)],
        compiler_params=_cparams(("parallel", "parallel", "arbitrary")),
        name="nsa_prompt",
    )(q, gates_t, kc, vct, slc_rows, slc_rows, win_rows, win_rows)


def _nsa_sample_select_kernel(q_ref, kc_ref, vct_ref, ocmp_ref, idx_ref, *, qpos, n_slc, nsp):
    q = q_ref[0, 0].astype(BF16)
    ncp = kc_ref.shape[2]
    s = _dot_nt(q, kc_ref[0, 0])
    m_idx = lax.broadcasted_iota(jnp.int32, (1, ncp), 1)
    mask = m_idx * CMP_STRIDE + (CMP_BLOCK - 1) <= qpos
    s = jnp.where(mask, s, NEG)
    e = jnp.where(mask, jnp.exp(s - jnp.max(s, axis=-1, keepdims=True)), 0.0)
    p = e / jnp.maximum(jnp.sum(e, axis=-1, keepdims=True), TINY)
    ocmp_ref[0, 0] = _dot_nt(p.astype(BF16), vct_ref[0, 0])

    ratio = SLC_BLOCK // CMP_STRIDE
    p_grp = jnp.broadcast_to(jnp.sum(p, axis=0, keepdims=True), (8, ncp))
    mm = lax.broadcasted_iota(jnp.int32, (ncp, nsp), 0)
    jj = lax.broadcasted_iota(jnp.int32, (ncp, nsp), 1)
    pool_t = ((mm >= ratio * jj - 1) & (mm <= ratio * jj + ratio - 1)).astype(BF16)
    p_slc = sum(_dot(piece, pool_t) for piece in _split3(p_grp))[0:1, :]
    blk = lax.broadcasted_iota(jnp.int32, (1, nsp), 1)
    cur = qpos // SLC_BLOCK
    forced = (blk == 0) | (blk == cur) | (blk == cur - 1)
    score = jnp.where(forced, FORCE, jnp.where(blk * SLC_BLOCK <= qpos, p_slc, -1.0))
    score = jnp.where(blk < n_slc, score, -2.0)

    ii = lax.broadcasted_iota(jnp.int32, (nsp, nsp), 0)
    jx = lax.broadcasted_iota(jnp.int32, (nsp, nsp), 1)
    mine = jnp.broadcast_to(score, (nsp, nsp))
    other = mine.T
    beats = (other > mine) | ((other == mine) & (ii < jx))
    rank = jnp.sum(beats.astype(F32), axis=0, keepdims=True)
    sel = (rank < float(N_SEL)).astype(F32)
    sel_other = jnp.broadcast_to(sel, (nsp, nsp)).T
    pos = jnp.sum(jnp.where(ii < jx, sel_other, 0.0), axis=0, keepdims=True)
    kk = lax.broadcasted_iota(jnp.int32, (N_SEL, nsp), 0).astype(F32)
    jrow = lax.broadcasted_iota(jnp.int32, (N_SEL, nsp), 1).astype(F32)
    hit = (sel > 0.5) & (pos == kk)
    idx = jnp.sum(jnp.where(hit, jrow, 0.0), axis=1, keepdims=True)
    idx_ref[0, 0] = jnp.broadcast_to(idx, (N_SEL, HEAD_DIM)).astype(jnp.int32)


def _nsa_sample_select(q, kc, vct, qpos, n_slc):
    dbs, _, hpg, _ = q.shape
    ncp = kc.shape[2]
    nsp = -(-n_slc // HEAD_DIM) * HEAD_DIM
    assert n_slc >= N_SEL
    blk4 = lambda s2, s3: pl.BlockSpec((1, 1, s2, s3), lambda b, g: (b, g, 0, 0))
    ocmp, idx = pl.pallas_call(
        functools.partial(_nsa_sample_select_kernel, qpos=qpos, n_slc=n_slc, nsp=nsp),
        grid=(dbs, N_KV),
        in_specs=[blk4(hpg, HEAD_DIM), blk4(ncp, HEAD_DIM), blk4(HEAD_DIM, ncp)],
        out_specs=[blk4(hpg, HEAD_DIM), blk4(N_SEL, HEAD_DIM)],
        out_shape=[jax.ShapeDtypeStruct((dbs, N_KV, hpg, HEAD_DIM), F32),
                   jax.ShapeDtypeStruct((dbs, N_KV, N_SEL, HEAD_DIM), jnp.int32)],
        compiler_params=_cparams(("parallel", "parallel")),
        name="nsa_sample_select",
    )(q, kc, vct)
    return ocmp, idx[..., 0]


def _nsa_sample_attend_kernel(pt_ref, idx_ref, q_ref, gate_ref, ocmp_ref, snew_ref, wnew_ref,
                              kw_ref, vw_ref, *rest, qpos, n_cached, win_buf):
    k_blocks, v_blocks, o_ref = rest[:N_SEL], rest[N_SEL:2 * N_SEL], rest[2 * N_SEL]
    b, g = pl.program_id(0), pl.program_id(1)
    q = q_ref[0, 0].astype(BF16)
    qf = q.astype(F32)

    def attend(s, mask, v, k_new, v_new):
        s_self = jnp.sum(qf * k_new.astype(BF16).astype(F32), axis=-1, keepdims=True)
        s = jnp.where(mask, s, NEG)
        m = jnp.maximum(jnp.max(s, axis=-1, keepdims=True), s_self)
        e = jnp.where(mask, jnp.exp(s - m), 0.0)
        e_self = jnp.exp(s_self - m)
        l = jnp.sum(e, axis=-1, keepdims=True) + e_self
        acc = _dot(e.astype(BF16), v) + e_self.astype(BF16).astype(F32) * v_new.astype(BF16).astype(F32)
        return acc / jnp.maximum(l, TINY)

    n_keys = N_SEL * SLC_BLOCK
    k_all = jnp.concatenate([kb[0] for kb in k_blocks], axis=0).astype(BF16)
    v_all = jnp.concatenate([vb[0] for vb in v_blocks], axis=0).astype(BF16)
    lane = lax.broadcasted_iota(jnp.int32, (1, n_keys), 1)
    slot = lax.shift_right_arithmetic(lane, SLC_SHIFT)
    blk_id = jnp.zeros((1, n_keys), jnp.int32)
    for k in range(N_SEL):
        blk_id = jnp.where(slot == k, idx_ref[(b * N_KV + g) * N_SEL + k], blk_id)
    kpos = blk_id * SLC_BLOCK + (lane & (SLC_BLOCK - 1))
    slc_mask = (blk_id < n_cached) & (kpos <= qpos)
    o_slc = attend(_dot_nt(q, k_all), slc_mask, v_all,
                   snew_ref[0, pl.ds(g, 1), :], snew_ref[0, pl.ds(N_KV + g, 1), :])

    wi = lax.broadcasted_iota(jnp.int32, (1, win_buf), 1)
    dlt = win_buf - wi
    win_mask = (dlt >= 0) & (dlt < WINDOW) & (qpos - dlt >= 0)
    o_win = attend(_dot_nt(q, kw_ref[0].astype(BF16)), win_mask, vw_ref[0].astype(BF16),
                   wnew_ref[0, pl.ds(g, 1), :], wnew_ref[0, pl.ds(N_KV + g, 1), :])

    gate = gate_ref[0, 0]
    o_ref[0, 0] = gate[:, 0:1] * ocmp_ref[0, 0] + gate[:, 1:2] * o_slc + gate[:, 2:3] * o_win


def _nsa_sample_attend(q, gates, ocmp, idx, page_table, slc_cache, slc_new, win_cache, win_new, qpos):
    dbs, _, hpg, _ = q.shape
    win_buf = win_cache.shape[1]
    n_cached = page_table.shape[1] * PAGE_SIZE // SLC_BLOCK
    bpp = PAGE_SIZE // SLC_BLOCK
    cache_blocks = slc_cache.reshape(-1, SLC_BLOCK, KV_ROW)

    def blk_map(b, g, pt, ix, *, k, col):
        blk = jnp.minimum(ix[(b * N_KV + g) * N_SEL + k], n_cached - 1)
        return pt[b, blk // bpp] * bpp + blk % bpp, 0, col * N_KV + g

    blk4 = lambda s2, s3: pl.BlockSpec((1, 1, s2, s3), lambda b, g, pt, ix: (b, g, 0, 0))
    new_spec = pl.BlockSpec((1, 2 * N_KV, HEAD_DIM), lambda b, g, pt, ix: (b, 0, 0))
    wk = pl.BlockSpec((1, win_buf, HEAD_DIM), lambda b, g, pt, ix: (b, 0, g))
    wv = pl.BlockSpec((1, win_buf, HEAD_DIM), lambda b, g, pt, ix: (b, 0, N_KV + g))
    gathered = [pl.BlockSpec((1, SLC_BLOCK, HEAD_DIM), functools.partial(blk_map, k=k, col=col))
                for col in range(2) for k in range(N_SEL)]
    grid_spec = pltpu.PrefetchScalarGridSpec(
        num_scalar_prefetch=2,
        grid=(dbs, N_KV),
        in_specs=[blk4(hpg, HEAD_DIM), blk4(hpg, N_BRANCH), blk4(hpg, HEAD_DIM), new_spec, new_spec,
                  wk, wv] + gathered,
        out_specs=blk4(hpg, HEAD_DIM),
    )
    return pl.pallas_call(
        functools.partial(_nsa_sample_attend_kernel, qpos=qpos, n_cached=n_cached, win_buf=win_buf),
        grid_spec=grid_spec,
        out_shape=jax.ShapeDtypeStruct((dbs, N_KV, hpg, HEAD_DIM), F32),
        compiler_params=_cparams(("parallel", "parallel")),
        name="nsa_sample_attend",
    )(page_table, idx.reshape(-1), q, gates, ocmp,
      slc_new.reshape(dbs, 2 * N_KV, HEAD_DIM), win_new.reshape(dbs, 2 * N_KV, HEAD_DIM),
      win_cache, win_cache, *([cache_blocks] * (2 * N_SEL)))


def kernel(x_prompt, x_sample, c_prompt, c_sample, state_lru_h, state_conv, cache_cmp_kv, cache_slc_kv, cache_win_kv, page_table, ada_w, ada_b, norm1_w, norm2_w, lru_w_in, lru_b_in, lru_conv_w, lru_conv_b, lru_gate_w, lru_gate_b, lru_lambda, lru_w_out, lru_b_out, ffn_w13, ffn_w2, kv_ada_w, kv_ada_b, kv_norm_w, w_kv, k_norm_w, cmp_pe, cmp_w1, cmp_b1, cmp_w2, cmp_b2, w_qg, q_norm_w, w_o):
    bsz, t_len, d = x_prompt.shape
    dbs, dec_seq, _ = x_sample.shape
    depth = ada_w.shape[0]
    n_a = lru_w_in.shape[0]
    assert dec_seq == 1 and depth == 2 and n_a == 1 and w_qg.shape[0] == 1
    lw = lru_w_in.shape[2] // 2
    d_ff = ffn_w2.shape[1]
    n_pages = page_table.shape[1]
    past_len = n_pages * PAGE_SIZE
    qcols = w_o.shape[1]
    hpg = qcols // (N_KV * HEAD_DIM)
    hist = CONV_W - 1
    branch_cols = KV_ROW

    n_c = bsz + dbs
    c_all = jnp.pad(jnp.concatenate([c_prompt, c_sample], axis=0), ((0, (-n_c) % 8), (0, 0)))
    ada_w2 = ada_w.reshape(depth * d, -1)
    mods = [_matmul(c_all, ada_w2, row0=l * d, bias=ada_b[l], a_silu=True, tn=1024, name="ada")
            for l in range(depth)]
    kv_mod = _matmul(c_all, kv_ada_w, bias=kv_ada_b, a_silu=True, tn=1024, name="kv_ada")

    def split_mod(mat, n, prompt):
        parts = [mat[:, i * d:(i + 1) * d] for i in range(n)]
        if prompt:
            return [p[:bsz].reshape(bsz, 1, d) for p in parts]
        return [p[bsz:n_c].reshape(1, dbs, d) for p in parts]

    w13 = ffn_w13.reshape(depth * d, 2 * d_ff)
    w2 = ffn_w2.astype(BF16).reshape(depth * d_ff, d)
    tk2 = _pick(d_ff, (d_ff // 2,)) if d_ff > 4096 else d_ff
    w_in, w_out = lru_w_in.reshape(d, 2 * lw), lru_w_out.reshape(lw, d)
    wq = w_qg.reshape(d, -1)
    wg_t = wq[:, qcols:].T
    wo = w_o.reshape(qcols, d)
    pe = cmp_pe.reshape(2, -1)

    def ffn(x, l, mod, act_dtype):
        u = _norm_mod(x, norm2_w[l], mod[3], mod[4], act_dtype)
        act = _matmul(u, w13, row0=l * d, col0=0, col0_b=d_ff, ncols=d_ff, epi="swiglu", tn=256,
                      out_dtype=act_dtype, name="ffn_up")
        return _matmul(act, w2, row0=l * d_ff, tk=tk2, epi="residual", res=x, gate=mod[5],
                       name="ffn_down")

    def lru_in(x, mod, act_dtype):
        u = _norm_mod(x, norm1_w[0], mod[0], mod[1], act_dtype)
        gy = _matmul(u, w_in, col0=0, ncols=lw, bias=lru_b_in[0], epi="gelu", out_dtype=act_dtype,
                     name="lru_in_y")
        xb = _matmul(u, w_in, col0=lw, ncols=lw, bias=lru_b_in[0], name="lru_in_x")
        return gy, xb

    def shared_kv(x, mod, act_dtype):
        s = _norm_mod(x, kv_norm_w, mod[0], mod[1], act_dtype)
        outs = []
        for br in range(N_BRANCH):
            if br == 0:
                outs.append(_matmul(s, w_kv, col0=0, ncols=branch_cols, tn=512, name="kv_cmp"))
            else:
                outs.append(_matmul(s, w_kv, col0=br * branch_cols, ncols=branch_cols, tn=512,
                                    epi="headnorm", hn_w=k_norm_w[br],
                                    n_norm_tiles=branch_cols // 2 // 512, name="kv_norm"))
        return outs

    def nsa_query(x, mod, act_dtype):
        u = _norm_mod(x, norm1_w[1], mod[0], mod[1], act_dtype)
        q = _matmul(u, wq, col0=0, ncols=qcols, epi="headnorm", hn_w=q_norm_w[0],
                    n_norm_tiles=qcols // 512, post_scale=HEAD_DIM ** -0.5, tn=512,
                    out_dtype=act_dtype, name="nsa_q")
        return q, _branch_gates_t(u, wg_t)

    lru_args = (lru_conv_w[0], lru_conv_b[0], lru_gate_w[0], lru_gate_b[0], lru_lambda[0])

    mp = [split_mod(m, 6, True) for m in mods]
    xp = x_prompt.reshape(bsz * t_len, d)
    gy, xb = lru_in(xp, mp[0], BF16)
    hg, conv_p, h_p = _lru(xb.reshape(bsz, t_len, lw), gy.reshape(bsz, t_len, lw),
                           jnp.zeros((bsz, hist, lw), F32), jnp.zeros((bsz, lw), F32), *lru_args)
    xp = _matmul(hg.reshape(bsz * t_len, lw), w_out, bias=lru_b_out[0], epi="residual", res=xp,
                 gate=mp[0][2], name="lru_out")
    xp = ffn(xp, 0, mp[0], BF16)
    cmp_p, slc_p, win_p = shared_kv(xp, split_mod(kv_mod, 2, True), BF16)
    q_p, gt_p = nsa_query(xp, mp[1], BF16)
    pq_p = _cmp_pq(cmp_p.reshape(bsz, t_len // CMP_STRIDE, CHUNK_COLS), cmp_w1)
    kc_p, vct_p = _cmp_finish(pq_p, pe, cmp_w1, cmp_b1, cmp_w2, cmp_b2, k_norm_w[0])
    o_p = _nsa_prompt(q_p, gt_p, kc_p, vct_p, slc_p.reshape(bsz, t_len, KV_ROW),
                      win_p.reshape(bsz, t_len, KV_ROW), bsz, t_len)
    xp = _matmul(o_p, wo, epi="residual", res=xp, gate=mp[1][2], name="nsa_out")
    y_prompt = ffn(xp, 1, mp[1], BF16).reshape(bsz, t_len, d)

    ms = [split_mod(m, 6, False) for m in mods]
    xs = x_sample.reshape(dbs, d)
    gy, xb = lru_in(xs, ms[0], F32)
    hg, conv_s, h_s = _lru_step(xb, gy, jnp.swapaxes(state_conv[0], 0, 1), state_lru_h[0], *lru_args)
    xs = _matmul(hg, w_out, bias=lru_b_out[0], epi="residual", res=xs, gate=ms[0][2], name="lru_out")
    xs = ffn(xs, 0, ms[0], F32)
    cmp_s, slc_s, win_s = shared_kv(xs, split_mod(kv_mod, 2, False), F32)
    q_s, gt_s = nsa_query(xs, ms[1], F32)
    pq_s = _cmp_pq(cache_cmp_kv.reshape(-1, PAGE_SIZE // CMP_STRIDE, CHUNK_COLS), cmp_w1,
                   page_table=page_table)
    kc_s, vct_s = _cmp_finish(pq_s, pe, cmp_w1, cmp_b1, cmp_w2, cmp_b2, k_norm_w[0])
    n_slc_s = -(-(past_len + dec_seq) // SLC_BLOCK)
    q_s4 = q_s.reshape(dbs, N_KV, hpg, HEAD_DIM)
    ocmp_s, idx_s = _nsa_sample_select(q_s4, kc_s, vct_s, past_len, n_slc_s)
    gates_s = gt_s.T.reshape(dbs, N_KV, hpg, N_BRANCH)
    o_s = _nsa_sample_attend(q_s4, gates_s, ocmp_s, idx_s, page_table,
                             cache_slc_kv.reshape(-1, PAGE_SIZE, KV_ROW), slc_s,
                             cache_win_kv.reshape(dbs, -1, KV_ROW), win_s, past_len)
    xs = _matmul(o_s.reshape(dbs, qcols), wo, epi="residual", res=xs, gate=ms[1][2], name="nsa_out")
    y_sample = ffn(xs, 1, ms[1], F32).reshape(dbs, dec_seq, d)

    kv5 = lambda a, n, t: a.reshape(n, t, 2, N_KV, HEAD_DIM)
    win_buf = cache_win_kv.shape[1]
    win_s5 = kv5(win_s, dbs, 1)
    new_win_s = jnp.concatenate([cache_win_kv, win_s5], axis=1)[:, -win_buf:]
    return (y_prompt, y_sample, h_p[None], h_s[None], conv_p[None],
            jnp.swapaxes(conv_s, 0, 1)[None],
            kv5(cmp_p, bsz, t_len), kv5(cmp_s, dbs, 1), kv5(slc_p, bsz, t_len), kv5(slc_s, dbs, 1),
            kv5(win_p, bsz, t_len)[:, -WINDOW:], new_win_s)
```

```python
import functools

import jax
import jax.numpy as jnp
from jax import lax
from jax.experimental import pallas as pl
from jax.experimental.pallas import tpu as pltpu

F32 = jnp.float32
BF16 = jnp.bfloat16

HEAD_DIM = 128
N_KV = 4
N_BRANCH = 3
CMP_BLOCK = 32
CMP_STRIDE = 16
SLC_BLOCK = 64
N_SEL = 16
WINDOW = 512
CONV_W = 4
LRU_C = 8.0
PAGE_SIZE = 128
EPS = 1e-6
NEG = -1e30
FORCE = 1e4
TINY = 1e-30

KV_ROW = 2 * N_KV * HEAD_DIM
CHUNK_COLS = CMP_STRIDE * KV_ROW
SLC_SHIFT = SLC_BLOCK.bit_length() - 1
SLC_PER_CHUNK = 4
KEY_CHUNK = SLC_PER_CHUNK * SLC_BLOCK
VMEM_LIMIT = 56 * 1024 * 1024


def _cparams(sem, vmem=None):
    return pltpu.CompilerParams(dimension_semantics=sem, vmem_limit_bytes=vmem or VMEM_LIMIT)


def _pick(n, cands):
    for c in cands:
        if n % c == 0:
            return c
    return n


def _dot(a, b):
    return jnp.dot(a, b, preferred_element_type=F32)


def _dot_nt(a, b):
    return lax.dot_general(a, b, (((1,), (1,)), ((), ())), preferred_element_type=F32)


def _sigmoid(x):
    return 0.5 * (jnp.tanh(0.5 * x) + 1.0)


def _split3(x):
    hi = x.astype(BF16)
    r1 = x - hi.astype(F32)
    mid = r1.astype(BF16)
    lo = (r1 - mid.astype(F32)).astype(BF16)
    return hi, mid, lo


def _norm_mod_kernel(x_ref, w_ref, shift_ref, scale_ref, o_ref):
    x = x_ref[...]
    y = x * lax.rsqrt(jnp.mean(x * x, axis=-1, keepdims=True) + EPS)
    y = y * w_ref[...]
    o_ref[...] = (y * (1.0 + scale_ref[0]) + shift_ref[0]).astype(o_ref.dtype)


def _norm_mod(x, w, shift, scale, out_dtype):
    m, d = x.shape
    nb, rb, _ = shift.shape
    rows_per_nb = m // nb
    tr = rows_per_nb if rb > 1 else _pick(rows_per_nb, (256, 128, 64, 32, 16, 8))
    tiles_per_nb = rows_per_nb // tr
    mod_spec = pl.BlockSpec((1, rb, d), lambda i: (i // tiles_per_nb, 0, 0))
    return pl.pallas_call(
        _norm_mod_kernel,
        grid=(m // tr,),
        in_specs=[pl.BlockSpec((tr, d), lambda i: (i, 0)),
                  pl.BlockSpec((1, d), lambda i: (0, 0)), mod_spec, mod_spec],
        out_specs=pl.BlockSpec((tr, d), lambda i: (i, 0)),
        out_shape=jax.ShapeDtypeStruct((m, d), out_dtype),
        compiler_params=_cparams(("parallel",)),
        name="norm_mod",
    )(x, w.reshape(1, d), shift, scale)


def _head_rms(acc, w, post_scale):
    outs = []
    for h in range(acc.shape[1] // HEAD_DIM):
        sl = acc[:, h * HEAD_DIM:(h + 1) * HEAD_DIM]
        y = sl * lax.rsqrt(jnp.mean(sl * sl, axis=-1, keepdims=True) + EPS)
        y = y * w
        if post_scale is not None:
            y = y * post_scale
        outs.append(y)
    return jnp.concatenate(outs, axis=1)


def _mm_kernel(*refs, nk, w_res, a_silu, has_bias, epi, n_norm_tiles, post_scale):
    it = iter(refs)
    a_ref, w_ref = next(it), next(it)
    w2_ref = next(it) if epi == "swiglu" else None
    b_ref = next(it) if has_bias else None
    res_ref, gate_ref = (next(it), next(it)) if epi == "residual" else (None, None)
    hn_ref = next(it) if epi == "headnorm" else None
    o_ref = next(it)
    acc_ref = next(it) if nk > 1 else None
    wb_ref = next(it) if w_res else None
    wb2_ref = next(it) if w_res and epi == "swiglu" else None

    if w_res:
        @pl.when(pl.program_id(1) == 0)
        def _():
            wb_ref[...] = w_ref[...].astype(BF16)
            if wb2_ref is not None:
                wb2_ref[...] = w2_ref[...].astype(BF16)

        w_first = wb_ref[...]
        w_second = (lambda: wb2_ref[...]) if wb2_ref is not None else None
    else:
        w_first = w_ref[...].astype(BF16)
        w_second = (lambda: w2_ref[...].astype(BF16)) if w2_ref is not None else None

    a = a_ref[...]
    if a_silu:
        a = a * _sigmoid(a)
    a = a.astype(BF16)
    part = _dot(a, w_first)

    def finish(acc):
        if has_bias:
            acc = acc + b_ref[...]
        if epi == "gelu":
            acc = jax.nn.gelu(acc)
        elif epi == "swiglu":
            acc = acc * _sigmoid(acc) * _dot(a, w_second())
        elif epi == "residual":
            acc = res_ref[...] + gate_ref[0] * acc
        elif epi == "headnorm":
            normed = _head_rms(acc, hn_ref[...], post_scale)
            acc = jnp.where(pl.program_id(0 if w_res else 1) < n_norm_tiles, normed, acc)
        o_ref[...] = acc.astype(o_ref.dtype)

    if nk == 1:
        finish(part)
    else:
        k = pl.program_id(2)

        @pl.when(k == 0)
        def _():
            acc_ref[...] = part

        @pl.when(k > 0)
        def _():
            acc_ref[...] += part

        @pl.when(k == nk - 1)
        def _():
            finish(acc_ref[...])


def _matmul(a, w, *, layer=0, col0=0, ncols=None, col0_b=None, bias=None, epi="none", res=None,
            gate=None, hn_w=None, n_norm_tiles=0, post_scale=None, a_silu=False, out_dtype=F32,
            tm=None, tn=None, tk=None, name="matmul"):
    m, kdim = a.shape
    ncols = ncols if ncols is not None else w.shape[-1]
    tm = tm or min(m, 1024, m // gate.shape[0] if gate is not None and gate.shape[1] == 1 else m)
    tn = tn or _pick(ncols, (512, 256, 128))
    tk = tk or kdim
    nk = kdim // tk
    assert w.shape[-2] == kdim
    assert m % tm == 0 and ncols % tn == 0 and kdim % tk == 0 and col0 % tn == 0
    assert epi != "swiglu" or (nk == 1 and col0_b % tn == 0)
    jo = col0 // tn
    w_res = nk == 1 and m // tm > 1 and w.dtype != BF16

    def spec(shape, fn):
        if w_res:
            return pl.BlockSpec(shape, lambda g0, g1, k: fn(g1, g0, k))
        return pl.BlockSpec(shape, fn)

    def w_spec(j0):
        if w.ndim == 3:
            return spec((None, tk, tn), lambda i, j, k: (layer, k, j + j0))
        return spec((tk, tn), lambda i, j, k: (k, j + j0))

    in_specs = [spec((tm, tk), lambda i, j, k: (i, k)), w_spec(jo)]
    args = [a, w]
    if epi == "swiglu":
        in_specs.append(w_spec(col0_b // tn))
        args.append(w)
    if bias is not None:
        in_specs.append(spec((1, tn), lambda i, j, k: (0, j + jo)))
        args.append(bias.reshape(1, -1))
    if epi == "residual":
        nb, rb, _ = gate.shape
        tiles_per_nb = (m // nb) // tm
        assert (m // nb) % tm == 0
        assert tiles_per_nb >= 1 and (rb == 1 or rb == tm)
        in_specs += [spec((tm, tn), lambda i, j, k: (i, j)),
                     spec((1, rb, tn), lambda i, j, k: (i // tiles_per_nb, 0, j))]
        args += [res, gate]
    if epi == "headnorm":
        in_specs.append(spec((1, HEAD_DIM), lambda i, j, k: (0, 0)))
        args.append(hn_w.reshape(1, HEAD_DIM))
    kern = functools.partial(_mm_kernel, nk=nk, w_res=w_res, a_silu=a_silu, has_bias=bias is not None,
                             epi=epi, n_norm_tiles=n_norm_tiles, post_scale=post_scale)
    scratch = [pltpu.VMEM((tm, tn), F32)] if nk > 1 else []
    if w_res:
        scratch += [pltpu.VMEM((tk, tn), BF16)] * (2 if epi == "swiglu" else 1)
    grid = (ncols // tn, m // tm, nk) if w_res else (m // tm, ncols // tn, nk)
    return pl.pallas_call(
        kern,
        grid=grid,
        in_specs=in_specs,
        out_specs=spec((tm, tn), lambda i, j, k: (i, j)),
        out_shape=jax.ShapeDtypeStruct((m, ncols), out_dtype),
        scratch_shapes=scratch,
        compiler_params=_cparams(("parallel", "arbitrary" if w_res else "parallel", "arbitrary")),
        name=name,
    )(*args)


def _gate_kernel(wg_ref, u_ref, o_ref):
    o_ref[...] = _sigmoid(_dot_nt(wg_ref[...].astype(BF16), u_ref[...].astype(BF16)))


def _branch_gates_t(u, wg_t):
    m, kdim = u.shape
    r = wg_t.shape[0]
    tm = _pick(m, (1024, 512, 256, 128))
    return pl.pallas_call(
        _gate_kernel,
        grid=(m // tm,),
        in_specs=[pl.BlockSpec((r, kdim), lambda i: (0, 0)),
                  pl.BlockSpec((tm, kdim), lambda i: (i, 0))],
        out_specs=pl.BlockSpec((r, tm), lambda i: (0, i)),
        out_shape=jax.ShapeDtypeStruct((r, m), F32),
        compiler_params=_cparams(("parallel",)),
        name="branch_gates",
    )(wg_t, u)


def _lru_coeffs(xb, gw_r, gw_i, gb, lam):
    xb16 = xb.astype(BF16)
    r = _sigmoid(_dot(xb16, gw_r) + gb[0:1, :])
    gi = _sigmoid(_dot(xb16, gw_i) + gb[1:2, :])
    z = -lam
    softplus = jnp.maximum(z, 0.0) + jnp.log1p(jnp.exp(-jnp.abs(z)))
    log_a = -LRU_C * r * softplus
    a = jnp.exp(log_a)
    return a, jnp.sqrt(-jnp.tanh(log_a) * (1.0 + a * a)) * gi * xb


def _lru_step_kernel(x_ref, gy_ref, cprev_ref, hprev_ref, cw_ref, cb_ref, gw_ref, gb_ref, lam_ref,
                     hg_ref, nconv_ref, h_ref, *, bw, nbp):
    hist = CONV_W - 1
    x = x_ref[...]
    conv = cprev_ref[0] * cw_ref[0:1, :]
    for k in range(1, hist):
        conv = conv + cprev_ref[k] * cw_ref[k:k + 1, :]
    xc = cb_ref[...] + (conv + x * cw_ref[hist:hist + 1, :])
    for blk in range(nbp):
        cols = slice(blk * bw, (blk + 1) * bw)
        a, b = _lru_coeffs(xc[:, cols], gw_ref[0, blk].astype(BF16), gw_ref[1, blk].astype(BF16),
                           gb_ref[:, cols], lam_ref[:, cols])
        h = a * hprev_ref[:, cols] + b
        h_ref[:, cols] = h
        hg_ref[:, cols] = (h * gy_ref[:, cols].astype(F32)).astype(hg_ref.dtype)
    for k in range(hist - 1):
        nconv_ref[k] = cprev_ref[k + 1]
    nconv_ref[hist - 1] = x


def _lru_step(x, gy, conv_prev, h_prev, conv_w, conv_b, gate_w, gate_b, lam):
    b, w = x.shape
    nblk, bw = gate_w.shape[1], gate_w.shape[2]
    nbp = _pick(nblk, (4, 2, 1))
    cw = nbp * bw
    hist = CONV_W - 1
    rows = pl.BlockSpec((b, cw), lambda c: (0, c))
    hrows = pl.BlockSpec((hist, b, cw), lambda c: (0, 0, c))
    chan = lambda r: pl.BlockSpec((r, cw), lambda c: (0, c))
    return pl.pallas_call(
        functools.partial(_lru_step_kernel, bw=bw, nbp=nbp),
        grid=(w // cw,),
        in_specs=[rows, rows, hrows, rows, chan(CONV_W), chan(1),
                  pl.BlockSpec((2, nbp, bw, bw), lambda c: (0, c, 0, 0)), chan(2), chan(1)],
        out_specs=[rows, hrows, rows],
        out_shape=[jax.ShapeDtypeStruct((b, w), gy.dtype), jax.ShapeDtypeStruct((hist, b, w), F32),
                   jax.ShapeDtypeStruct((b, w), F32)],
        compiler_params=_cparams(("parallel",)),
        name="rglru_step",
    )(x, gy, conv_prev, h_prev, conv_w, conv_b.reshape(1, w), gate_w, gate_b, lam.reshape(1, w))


def _lru_kernel(x_ref, gy_ref, cprev_ref, hprev_ref, cw_ref, cb_ref, gw_ref, gb_ref, lam_ref,
                hg_ref, nconv_ref, hlast_ref,
                xbuf, gw_scr, a_scr, b_scr, hs_scr, h_scr, *, tt, nt, bw, nbp):
    t = pl.program_id(2)
    hist = CONV_W - 1
    base = 8

    @pl.when(t == 0)
    def _():
        xbuf[base - hist:base, :] = cprev_ref[0]
        h_scr[...] = hprev_ref[0]
        gw_scr[...] = gw_ref[...].astype(BF16)

    @pl.when(t > 0)
    def _():
        xbuf[base - hist:base, :] = xbuf[base + tt - hist:base + tt, :]

    xbuf[base:base + tt, :] = x_ref[0]
    conv = xbuf[base - hist:base - hist + tt, :] * cw_ref[0:1, :]
    for k in range(1, CONV_W):
        conv = conv + xbuf[base - hist + k:base - hist + k + tt, :] * cw_ref[k:k + 1, :]
    xc = cb_ref[...] + conv

    for blk in range(nbp):
        cols = slice(blk * bw, (blk + 1) * bw)
        a_scr[:, cols], b_scr[:, cols] = _lru_coeffs(
            xc[:, cols], gw_scr[0, blk], gw_scr[1, blk], gb_ref[:, cols], lam_ref[:, cols])

    def step(i, h):
        h = a_scr[pl.ds(i, 1), :] * h + b_scr[pl.ds(i, 1), :]
        hs_scr[pl.ds(i, 1), :] = h
        return h

    h_scr[...] = lax.fori_loop(0, tt, step, h_scr[...], unroll=min(tt, 8))
    hg_ref[0] = (hs_scr[...] * gy_ref[0].astype(F32)).astype(hg_ref.dtype)

    @pl.when(t == nt - 1)
    def _():
        nconv_ref[0] = xbuf[base + tt - hist:base + tt, :]
        hlast_ref[0] = h_scr[...]


def _lru(x, gy, conv_prev, h_prev, conv_w, conv_b, gate_w, gate_b, lam):
    b, t, w = x.shape
    nblk, bw = gate_w.shape[1], gate_w.shape[2]
    nbp = _pick(nblk, (4, 2, 1))
    cw = nbp * bw
    tt = _pick(t, (256, 128, 64, 32, 16, 8))
    nt = t // tt
    hist = CONV_W - 1
    kern = functools.partial(_lru_kernel, tt=tt, nt=nt, bw=bw, nbp=nbp)
    row = lambda bi, c, ti: (bi, ti, c)
    fixed = lambda bi, c, ti: (bi, 0, c)
    chan = lambda bi, c, ti: (0, c)
    hg, nconv, hlast = pl.pallas_call(
        kern,
        grid=(b, w // cw, nt),
        in_specs=[pl.BlockSpec((1, tt, cw), row), pl.BlockSpec((1, tt, cw), row),
                  pl.BlockSpec((1, hist, cw), fixed), pl.BlockSpec((1, 1, cw), fixed),
                  pl.BlockSpec((CONV_W, cw), chan), pl.BlockSpec((1, cw), chan),
                  pl.BlockSpec((2, nbp, bw, bw), lambda bi, c, ti: (0, c, 0, 0)),
                  pl.BlockSpec((2, cw), chan), pl.BlockSpec((1, cw), chan)],
        out_specs=[pl.BlockSpec((1, tt, cw), row), pl.BlockSpec((1, hist, cw), fixed),
                   pl.BlockSpec((1, 1, cw), fixed)],
        out_shape=[jax.ShapeDtypeStruct((b, t, w), BF16),
                   jax.ShapeDtypeStruct((b, hist, w), F32),
                   jax.ShapeDtypeStruct((b, 1, w), F32)],
        scratch_shapes=[pltpu.VMEM((8 + tt, cw), F32), pltpu.VMEM((2, nbp, bw, bw), BF16),
                        pltpu.VMEM((tt, cw), F32), pltpu.VMEM((tt, cw), F32),
                        pltpu.VMEM((tt, cw), F32), pltpu.VMEM((1, cw), F32)],
        compiler_params=_cparams(("parallel", "parallel", "arbitrary")),
        name="rglru",
    )(x, gy, conv_prev, h_prev.reshape(b, 1, w), conv_w, conv_b.reshape(1, w), gate_w, gate_b,
      lam.reshape(1, w))
    return hg, nconv, hlast.reshape(b, w)


def _cmp_pq_kernel(*refs, n_src):
    refs = refs[len(refs) - (n_src + 2):]
    src = refs[:n_src]
    w1_ref, o_ref = refs[n_src], refs[n_src + 1]
    half = CMP_STRIDE * HEAD_DIM
    nc = n_src * src[0].shape[1]
    for kv in range(2):
        w_pq = jnp.concatenate([w1_ref[kv, :half, :], w1_ref[kv, half:, :]], axis=1).astype(BF16)
        rows = []
        for g in range(N_KV):
            off = kv * N_KV * HEAD_DIM + g * HEAD_DIM
            for s in src:
                rows.append(jnp.concatenate(
                    [s[0, :, r * KV_ROW + off:r * KV_ROW + off + HEAD_DIM].astype(BF16)
                     for r in range(CMP_STRIDE)], axis=1))
        pq = _dot(jnp.concatenate(rows, axis=0), w_pq)
        for g in range(N_KV):
            c0 = (kv * N_KV + g) * 2 * HEAD_DIM
            o_ref[0, :, c0:c0 + 2 * HEAD_DIM] = pq[g * nc:(g + 1) * nc, :]


def _cmp_pq(rows_view, w1, *, page_table=None, pages_per_step=16):
    out_cols = 2 * N_KV * 2 * HEAD_DIM
    w_spec_shape = w1.shape
    if page_table is None:
        nb, nch, _ = rows_view.shape
        tc = _pick(nch, (128, 64, 32, 16, 8))
        return pl.pallas_call(
            functools.partial(_cmp_pq_kernel, n_src=1),
            grid=(nb, nch // tc),
            in_specs=[pl.BlockSpec((1, tc, CHUNK_COLS), lambda b, c: (b, c, 0)),
                      pl.BlockSpec(w_spec_shape, lambda b, c: (0, 0, 0))],
            out_specs=pl.BlockSpec((1, tc, out_cols), lambda b, c: (b, c, 0)),
            out_shape=jax.ShapeDtypeStruct((nb, nch, out_cols), F32),
            compiler_params=_cparams(("parallel", "parallel")),
            name="cmp_pq",
        )(rows_view, w1)
    nb, n_pages = page_table.shape
    cpp = rows_view.shape[1]
    pps = _pick(n_pages, (pages_per_step, 8, 4, 2, 1))
    src_specs = [pl.BlockSpec((1, cpp, CHUNK_COLS),
                              functools.partial(lambda b, s, pt, p: (pt[b, s * pps + p], 0, 0), p=p))
                 for p in range(pps)]
    grid_spec = pltpu.PrefetchScalarGridSpec(
        num_scalar_prefetch=1,
        grid=(nb, n_pages // pps),
        in_specs=src_specs + [pl.BlockSpec(w_spec_shape, lambda b, s, pt: (0, 0, 0))],
        out_specs=pl.BlockSpec((1, pps * cpp, out_cols), lambda b, s, pt: (b, s, 0)),
    )
    return pl.pallas_call(
        functools.partial(_cmp_pq_kernel, n_src=pps),
        grid_spec=grid_spec,
        out_shape=jax.ShapeDtypeStruct((nb, n_pages * cpp, out_cols), F32),
        compiler_params=_cparams(("parallel", "parallel")),
        name="cmp_pq_paged",
    )(page_table, *([rows_view] * pps), w1)


def _cmp_finish_kernel(pq_ref, pe_ref, w1_ref, b1_ref, w2_ref, b2_ref, kn_ref, kc_ref, vct_ref, *, nch):
    is_block = lax.broadcasted_iota(jnp.int32, (nch, 1), 0) < nch - 1
    for kv in range(2):
        pe8 = jnp.broadcast_to(pe_ref[kv], (8, pe_ref.shape[2])).astype(BF16)
        const = _dot(pe8, w1_ref[kv].astype(BF16))[0:1, :] + b1_ref[kv]
        w2 = w2_ref[kv].astype(BF16)
        for g in range(N_KV):
            c0 = (kv * N_KV + g) * 2 * HEAD_DIM
            nxt = jnp.concatenate([pq_ref[0, 1:nch, c0 + HEAD_DIM:c0 + 2 * HEAD_DIM],
                                   jnp.zeros((1, HEAD_DIM), F32)], axis=0)
            hid = pq_ref[0, :, c0:c0 + HEAD_DIM] + nxt + const
            out = _dot(jax.nn.gelu(hid).astype(BF16), w2) + b2_ref[kv]
            if kv == 0:
                out = out * lax.rsqrt(jnp.mean(out * out, axis=-1, keepdims=True) + EPS) * kn_ref[...]
            out = jnp.where(is_block, out, 0.0)
            if kv == 0:
                kc_ref[0, g] = out.astype(kc_ref.dtype)
            else:
                vct_ref[0, g] = out.T.astype(vct_ref.dtype)


def _cmp_finish(pq, pe, w1, b1, w2, b2, k_norm):
    nb, nch, cols = pq.shape
    full = lambda shape: pl.BlockSpec(shape, lambda b: (0,) * len(shape))
    return pl.pallas_call(
        functools.partial(_cmp_finish_kernel, nch=nch),
        grid=(nb,),
        in_specs=[pl.BlockSpec((1, nch, cols), lambda b: (b, 0, 0)),
                  full((2, 1, CMP_BLOCK * HEAD_DIM)), full(w1.shape), full((2, 1, HEAD_DIM)),
                  full(w2.shape), full((2, 1, HEAD_DIM)), full((1, HEAD_DIM))],
        out_specs=[pl.BlockSpec((1, N_KV, nch, HEAD_DIM), lambda b: (b, 0, 0, 0)),
                   pl.BlockSpec((1, N_KV, HEAD_DIM, nch), lambda b: (b, 0, 0, 0))],
        out_shape=[jax.ShapeDtypeStruct((nb, N_KV, nch, HEAD_DIM), BF16),
                   jax.ShapeDtypeStruct((nb, N_KV, HEAD_DIM, nch), BF16)],
        compiler_params=_cparams(("parallel",)),
        name="cmp_finish",
    )(pq, pe.reshape(2, 1, -1), w1, b1.reshape(2, 1, -1), w2, b2.reshape(2, 1, -1),
      k_norm.reshape(1, HEAD_DIM))


def _per_head(fn, x, hpg):
    tq = x.shape[1] // hpg
    return jnp.concatenate([fn(x[:, h * tq:(h + 1) * tq]) for h in range(hpg)], axis=1)


def _flash_update(s, m_scr, l_scr, acc_scr, v_t):
    m_old = m_scr[...]
    m_new = jnp.maximum(m_old, jnp.max(s, axis=0, keepdims=True))
    alpha = jnp.exp(m_old - m_new)
    p = jnp.exp(s - m_new)
    l_scr[...] = alpha * l_scr[...] + jnp.sum(p, axis=0, keepdims=True)
    acc_scr[...] = alpha * acc_scr[...] + _dot(v_t, p.astype(BF16))
    m_scr[...] = m_new


def _nsa_prompt_kernel(q_ref, gt_ref, kc_ref, vct_ref, ks_ref, vs_ref, kw_ref, vw_ref, o_ref,
                       ks_scr, vst_scr, kw_scr, vwt_scr, qt_scr, score_scr, sel_scr,
                       bias_scr, wbias_scr, m_scr, l_scr, acc_scr, o_scr, *, tq, t_len, hpg, n_cmp_pad):
    g = pl.program_id(1)
    qi = pl.program_id(2)
    n_chunks = t_len // KEY_CHUNK
    n_slc = t_len // SLC_BLOCK
    win_chunks = WINDOW // KEY_CHUNK
    n_wb = tq // KEY_CHUNK + win_chunks
    nl = hpg * tq

    @pl.when(qi == 0)
    def _():
        for c in range(n_chunks):
            rows = slice(c * KEY_CHUNK, (c + 1) * KEY_CHUNK)
            ks_scr[rows, :] = ks_ref[0, rows, :].astype(BF16)
            kw_scr[rows, :] = kw_ref[0, rows, :].astype(BF16)
            vst_scr[:, rows] = vs_ref[0, rows, :].T.astype(BF16)
            vwt_scr[:, rows] = vw_ref[0, rows, :].T.astype(BF16)
        kl = lax.broadcasted_iota(jnp.int32, (KEY_CHUNK, tq), 0)
        for c in range(n_wb):
            dlt = (win_chunks - c) * KEY_CHUNK + lax.broadcasted_iota(jnp.int32, (KEY_CHUNK, tq), 1) - kl
            wbias_scr[c] = jnp.where((dlt >= 0) & (dlt < WINDOW), 0.0, NEG)

    for h in range(hpg):
        qt_scr[:, h * tq:(h + 1) * tq] = (
            q_ref[:, h * HEAD_DIM:(h + 1) * HEAD_DIM].astype(F32).T.astype(BF16))
    qpos = qi * tq + lax.broadcasted_iota(jnp.int32, (1, tq), 1)

    def gate_row(branch):
        return jnp.concatenate(
            [gt_ref[pl.ds((g * hpg + h) * N_BRANCH + branch, 1), :] for h in range(hpg)], axis=1)

    cmp_idx = lax.broadcasted_iota(jnp.int32, (n_cmp_pad, tq), 0)
    cmp_mask = cmp_idx * CMP_STRIDE + (CMP_BLOCK - 1) <= qpos
    s = _per_head(lambda x: jnp.where(cmp_mask, x, NEG), _dot(kc_ref[0, 0], qt_scr[...]), hpg)
    e = _per_head(lambda x: jnp.where(cmp_mask, x, 0.0),
                  jnp.exp(s - jnp.max(s, axis=0, keepdims=True)), hpg)
    p = e / jnp.maximum(jnp.sum(e, axis=0, keepdims=True), TINY)
    o_scr[...] = gate_row(0) * _dot(vct_ref[0, 0], p.astype(BF16))
    p_grp = p[:, 0:tq]
    for h in range(1, hpg):
        p_grp = p_grp + p[:, h * tq:(h + 1) * tq]

    ratio = SLC_BLOCK // CMP_STRIDE
    jj = lax.broadcasted_iota(jnp.int32, (n_slc, n_cmp_pad), 0)
    mm = lax.broadcasted_iota(jnp.int32, (n_slc, n_cmp_pad), 1)
    pool = ((mm >= ratio * jj - 1) & (mm <= ratio * jj + ratio - 1)).astype(BF16)
    p_slc = sum(_dot(pool, piece) for piece in _split3(p_grp))
    blk = lax.broadcasted_iota(jnp.int32, (n_slc, tq), 0)
    cur = lax.shift_right_arithmetic(qpos, SLC_SHIFT)
    forced = (blk == 0) | (blk == cur) | (blk == cur - 1)
    valid = blk * SLC_BLOCK <= qpos
    score = jnp.where(forced, FORCE, jnp.where(valid, p_slc, -1.0))
    score_scr[...] = score

    def rank_step(j, rank):
        other = score_scr[pl.ds(j, 1), :]
        beats = (other > score) | ((other == score) & (j < blk))
        return rank + beats.astype(jnp.int32)

    rank = lax.fori_loop(0, n_slc, rank_step, jnp.zeros((n_slc, tq), jnp.int32))
    sel_scr[...] = (rank < min(N_SEL, n_slc)).astype(F32)

    def slc_bias(c, _):
        for r in range(SLC_PER_CHUNK):
            sel_row = sel_scr[pl.ds(c * SLC_PER_CHUNK + r, 1), :]
            kpos = c * KEY_CHUNK + r * SLC_BLOCK + lax.broadcasted_iota(jnp.int32, (SLC_BLOCK, tq), 0)
            ok = (sel_row > 0.5) & (kpos <= qpos)
            bias_scr[pl.ds(pl.multiple_of(c * KEY_CHUNK + r * SLC_BLOCK, SLC_BLOCK), SLC_BLOCK), :] = (
                jnp.where(ok, 0.0, NEG))
        return 0

    n_live = (qi * tq + tq + KEY_CHUNK - 1) // KEY_CHUNK
    lax.fori_loop(0, n_live, slc_bias, 0)

    def reset():
        m_scr[...] = jnp.full((1, nl), NEG, F32)
        l_scr[...] = jnp.zeros((1, nl), F32)
        acc_scr[...] = jnp.zeros((HEAD_DIM, nl), F32)

    def add_branch(branch):
        inv = 1.0 / jnp.maximum(l_scr[...], TINY)
        o_scr[...] += (gate_row(branch) * inv) * acc_scr[...]

    reset()

    def slc_step(c, _):
        keys = pl.ds(pl.multiple_of(c * KEY_CHUNK, KEY_CHUNK), KEY_CHUNK)
        bias = bias_scr[keys, :]
        s = _per_head(lambda x: x + bias, _dot(ks_scr[keys, :], qt_scr[...]), hpg)
        _flash_update(s, m_scr, l_scr, acc_scr, vst_scr[:, keys])
        return 0

    lax.fori_loop(0, n_live, slc_step, 0)
    add_branch(1)

    reset()
    first_key_chunk = qi * (tq // KEY_CHUNK) - win_chunks

    def win_step(cc, _):
        c = n_wb - 1 - cc
        keys = pl.ds(pl.multiple_of((first_key_chunk + c) * KEY_CHUNK, KEY_CHUNK), KEY_CHUNK)
        bias = wbias_scr[c]
        s = _per_head(lambda x: x + bias, _dot(kw_scr[keys, :], qt_scr[...]), hpg)
        _flash_update(s, m_scr, l_scr, acc_scr, vwt_scr[:, keys])
        return 0

    lax.fori_loop(0, jnp.minimum(first_key_chunk, 0) + n_wb, win_step, 0)
    add_branch(2)

    for h in range(hpg):
        o_ref[:, h * HEAD_DIM:(h + 1) * HEAD_DIM] = o_scr[:, h * tq:(h + 1) * tq].T.astype(o_ref.dtype)


def _nsa_prompt(q, gates_t, kc, vct, slc_rows, win_rows, bsz, t_len):
    m, qcols = q.shape
    hpg = qcols // (N_KV * HEAD_DIM)
    gw = hpg * HEAD_DIM
    tq = KEY_CHUNK
    assert t_len % KEY_CHUNK == 0 and WINDOW % KEY_CHUNK == 0 and tq % KEY_CHUNK == 0
    nq = t_len // tq
    n_cmp_pad = kc.shape[2]
    n_slc = t_len // SLC_BLOCK
    n_wb = tq // KEY_CHUNK + WINDOW // KEY_CHUNK
    kern = functools.partial(_nsa_prompt_kernel, tq=tq, t_len=t_len, hpg=hpg, n_cmp_pad=n_cmp_pad)
    kv_k = lambda b, g, i: (b, 0, g)
    kv_v = lambda b, g, i: (b, 0, N_KV + g)
    return pl.pallas_call(
        kern,
        grid=(bsz, N_KV, nq),
        in_specs=[pl.BlockSpec((tq, gw), lambda b, g, i: (b * nq + i, g)),
                  pl.BlockSpec((gates_t.shape[0], tq), lambda b, g, i: (0, b * nq + i)),
                  pl.BlockSpec((1, 1, n_cmp_pad, HEAD_DIM), lambda b, g, i: (b, g, 0, 0)),
                  pl.BlockSpec((1, 1, HEAD_DIM, n_cmp_pad), lambda b, g, i: (b, g, 0, 0)),
                  pl.BlockSpec((1, t_len, HEAD_DIM), kv_k), pl.BlockSpec((1, t_len, HEAD_DIM), kv_v),
                  pl.BlockSpec((1, t_len, HEAD_DIM), kv_k), pl.BlockSpec((1, t_len, HEAD_DIM), kv_v)],
        out_specs=pl.BlockSpec((tq, gw), lambda b, g, i: (b * nq + i, g)),
        out_shape=jax.ShapeDtypeStruct((m, qcols), BF16),
        scratch_shapes=[pltpu.VMEM((t_len, HEAD_DIM), BF16), pltpu.VMEM((HEAD_DIM, t_len), BF16),
                        pltpu.VMEM((t_len, HEAD_DIM), BF16), pltpu.VMEM((HEAD_DIM, t_len), BF16),
                        pltpu.VMEM((HEAD_DIM, hpg * tq), BF16),
                        pltpu.VMEM((n_slc, tq), F32), pltpu.VMEM((n_slc, tq), F32),
                        pltpu.VMEM((t_len, tq), F32), pltpu.VMEM((n_wb, KEY_CHUNK, tq), F32),
                        pltpu.VMEM((1, hpg * tq), F32), pltpu.VMEM((1, hpg * tq), F32),
                        pltpu.VMEM((HEAD_DIM, hpg * tq), F32), pltpu.VMEM((HEAD_DIM, hpg * tq), F32)],
        compiler_params=_cparams(("parallel", "parallel", "arbitrary")),
        name="nsa_prompt",
    )(q, gates_t, kc, vct, slc_rows, slc_rows, win_rows, win_rows)


def _nsa_sample_select_kernel(q_ref, kc_ref, vct_ref, ocmp_ref, idx_ref, *, qpos, n_slc, nsp):
    q = q_ref[0, 0].astype(BF16)
    ncp = kc_ref.shape[2]
    s = _dot_nt(q, kc_ref[0, 0])
    m_idx = lax.broadcasted_iota(jnp.int32, (1, ncp), 1)
    mask = m_idx * CMP_STRIDE + (CMP_BLOCK - 1) <= qpos
    s = jnp.where(mask, s, NEG)
    e = jnp.where(mask, jnp.exp(s - jnp.max(s, axis=-1, keepdims=True)), 0.0)
    p = e / jnp.maximum(jnp.sum(e, axis=-1, keepdims=True), TINY)
    ocmp_ref[0, 0] = _dot_nt(p.astype(BF16), vct_ref[0, 0])

    ratio = SLC_BLOCK // CMP_STRIDE
    p_grp = jnp.broadcast_to(jnp.sum(p, axis=0, keepdims=True), (8, ncp))
    mm = lax.broadcasted_iota(jnp.int32, (ncp, nsp), 0)
    jj = lax.broadcasted_iota(jnp.int32, (ncp, nsp), 1)
    pool_t = ((mm >= ratio * jj - 1) & (mm <= ratio * jj + ratio - 1)).astype(BF16)
    p_slc = sum(_dot(piece, pool_t) for piece in _split3(p_grp))[0:1, :]
    blk = lax.broadcasted_iota(jnp.int32, (1, nsp), 1)
    cur = qpos // SLC_BLOCK
    forced = (blk == 0) | (blk == cur) | (blk == cur - 1)
    score = jnp.where(forced, FORCE, jnp.where(blk * SLC_BLOCK <= qpos, p_slc, -1.0))
    score = jnp.where(blk < n_slc, score, -2.0)

    ii = lax.broadcasted_iota(jnp.int32, (nsp, nsp), 0)
    jx = lax.broadcasted_iota(jnp.int32, (nsp, nsp), 1)
    mine = jnp.broadcast_to(score, (nsp, nsp))
    other = mine.T
    beats = (other > mine) | ((other == mine) & (ii < jx))
    rank = jnp.sum(beats.astype(F32), axis=0, keepdims=True)
    sel = (rank < float(N_SEL)).astype(F32)
    sel_other = jnp.broadcast_to(sel, (nsp, nsp)).T
    pos = jnp.sum(jnp.where(ii < jx, sel_other, 0.0), axis=0, keepdims=True)
    kk = lax.broadcasted_iota(jnp.int32, (N_SEL, nsp), 0).astype(F32)
    jrow = lax.broadcasted_iota(jnp.int32, (N_SEL, nsp), 1).astype(F32)
    hit = (sel > 0.5) & (pos == kk)
    idx = jnp.sum(jnp.where(hit, jrow, 0.0), axis=1, keepdims=True)
    idx_ref[0, 0] = jnp.broadcast_to(idx, (N_SEL, HEAD_DIM)).astype(jnp.int32)


def _nsa_sample_select(q, kc, vct, qpos, n_slc):
    dbs, _, hpg, _ = q.shape
    ncp = kc.shape[2]
    nsp = -(-n_slc // HEAD_DIM) * HEAD_DIM
    assert n_slc >= N_SEL
    blk4 = lambda s2, s3: pl.BlockSpec((1, 1, s2, s3), lambda b, g: (b, g, 0, 0))
    ocmp, idx = pl.pallas_call(
        functools.partial(_nsa_sample_select_kernel, qpos=qpos, n_slc=n_slc, nsp=nsp),
        grid=(dbs, N_KV),
        in_specs=[blk4(hpg, HEAD_DIM), blk4(ncp, HEAD_DIM), blk4(HEAD_DIM, ncp)],
        out_specs=[blk4(hpg, HEAD_DIM), blk4(N_SEL, HEAD_DIM)],
        out_shape=[jax.ShapeDtypeStruct((dbs, N_KV, hpg, HEAD_DIM), F32),
                   jax.ShapeDtypeStruct((dbs, N_KV, N_SEL, HEAD_DIM), jnp.int32)],
        compiler_params=_cparams(("parallel", "parallel")),
        name="nsa_sample_select",
    )(q, kc, vct)
    return ocmp, idx[..., 0]


def _nsa_sample_attend_kernel(pt_ref, idx_ref, q_ref, gate_ref, ocmp_ref, snew_ref, wnew_ref,
                              kw_ref, vw_ref, *rest, qpos, n_cached, win_buf):
    k_blocks, v_blocks, o_ref = rest[:N_SEL], rest[N_SEL:2 * N_SEL], rest[2 * N_SEL]
    b, g = pl.program_id(0), pl.program_id(1)
    q = q_ref[0, 0].astype(BF16)
    qf = q.astype(F32)

    def attend(s, mask, v, k_new, v_new):
        s_self = jnp.sum(qf * k_new.astype(BF16).astype(F32), axis=-1, keepdims=True)
        s = jnp.where(mask, s, NEG)
        m = jnp.maximum(jnp.max(s, axis=-1, keepdims=True), s_self)
        e = jnp.where(mask, jnp.exp(s - m), 0.0)
        e_self = jnp.exp(s_self - m)
        l = jnp.sum(e, axis=-1, keepdims=True) + e_self
        acc = _dot(e.astype(BF16), v) + e_self.astype(BF16).astype(F32) * v_new.astype(BF16).astype(F32)
        return acc / jnp.maximum(l, TINY)

    n_keys = N_SEL * SLC_BLOCK
    k_all = jnp.concatenate([kb[0] for kb in k_blocks], axis=0).astype(BF16)
    v_all = jnp.concatenate([vb[0] for vb in v_blocks], axis=0).astype(BF16)
    lane = lax.broadcasted_iota(jnp.int32, (1, n_keys), 1)
    slot = lax.shift_right_arithmetic(lane, SLC_SHIFT)
    blk_id = jnp.zeros((1, n_keys), jnp.int32)
    for k in range(N_SEL):
        blk_id = jnp.where(slot == k, idx_ref[(b * N_KV + g) * N_SEL + k], blk_id)
    kpos = blk_id * SLC_BLOCK + (lane & (SLC_BLOCK - 1))
    slc_mask = (blk_id < n_cached) & (kpos <= qpos)
    o_slc = attend(_dot_nt(q, k_all), slc_mask, v_all,
                   snew_ref[0, pl.ds(g, 1), :], snew_ref[0, pl.ds(N_KV + g, 1), :])

    wi = lax.broadcasted_iota(jnp.int32, (1, win_buf), 1)
    dlt = win_buf - wi
    win_mask = (dlt >= 0) & (dlt < WINDOW) & (qpos - dlt >= 0)
    o_win = attend(_dot_nt(q, kw_ref[0].astype(BF16)), win_mask, vw_ref[0].astype(BF16),
                   wnew_ref[0, pl.ds(g, 1), :], wnew_ref[0, pl.ds(N_KV + g, 1), :])

    gate = gate_ref[0, 0]
    o_ref[0, 0] = gate[:, 0:1] * ocmp_ref[0, 0] + gate[:, 1:2] * o_slc + gate[:, 2:3] * o_win


def _nsa_sample_attend(q, gates, ocmp, idx, page_table, slc_cache, slc_new, win_cache, win_new, qpos):
    dbs, _, hpg, _ = q.shape
    win_buf = win_cache.shape[1]
    n_cached = page_table.shape[1] * PAGE_SIZE // SLC_BLOCK
    bpp = PAGE_SIZE // SLC_BLOCK
    cache_blocks = slc_cache.reshape(-1, SLC_BLOCK, KV_ROW)

    def blk_map(b, g, pt, ix, *, k, col):
        blk = jnp.minimum(ix[(b * N_KV + g) * N_SEL + k], n_cached - 1)
        return pt[b, blk // bpp] * bpp + blk % bpp, 0, col * N_KV + g

    blk4 = lambda s2, s3: pl.BlockSpec((1, 1, s2, s3), lambda b, g, pt, ix: (b, g, 0, 0))
    new_spec = pl.BlockSpec((1, 2 * N_KV, HEAD_DIM), lambda b, g, pt, ix: (b, 0, 0))
    wk = pl.BlockSpec((1, win_buf, HEAD_DIM), lambda b, g, pt, ix: (b, 0, g))
    wv = pl.BlockSpec((1, win_buf, HEAD_DIM), lambda b, g, pt, ix: (b, 0, N_KV + g))
    gathered = [pl.BlockSpec((1, SLC_BLOCK, HEAD_DIM), functools.partial(blk_map, k=k, col=col))
                for col in range(2) for k in range(N_SEL)]
    grid_spec = pltpu.PrefetchScalarGridSpec(
        num_scalar_prefetch=2,
        grid=(dbs, N_KV),
        in_specs=[blk4(hpg, HEAD_DIM), blk4(hpg, N_BRANCH), blk4(hpg, HEAD_DIM), new_spec, new_spec,
                  wk, wv] + gathered,
        out_specs=blk4(hpg, HEAD_DIM),
    )
    return pl.pallas_call(
        functools.partial(_nsa_sample_attend_kernel, qpos=qpos, n_cached=n_cached, win_buf=win_buf),
        grid_spec=grid_spec,
        out_shape=jax.ShapeDtypeStruct((dbs, N_KV, hpg, HEAD_DIM), F32),
        compiler_params=_cparams(("parallel", "parallel")),
        name="nsa_sample_attend",
    )(page_table, idx.reshape(-1), q, gates, ocmp,
      slc_new.reshape(dbs, 2 * N_KV, HEAD_DIM), win_new.reshape(dbs, 2 * N_KV, HEAD_DIM),
      win_cache, win_cache, *([cache_blocks] * (2 * N_SEL)))


def kernel(x_prompt, x_sample, c_prompt, c_sample, state_lru_h, state_conv, cache_cmp_kv, cache_slc_kv, cache_win_kv, page_table, ada_w, ada_b, norm1_w, norm2_w, lru_w_in, lru_b_in, lru_conv_w, lru_conv_b, lru_gate_w, lru_gate_b, lru_lambda, lru_w_out, lru_b_out, ffn_w13, ffn_w2, kv_ada_w, kv_ada_b, kv_norm_w, w_kv, k_norm_w, cmp_pe, cmp_w1, cmp_b1, cmp_w2, cmp_b2, w_qg, q_norm_w, w_o):
    bsz, t_len, d = x_prompt.shape
    dbs, dec_seq, _ = x_sample.shape
    depth = ada_w.shape[0]
    n_a = lru_w_in.shape[0]
    assert dec_seq == 1 and depth == 2 and n_a == 1 and w_qg.shape[0] == 1
    lw = lru_w_in.shape[2] // 2
    d_ff = ffn_w2.shape[1]
    n_pages = page_table.shape[1]
    past_len = n_pages * PAGE_SIZE
    qcols = w_o.shape[1]
    hpg = qcols // (N_KV * HEAD_DIM)
    hist = CONV_W - 1
    branch_cols = KV_ROW

    n_c = bsz + dbs
    c_all = jnp.pad(jnp.concatenate([c_prompt, c_sample], axis=0), ((0, (-n_c) % 8), (0, 0)))
    mods = [_matmul(c_all, ada_w, layer=l, bias=ada_b[l], a_silu=True, tn=1024, name="ada")
            for l in range(depth)]
    kv_mod = _matmul(c_all, kv_ada_w, bias=kv_ada_b, a_silu=True, tn=1024, name="kv_ada")

    def split_mod(mat, n, prompt):
        parts = [mat[:, i * d:(i + 1) * d] for i in range(n)]
        if prompt:
            return [p[:bsz].reshape(bsz, 1, d) for p in parts]
        return [p[bsz:n_c].reshape(1, dbs, d) for p in parts]

    w13 = ffn_w13
    w2 = ffn_w2.astype(BF16)
    tk2 = _pick(d_ff, (d_ff // 2,)) if d_ff > 4096 else d_ff
    w_in, w_out, wq, wo = lru_w_in, lru_w_out, w_qg, w_o
    wg_t = w_qg[0, :, qcols:].T
    pe = cmp_pe.reshape(2, -1)

    def ffn(x, l, mod, act_dtype):
        u = _norm_mod(x, norm2_w[l], mod[3], mod[4], act_dtype)
        act = _matmul(u, w13, layer=l, col0=0, col0_b=d_ff, ncols=d_ff, epi="swiglu", tn=256,
                      out_dtype=act_dtype, name="ffn_up")
        return _matmul(act, w2, layer=l, tk=tk2, epi="residual", res=x, gate=mod[5],
                       name="ffn_down")

    def lru_in(x, mod, act_dtype):
        u = _norm_mod(x, norm1_w[0], mod[0], mod[1], act_dtype)
        gy = _matmul(u, w_in, col0=0, ncols=lw, bias=lru_b_in[0], epi="gelu", out_dtype=act_dtype,
                     name="lru_in_y")
        xb = _matmul(u, w_in, col0=lw, ncols=lw, bias=lru_b_in[0], name="lru_in_x")
        return gy, xb

    def shared_kv(x, mod, act_dtype):
        s = _norm_mod(x, kv_norm_w, mod[0], mod[1], act_dtype)
        outs = []
        for br in range(N_BRANCH):
            if br == 0:
                outs.append(_matmul(s, w_kv, col0=0, ncols=branch_cols, tn=512, name="kv_cmp"))
            else:
                outs.append(_matmul(s, w_kv, col0=br * branch_cols, ncols=branch_cols, tn=512,
                                    epi="headnorm", hn_w=k_norm_w[br],
                                    n_norm_tiles=branch_cols // 2 // 512, name="kv_norm"))
        return outs

    def nsa_query(x, mod, act_dtype):
        u = _norm_mod(x, norm1_w[1], mod[0], mod[1], act_dtype)
        q = _matmul(u, wq, col0=0, ncols=qcols, epi="headnorm", hn_w=q_norm_w[0],
                    n_norm_tiles=qcols // 512, post_scale=HEAD_DIM ** -0.5, tn=512,
                    out_dtype=act_dtype, name="nsa_q")
        return q, _branch_gates_t(u, wg_t)

    lru_args = (lru_conv_w[0], lru_conv_b[0], lru_gate_w[0], lru_gate_b[0], lru_lambda[0])

    mp = [split_mod(m, 6, True) for m in mods]
    xp = x_prompt.reshape(bsz * t_len, d)
    gy, xb = lru_in(xp, mp[0], BF16)
    hg, conv_p, h_p = _lru(xb.reshape(bsz, t_len, lw), gy.reshape(bsz, t_len, lw),
                           jnp.zeros((bsz, hist, lw), F32), jnp.zeros((bsz, lw), F32), *lru_args)
    xp = _matmul(hg.reshape(bsz * t_len, lw), w_out, bias=lru_b_out[0], epi="residual", res=xp,
                 gate=mp[0][2], name="lru_out")
    xp = ffn(xp, 0, mp[0], BF16)
    cmp_p, slc_p, win_p = shared_kv(xp, split_mod(kv_mod, 2, True), BF16)
    q_p, gt_p = nsa_query(xp, mp[1], BF16)
    pq_p = _cmp_pq(cmp_p.reshape(bsz, t_len // CMP_STRIDE, CHUNK_COLS), cmp_w1)
    kc_p, vct_p = _cmp_finish(pq_p, pe, cmp_w1, cmp_b1, cmp_w2, cmp_b2, k_norm_w[0])
    o_p = _nsa_prompt(q_p, gt_p, kc_p, vct_p, slc_p.reshape(bsz, t_len, KV_ROW),
                      win_p.reshape(bsz, t_len, KV_ROW), bsz, t_len)
    xp = _matmul(o_p, wo, epi="residual", res=xp, gate=mp[1][2], name="nsa_out")
    y_prompt = ffn(xp, 1, mp[1], BF16).reshape(bsz, t_len, d)

    ms = [split_mod(m, 6, False) for m in mods]
    xs = x_sample.reshape(dbs, d)
    gy, xb = lru_in(xs, ms[0], F32)
    hg, conv_s, h_s = _lru_step(xb, gy, jnp.swapaxes(state_conv[0], 0, 1), state_lru_h[0], *lru_args)
    xs = _matmul(hg, w_out, bias=lru_b_out[0], epi="residual", res=xs, gate=ms[0][2], name="lru_out")
    xs = ffn(xs, 0, ms[0], F32)
    cmp_s, slc_s, win_s = shared_kv(xs, split_mod(kv_mod, 2, False), F32)
    q_s, gt_s = nsa_query(xs, ms[1], F32)
    pq_s = _cmp_pq(cache_cmp_kv.reshape(-1, PAGE_SIZE // CMP_STRIDE, CHUNK_COLS), cmp_w1,
                   page_table=page_table)
    kc_s, vct_s = _cmp_finish(pq_s, pe, cmp_w1, cmp_b1, cmp_w2, cmp_b2, k_norm_w[0])
    n_slc_s = -(-(past_len + dec_seq) // SLC_BLOCK)
    q_s4 = q_s.reshape(dbs, N_KV, hpg, HEAD_DIM)
    ocmp_s, idx_s = _nsa_sample_select(q_s4, kc_s, vct_s, past_len, n_slc_s)
    gates_s = gt_s.T.reshape(dbs, N_KV, hpg, N_BRANCH)
    o_s = _nsa_sample_attend(q_s4, gates_s, ocmp_s, idx_s, page_table,
                             cache_slc_kv.reshape(-1, PAGE_SIZE, KV_ROW), slc_s,
                             cache_win_kv.reshape(dbs, -1, KV_ROW), win_s, past_len)
    xs = _matmul(o_s.reshape(dbs, qcols), wo, epi="residual", res=xs, gate=ms[1][2], name="nsa_out")
    y_sample = ffn(xs, 1, ms[1], F32).reshape(dbs, dec_seq, d)

    kv5 = lambda a, n, t: a.reshape(n, t, 2, N_KV, HEAD_DIM)
    win_buf = cache_win_kv.shape[1]
    win_s5 = kv5(win_s, dbs, 1)
    new_win_s = jnp.concatenate([cache_win_kv, win_s5], axis=1)[:, -win_buf:]
    return (y_prompt, y_sample, h_p[None], h_s[None], conv_p[None],
            jnp.swapaxes(conv_s, 0, 1)[None],
            kv5(cmp_p, bsz, t_len), kv5(cmp_s, dbs, 1), kv5(slc_p, bsz, t_len), kv5(slc_s, dbs, 1),
            kv5(win_p, bsz, t_len)[:, -WINDOW:], new_win_s)
```

```python
import functools

import jax
import jax.numpy as jnp
from jax import lax
from jax.experimental import pallas as pl
from jax.experimental.pallas import tpu as pltpu

F32 = jnp.float32
BF16 = jnp.bfloat16

HEAD_DIM = 128
N_KV = 4
N_BRANCH = 3
CMP_BLOCK = 32
CMP_STRIDE = 16
SLC_BLOCK = 64
N_SEL = 16
WINDOW = 512
CONV_W = 4
LRU_C = 8.0
PAGE_SIZE = 128
EPS = 1e-6
NEG = -1e30
FORCE = 1e4
TINY = 1e-30

KV_ROW = 2 * N_KV * HEAD_DIM
SLC_SHIFT = SLC_BLOCK.bit_length() - 1
SLC_PER_CHUNK = 4
KEY_CHUNK = SLC_PER_CHUNK * SLC_BLOCK
ONES_ROWS = 16
VMEM_LIMIT = 56 * 1024 * 1024


def _cparams(sem, vmem=None):
    return pltpu.CompilerParams(dimension_semantics=sem, vmem_limit_bytes=vmem or VMEM_LIMIT)


def _pick(n, cands):
    for c in cands:
        if n % c == 0:
            return c
    return n


def _dot(a, b):
    return jnp.dot(a, b, preferred_element_type=F32)


def _dot_nt(a, b):
    return lax.dot_general(a, b, (((1,), (1,)), ((), ())), preferred_element_type=F32)


def _sigmoid(x):
    return 0.5 * (jnp.tanh(0.5 * x) + 1.0)


def _split3(x):
    hi = x.astype(BF16)
    r1 = x - hi.astype(F32)
    mid = r1.astype(BF16)
    lo = (r1 - mid.astype(F32)).astype(BF16)
    return hi, mid, lo


def _norm_mod_kernel(x_ref, w_ref, shift_ref, scale_ref, o_ref):
    x = x_ref[...]
    y = x * lax.rsqrt(jnp.mean(x * x, axis=-1, keepdims=True) + EPS)
    y = y * w_ref[...]
    o_ref[...] = (y * (1.0 + scale_ref[0]) + shift_ref[0]).astype(o_ref.dtype)


def _norm_mod(x, w, shift, scale, out_dtype):
    m, d = x.shape
    nb, rb, _ = shift.shape
    rows_per_nb = m // nb
    tr = rows_per_nb if rb > 1 else _pick(rows_per_nb, (512, 256, 128, 64, 32, 16, 8))
    tiles_per_nb = rows_per_nb // tr
    mod_spec = pl.BlockSpec((1, rb, d), lambda i: (i // tiles_per_nb, 0, 0))
    return pl.pallas_call(
        _norm_mod_kernel,
        grid=(m // tr,),
        in_specs=[pl.BlockSpec((tr, d), lambda i: (i, 0)),
                  pl.BlockSpec((1, d), lambda i: (0, 0)), mod_spec, mod_spec],
        out_specs=pl.BlockSpec((tr, d), lambda i: (i, 0)),
        out_shape=jax.ShapeDtypeStruct((m, d), out_dtype),
        compiler_params=_cparams(("parallel",)),
        name="norm_mod",
    )(x, w.reshape(1, d), shift, scale)


def _head_rms(acc, w, post_scale):
    outs = []
    for h in range(acc.shape[1] // HEAD_DIM):
        sl = acc[:, h * HEAD_DIM:(h + 1) * HEAD_DIM]
        y = sl * lax.rsqrt(jnp.mean(sl * sl, axis=-1, keepdims=True) + EPS)
        y = y * w
        if post_scale is not None:
            y = y * post_scale
        outs.append(y)
    return jnp.concatenate(outs, axis=1)


def _mm_kernel(*refs, nk, a_silu, has_bias, epi, n_norm_tiles, post_scale):
    it = iter(refs)
    a_ref, w_ref = next(it), next(it)
    w2_ref = next(it) if epi == "swiglu" else None
    b_ref = next(it) if has_bias else None
    res_ref, gate_ref = (next(it), next(it)) if epi == "residual" else (None, None)
    hn_ref = next(it) if epi == "headnorm" else None
    o_ref = next(it)
    acc_ref = next(it) if nk > 1 else None

    a = a_ref[...]
    if a_silu:
        a = a * _sigmoid(a)
    a = a.astype(BF16)
    part = _dot(a, w_ref[...].astype(BF16))

    def finish(acc):
        if has_bias:
            acc = acc + b_ref[...]
        if epi == "gelu":
            acc = jax.nn.gelu(acc)
        elif epi == "swiglu":
            acc = acc * _sigmoid(acc) * _dot(a, w2_ref[...].astype(BF16))
        elif epi == "residual":
            acc = res_ref[...] + gate_ref[0] * acc
        elif epi == "headnorm":
            normed = _head_rms(acc, hn_ref[...], post_scale)
            acc = jnp.where(pl.program_id(1) < n_norm_tiles, normed, acc)
        o_ref[...] = acc.astype(o_ref.dtype)

    if nk == 1:
        finish(part)
    else:
        k = pl.program_id(2)

        @pl.when(k == 0)
        def _():
            acc_ref[...] = part

        @pl.when(k > 0)
        def _():
            acc_ref[...] += part

        @pl.when(k == nk - 1)
        def _():
            finish(acc_ref[...])


def _matmul(a, w, *, layer=0, col0=0, ncols=None, col0_b=None, bias=None, epi="none", res=None,
            gate=None, hn_w=None, n_norm_tiles=0, post_scale=None, a_silu=False, out_dtype=F32,
            tm=None, tn=None, tk=None, name="matmul"):
    m, kdim = a.shape
    ncols = ncols if ncols is not None else w.shape[-1]
    tm = tm or min(m, 1024, m // gate.shape[0] if gate is not None and gate.shape[1] == 1 else m)
    tn = tn or _pick(ncols, (512, 256, 128))
    tk = tk or kdim
    nk = kdim // tk
    assert w.shape[-2] == kdim
    assert m % tm == 0 and ncols % tn == 0 and kdim % tk == 0 and col0 % tn == 0
    assert epi != "swiglu" or (nk == 1 and col0_b % tn == 0)
    jo = col0 // tn
    spec = pl.BlockSpec

    def w_spec(j0):
        if w.ndim == 3:
            return spec((None, tk, tn), lambda i, j, k: (layer, k, j + j0))
        return spec((tk, tn), lambda i, j, k: (k, j + j0))

    in_specs = [spec((tm, tk), lambda i, j, k: (i, k)), w_spec(jo)]
    args = [a, w]
    if epi == "swiglu":
        in_specs.append(w_spec(col0_b // tn))
        args.append(w)
    if bias is not None:
        in_specs.append(spec((1, tn), lambda i, j, k: (0, j + jo)))
        args.append(bias.reshape(1, -1))
    if epi == "residual":
        nb, rb, _ = gate.shape
        tiles_per_nb = (m // nb) // tm
        assert (m // nb) % tm == 0
        assert tiles_per_nb >= 1 and (rb == 1 or rb == tm)
        in_specs += [spec((tm, tn), lambda i, j, k: (i, j)),
                     spec((1, rb, tn), lambda i, j, k: (i // tiles_per_nb, 0, j))]
        args += [res, gate]
    if epi == "headnorm":
        in_specs.append(spec((1, HEAD_DIM), lambda i, j, k: (0, 0)))
        args.append(hn_w.reshape(1, HEAD_DIM))
    kern = functools.partial(_mm_kernel, nk=nk, a_silu=a_silu, has_bias=bias is not None,
                             epi=epi, n_norm_tiles=n_norm_tiles, post_scale=post_scale)
    return pl.pallas_call(
        kern,
        grid=(m // tm, ncols // tn, nk),
        in_specs=in_specs,
        out_specs=spec((tm, tn), lambda i, j, k: (i, j)),
        out_shape=jax.ShapeDtypeStruct((m, ncols), out_dtype),
        scratch_shapes=[pltpu.VMEM((tm, tn), F32)] if nk > 1 else [],
        compiler_params=_cparams(("parallel", "parallel", "arbitrary")),
        name=name,
    )(*args)


def _gate_kernel(wg_ref, u_ref, o_ref):
    o_ref[...] = _sigmoid(_dot_nt(wg_ref[...].astype(BF16), u_ref[...].astype(BF16)))


def _branch_gates_t(u, wg_t):
    m, kdim = u.shape
    r = wg_t.shape[0]
    tm = _pick(m, (1024, 512, 256, 128))
    return pl.pallas_call(
        _gate_kernel,
        grid=(m // tm,),
        in_specs=[pl.BlockSpec((r, kdim), lambda i: (0, 0)),
                  pl.BlockSpec((tm, kdim), lambda i: (i, 0))],
        out_specs=pl.BlockSpec((r, tm), lambda i: (0, i)),
        out_shape=jax.ShapeDtypeStruct((r, m), F32),
        compiler_params=_cparams(("parallel",)),
        name="branch_gates",
    )(wg_t, u)


def _lru_coeffs(xb, gw_r, gw_i, gb, lam):
    xb16 = xb.astype(BF16)
    r = _sigmoid(_dot(xb16, gw_r) + gb[0:1, :])
    gi = _sigmoid(_dot(xb16, gw_i) + gb[1:2, :])
    z = -lam
    softplus = jnp.maximum(z, 0.0) + jnp.log1p(jnp.exp(-jnp.abs(z)))
    log_a = -LRU_C * r * softplus
    a = jnp.exp(log_a)
    return a, jnp.sqrt(-jnp.tanh(log_a) * (1.0 + a * a)) * gi * xb


def _lru_step_kernel(x_ref, gy_ref, cprev_ref, hprev_ref, cw_ref, cb_ref, gw_ref, gb_ref, lam_ref,
                     hg_ref, nconv_ref, h_ref, *, bw, nbp):
    hist = CONV_W - 1
    x = x_ref[...]
    conv = cprev_ref[0] * cw_ref[0:1, :]
    for k in range(1, hist):
        conv = conv + cprev_ref[k] * cw_ref[k:k + 1, :]
    xc = cb_ref[...] + (conv + x * cw_ref[hist:hist + 1, :])
    for blk in range(nbp):
        cols = slice(blk * bw, (blk + 1) * bw)
        a, b = _lru_coeffs(xc[:, cols], gw_ref[0, blk].astype(BF16), gw_ref[1, blk].astype(BF16),
                           gb_ref[:, cols], lam_ref[:, cols])
        h = a * hprev_ref[:, cols] + b
        h_ref[:, cols] = h
        hg_ref[:, cols] = (h * gy_ref[:, cols].astype(F32)).astype(hg_ref.dtype)
    for k in range(hist - 1):
        nconv_ref[k] = cprev_ref[k + 1]
    nconv_ref[hist - 1] = x


def _lru_step(x, gy, conv_prev, h_prev, conv_w, conv_b, gate_w, gate_b, lam):
    b, w = x.shape
    nblk, bw = gate_w.shape[1], gate_w.shape[2]
    nbp = _pick(nblk, (4, 2, 1))
    cw = nbp * bw
    hist = CONV_W - 1
    rows = pl.BlockSpec((b, cw), lambda c: (0, c))
    hrows = pl.BlockSpec((hist, b, cw), lambda c: (0, 0, c))
    chan = lambda r: pl.BlockSpec((r, cw), lambda c: (0, c))
    return pl.pallas_call(
        functools.partial(_lru_step_kernel, bw=bw, nbp=nbp),
        grid=(w // cw,),
        in_specs=[rows, rows, hrows, rows, chan(CONV_W), chan(1),
                  pl.BlockSpec((2, nbp, bw, bw), lambda c: (0, c, 0, 0)), chan(2), chan(1)],
        out_specs=[rows, hrows, rows],
        out_shape=[jax.ShapeDtypeStruct((b, w), gy.dtype), jax.ShapeDtypeStruct((hist, b, w), F32),
                   jax.ShapeDtypeStruct((b, w), F32)],
        compiler_params=_cparams(("parallel",)),
        name="rglru_step",
    )(x, gy, conv_prev, h_prev, conv_w, conv_b.reshape(1, w), gate_w, gate_b, lam.reshape(1, w))


def _lru_kernel(x_ref, gy_ref, cprev_ref, hprev_ref, cw_ref, cb_ref, gw_ref, gb_ref, lam_ref,
                hg_ref, nconv_ref, hlast_ref,
                xbuf, gw_scr, a_scr, b_scr, hs_scr, h_scr, *, tt, nt, bw, nbp):
    t = pl.program_id(2)
    hist = CONV_W - 1
    base = 8

    @pl.when(t == 0)
    def _():
        xbuf[base - hist:base, :] = cprev_ref[0]
        h_scr[...] = hprev_ref[0]
        gw_scr[...] = gw_ref[...].astype(BF16)

    @pl.when(t > 0)
    def _():
        xbuf[base - hist:base, :] = xbuf[base + tt - hist:base + tt, :]

    xbuf[base:base + tt, :] = x_ref[0]
    conv = xbuf[base - hist:base - hist + tt, :] * cw_ref[0:1, :]
    for k in range(1, CONV_W):
        conv = conv + xbuf[base - hist + k:base - hist + k + tt, :] * cw_ref[k:k + 1, :]
    xc = cb_ref[...] + conv

    for blk in range(nbp):
        cols = slice(blk * bw, (blk + 1) * bw)
        a_scr[:, cols], b_scr[:, cols] = _lru_coeffs(
            xc[:, cols], gw_scr[0, blk], gw_scr[1, blk], gb_ref[:, cols], lam_ref[:, cols])

    def step(i, h):
        h = a_scr[pl.ds(i, 1), :] * h + b_scr[pl.ds(i, 1), :]
        hs_scr[pl.ds(i, 1), :] = h
        return h

    h_scr[...] = lax.fori_loop(0, tt, step, h_scr[...], unroll=min(tt, 8))
    hg_ref[0] = (hs_scr[...] * gy_ref[0].astype(F32)).astype(hg_ref.dtype)

    @pl.when(t == nt - 1)
    def _():
        nconv_ref[0] = xbuf[base + tt - hist:base + tt, :]
        hlast_ref[0] = h_scr[...]


def _lru(x, gy, conv_prev, h_prev, conv_w, conv_b, gate_w, gate_b, lam):
    b, t, w = x.shape
    nblk, bw = gate_w.shape[1], gate_w.shape[2]
    nbp = _pick(nblk, (4, 2, 1))
    cw = nbp * bw
    tt = _pick(t, (256, 128, 64, 32, 16, 8))
    nt = t // tt
    hist = CONV_W - 1
    kern = functools.partial(_lru_kernel, tt=tt, nt=nt, bw=bw, nbp=nbp)
    row = lambda bi, c, ti: (bi, ti, c)
    fixed = lambda bi, c, ti: (bi, 0, c)
    chan = lambda bi, c, ti: (0, c)
    hg, nconv, hlast = pl.pallas_call(
        kern,
        grid=(b, w // cw, nt),
        in_specs=[pl.BlockSpec((1, tt, cw), row), pl.BlockSpec((1, tt, cw), row),
                  pl.BlockSpec((1, hist, cw), fixed), pl.BlockSpec((1, 1, cw), fixed),
                  pl.BlockSpec((CONV_W, cw), chan), pl.BlockSpec((1, cw), chan),
                  pl.BlockSpec((2, nbp, bw, bw), lambda bi, c, ti: (0, c, 0, 0)),
                  pl.BlockSpec((2, cw), chan), pl.BlockSpec((1, cw), chan)],
        out_specs=[pl.BlockSpec((1, tt, cw), row), pl.BlockSpec((1, hist, cw), fixed),
                   pl.BlockSpec((1, 1, cw), fixed)],
        out_shape=[jax.ShapeDtypeStruct((b, t, w), BF16),
                   jax.ShapeDtypeStruct((b, hist, w), F32),
                   jax.ShapeDtypeStruct((b, 1, w), F32)],
        scratch_shapes=[pltpu.VMEM((8 + tt, cw), F32), pltpu.VMEM((2, nbp, bw, bw), BF16),
                        pltpu.VMEM((tt, cw), F32), pltpu.VMEM((tt, cw), F32),
                        pltpu.VMEM((tt, cw), F32), pltpu.VMEM((1, cw), F32)],
        compiler_params=_cparams(("parallel", "parallel", "arbitrary")),
        name="rglru",
    )(x, gy, conv_prev, h_prev.reshape(b, 1, w), conv_w, conv_b.reshape(1, w), gate_w, gate_b,
      lam.reshape(1, w))
    return hg, nconv, hlast.reshape(b, w)


def _cmp_pq_kernel(*refs, n_src, paged):
    if paged:
        refs = refs[1:]
        src, (w1_ref, o_ref) = refs[:n_src], refs[n_src:]
    else:
        src, (w1_ref, o_ref, slab) = refs[:n_src], refs[n_src:]
    half = CMP_STRIDE * HEAD_DIM
    slots = 2 * N_KV

    def chunk_rows(s, kv, g):
        if paged:
            n = s.shape[0] // (CMP_STRIDE * slots)
            pieces = [s[pl.ds(r * slots + kv * N_KV + g, n, stride=CMP_STRIDE * slots), :]
                      for r in range(CMP_STRIDE)]
        else:
            off = (kv * N_KV + g) * HEAD_DIM
            slab[...] = s[0, :, off:off + HEAD_DIM]
            n = slab.shape[0] // CMP_STRIDE
            pieces = [slab[pl.ds(r, n, stride=CMP_STRIDE), :] for r in range(CMP_STRIDE)]
        return jnp.concatenate([p.astype(BF16) for p in pieces], axis=1)

    for kv in range(2):
        w_pq = jnp.concatenate([w1_ref[kv, :half, :], w1_ref[kv, half:, :]], axis=1).astype(BF16)
        rows = [chunk_rows(s, kv, g) for g in range(N_KV) for s in src]
        pq = _dot(jnp.concatenate(rows, axis=0), w_pq)
        nc = pq.shape[0] // N_KV
        for g in range(N_KV):
            c0 = (kv * N_KV + g) * 2 * HEAD_DIM
            o_ref[0, :, c0:c0 + 2 * HEAD_DIM] = pq[g * nc:(g + 1) * nc, :]


def _cmp_pq(rows, w1, *, page_table=None, pages_per_step=16):
    out_cols = 2 * N_KV * 2 * HEAD_DIM
    if page_table is None:
        nb, t_len, _ = rows.shape
        nch = t_len // CMP_STRIDE
        tc = _pick(nch, (128, 64, 32, 16, 8))
        return pl.pallas_call(
            functools.partial(_cmp_pq_kernel, n_src=1, paged=False),
            grid=(nb, nch // tc),
            in_specs=[pl.BlockSpec((1, tc * CMP_STRIDE, KV_ROW), lambda b, c: (b, c, 0)),
                      pl.BlockSpec(w1.shape, lambda b, c: (0, 0, 0))],
            out_specs=pl.BlockSpec((1, tc, out_cols), lambda b, c: (b, c, 0)),
            out_shape=jax.ShapeDtypeStruct((nb, nch, out_cols), F32),
            scratch_shapes=[pltpu.VMEM((tc * CMP_STRIDE, HEAD_DIM), F32)],
            compiler_params=_cparams(("parallel", "parallel")),
            name="cmp_pq",
        )(rows, w1)
    nb, n_pages = page_table.shape
    page_rows = PAGE_SIZE * 2 * N_KV
    cpp = PAGE_SIZE // CMP_STRIDE
    pps = _pick(n_pages, (pages_per_step, 8, 4, 2, 1))
    src_specs = [pl.BlockSpec((page_rows, HEAD_DIM),
                              functools.partial(lambda b, s, pt, p: (pt[b, s * pps + p], 0), p=p))
                 for p in range(pps)]
    grid_spec = pltpu.PrefetchScalarGridSpec(
        num_scalar_prefetch=1,
        grid=(nb, n_pages // pps),
        in_specs=src_specs + [pl.BlockSpec(w1.shape, lambda b, s, pt: (0, 0, 0))],
        out_specs=pl.BlockSpec((1, pps * cpp, out_cols), lambda b, s, pt: (b, s, 0)),
    )
    return pl.pallas_call(
        functools.partial(_cmp_pq_kernel, n_src=pps, paged=True),
        grid_spec=grid_spec,
        out_shape=jax.ShapeDtypeStruct((nb, n_pages * cpp, out_cols), F32),
        compiler_params=_cparams(("parallel", "parallel")),
        name="cmp_pq_paged",
    )(page_table, *([rows] * pps), w1)


def _cmp_finish_kernel(pq_ref, pe_ref, w1_ref, b1_ref, w2_ref, b2_ref, kn_ref, kc_ref, vct_ref, *, nch):
    is_block = lax.broadcasted_iota(jnp.int32, (nch, 1), 0) < nch - 1
    for kv in range(2):
        pe8 = jnp.broadcast_to(pe_ref[kv], (8, pe_ref.shape[2])).astype(BF16)
        const = _dot(pe8, w1_ref[kv].astype(BF16))[0:1, :] + b1_ref[kv]
        w2 = w2_ref[kv].astype(BF16)
        for g in range(N_KV):
            c0 = (kv * N_KV + g) * 2 * HEAD_DIM
            nxt = jnp.concatenate([pq_ref[0, 1:nch, c0 + HEAD_DIM:c0 + 2 * HEAD_DIM],
                                   jnp.zeros((1, HEAD_DIM), F32)], axis=0)
            hid = pq_ref[0, :, c0:c0 + HEAD_DIM] + nxt + const
            out = _dot(jax.nn.gelu(hid).astype(BF16), w2) + b2_ref[kv]
            if kv == 0:
                out = out * lax.rsqrt(jnp.mean(out * out, axis=-1, keepdims=True) + EPS) * kn_ref[...]
            out = jnp.where(is_block, out, 0.0)
            if kv == 0:
                kc_ref[0, g] = out.astype(kc_ref.dtype)
            else:
                vct_ref[0, g] = out.T.astype(vct_ref.dtype)


def _cmp_finish(pq, pe, w1, b1, w2, b2, k_norm):
    nb, nch, cols = pq.shape
    full = lambda shape: pl.BlockSpec(shape, lambda b: (0,) * len(shape))
    return pl.pallas_call(
        functools.partial(_cmp_finish_kernel, nch=nch),
        grid=(nb,),
        in_specs=[pl.BlockSpec((1, nch, cols), lambda b: (b, 0, 0)),
                  full((2, 1, CMP_BLOCK * HEAD_DIM)), full(w1.shape), full((2, 1, HEAD_DIM)),
                  full(w2.shape), full((2, 1, HEAD_DIM)), full((1, HEAD_DIM))],
        out_specs=[pl.BlockSpec((1, N_KV, nch, HEAD_DIM), lambda b: (b, 0, 0, 0)),
                   pl.BlockSpec((1, N_KV, HEAD_DIM, nch), lambda b: (b, 0, 0, 0))],
        out_shape=[jax.ShapeDtypeStruct((nb, N_KV, nch, HEAD_DIM), BF16),
                   jax.ShapeDtypeStruct((nb, N_KV, HEAD_DIM, nch), BF16)],
        compiler_params=_cparams(("parallel",)),
        name="cmp_finish",
    )(pq, pe.reshape(2, 1, -1), w1, b1.reshape(2, 1, -1), w2, b2.reshape(2, 1, -1),
      k_norm.reshape(1, HEAD_DIM))


def _per_head(fn, x, hpg):
    tq = x.shape[1] // hpg
    return jnp.concatenate([fn(x[:, h * tq:(h + 1) * tq]) for h in range(hpg)], axis=1)


def _flash_update(s, m_scr, acc_scr, v_aug):
    m_old = m_scr[...]
    m_new = jnp.maximum(m_old, jnp.max(s, axis=0, keepdims=True))
    p = jnp.exp((s - m_new).astype(BF16))
    acc_scr[...] = jnp.exp(m_old - m_new) * acc_scr[...] + _dot(v_aug, p)
    m_scr[...] = m_new


def _nsa_prompt_kernel(q_ref, gt_ref, kc_ref, vct_ref, ks_ref, vs_ref, kw_ref, vw_ref, o_ref,
                       ks_scr, vst_scr, kw_scr, vwt_scr, qt_scr, score_scr, sel_scr,
                       bias_scr, wbias_scr, m_scr, acc_scr, o_scr, *, tq, t_len, hpg, n_cmp_pad):
    g = pl.program_id(1)
    qi = pl.program_id(2)
    n_chunks = t_len // KEY_CHUNK
    n_slc = t_len // SLC_BLOCK
    win_chunks = WINDOW // KEY_CHUNK
    n_wb = tq // KEY_CHUNK + win_chunks
    nl = hpg * tq

    @pl.when(qi == 0)
    def _():
        for c in range(n_chunks):
            rows = slice(c * KEY_CHUNK, (c + 1) * KEY_CHUNK)
            ks_scr[rows, :] = ks_ref[0, rows, :].astype(BF16)
            kw_scr[rows, :] = kw_ref[0, rows, :].astype(BF16)
            vst_scr[0:HEAD_DIM, rows] = vs_ref[0, rows, :].T.astype(BF16)
            vwt_scr[0:HEAD_DIM, rows] = vw_ref[0, rows, :].T.astype(BF16)
        ones = jnp.ones((ONES_ROWS, t_len), BF16)
        vst_scr[HEAD_DIM:, :] = ones
        vwt_scr[HEAD_DIM:, :] = ones
        kl = lax.broadcasted_iota(jnp.int32, (KEY_CHUNK, tq), 0)
        for c in range(n_wb):
            dlt = (win_chunks - c) * KEY_CHUNK + lax.broadcasted_iota(jnp.int32, (KEY_CHUNK, tq), 1) - kl
            wbias_scr[c] = jnp.where((dlt >= 0) & (dlt < WINDOW), 0.0, NEG)

    for h in range(hpg):
        qt_scr[:, h * tq:(h + 1) * tq] = (
            q_ref[:, h * HEAD_DIM:(h + 1) * HEAD_DIM].astype(F32).T.astype(BF16))
    qpos = qi * tq + lax.broadcasted_iota(jnp.int32, (1, tq), 1)

    def gate_row(branch):
        return jnp.concatenate(
            [gt_ref[pl.ds((g * hpg + h) * N_BRANCH + branch, 1), :] for h in range(hpg)], axis=1)

    cmp_idx = lax.broadcasted_iota(jnp.int32, (n_cmp_pad, tq), 0)
    cmp_mask = cmp_idx * CMP_STRIDE + (CMP_BLOCK - 1) <= qpos
    s = _per_head(lambda x: jnp.where(cmp_mask, x, NEG), _dot(kc_ref[0, 0], qt_scr[...]), hpg)
    e = _per_head(lambda x: jnp.where(cmp_mask, x, 0.0),
                  jnp.exp(s - jnp.max(s, axis=0, keepdims=True)), hpg)
    p = e / jnp.maximum(jnp.sum(e, axis=0, keepdims=True), TINY)
    o_scr[...] = gate_row(0) * _dot(vct_ref[0, 0], p.astype(BF16))
    p_grp = p[:, 0:tq]
    for h in range(1, hpg):
        p_grp = p_grp + p[:, h * tq:(h + 1) * tq]

    ratio = SLC_BLOCK // CMP_STRIDE
    jj = lax.broadcasted_iota(jnp.int32, (n_slc, n_cmp_pad), 0)
    mm = lax.broadcasted_iota(jnp.int32, (n_slc, n_cmp_pad), 1)
    pool = ((mm >= ratio * jj - 1) & (mm <= ratio * jj + ratio - 1)).astype(BF16)
    p_slc = sum(_dot(pool, piece) for piece in _split3(p_grp))
    blk = lax.broadcasted_iota(jnp.int32, (n_slc, tq), 0)
    cur = lax.shift_right_arithmetic(qpos, SLC_SHIFT)
    forced = (blk == 0) | (blk == cur) | (blk == cur - 1)
    valid = blk * SLC_BLOCK <= qpos
    score = jnp.where(forced, FORCE, jnp.where(valid, p_slc, -1.0))
    score_scr[...] = score

    def rank_step(j, rank):
        other = score_scr[pl.ds(j, 1), :]
        beats = (other > score) | ((other == score) & (j < blk))
        return rank + beats.astype(jnp.int32)

    rank = lax.fori_loop(0, n_slc, rank_step, jnp.zeros((n_slc, tq), jnp.int32))
    sel_scr[...] = (rank < min(N_SEL, n_slc)).astype(F32)

    def slc_bias(c, _):
        for r in range(SLC_PER_CHUNK):
            sel_row = sel_scr[pl.ds(c * SLC_PER_CHUNK + r, 1), :]
            kpos = c * KEY_CHUNK + r * SLC_BLOCK + lax.broadcasted_iota(jnp.int32, (SLC_BLOCK, tq), 0)
            ok = (sel_row > 0.5) & (kpos <= qpos)
            bias_scr[pl.ds(pl.multiple_of(c * KEY_CHUNK + r * SLC_BLOCK, SLC_BLOCK), SLC_BLOCK), :] = (
                jnp.where(ok, 0.0, NEG))
        return 0

    n_live = (qi * tq + tq + KEY_CHUNK - 1) // KEY_CHUNK
    n_pairs = (n_live + 1) // 2
    lax.fori_loop(0, 2 * n_pairs, slc_bias, 0)

    def scores(k_scr, chunk, bias):
        keys = pl.ds(pl.multiple_of(chunk * KEY_CHUNK, KEY_CHUNK), KEY_CHUNK)
        return _per_head(lambda x: x + bias(keys), _dot(k_scr[keys, :], qt_scr[...]), hpg), keys

    def reset():
        m_scr[...] = jnp.full((1, nl), NEG, F32)
        acc_scr[...] = jnp.zeros((HEAD_DIM + ONES_ROWS, nl), F32)

    def add_branch(branch):
        inv = 1.0 / jnp.maximum(acc_scr[HEAD_DIM:HEAD_DIM + 1, :], TINY)
        o_scr[...] += (gate_row(branch) * inv) * acc_scr[0:HEAD_DIM, :]

    reset()

    def slc_pair(i, _):
        tiles = [scores(ks_scr, 2 * i + u, lambda keys: bias_scr[keys, :]) for u in range(2)]
        for s, keys in tiles:
            _flash_update(s, m_scr, acc_scr, vst_scr[:, keys])
        return 0

    lax.fori_loop(0, n_pairs, slc_pair, 0)
    add_branch(1)

    reset()
    first_key_chunk = qi * (tq // KEY_CHUNK) - win_chunks
    tiles = []
    for c in reversed(range(n_wb)):
        chunk = first_key_chunk + c
        band = jnp.where(chunk >= 0, wbias_scr[c], NEG)
        tiles.append(scores(kw_scr, jnp.maximum(chunk, 0), lambda keys: band))
    for s, keys in tiles:
        _flash_update(s, m_scr, acc_scr, vwt_scr[:, keys])
    add_branch(2)

    for h in range(hpg):
        o_ref[:, h * HEAD_DIM:(h + 1) * HEAD_DIM] = o_scr[:, h * tq:(h + 1) * tq].T.astype(o_ref.dtype)


def _nsa_prompt(q, gates_t, kc, vct, slc_rows, win_rows, bsz, t_len):
    m, qcols = q.shape
    hpg = qcols // (N_KV * HEAD_DIM)
    gw = hpg * HEAD_DIM
    tq = KEY_CHUNK
    assert t_len % (2 * KEY_CHUNK) == 0 and WINDOW % KEY_CHUNK == 0 and tq % KEY_CHUNK == 0
    nq = t_len // tq
    n_cmp_pad = kc.shape[2]
    n_slc = t_len // SLC_BLOCK
    n_wb = tq // KEY_CHUNK + WINDOW // KEY_CHUNK
    kern = functools.partial(_nsa_prompt_kernel, tq=tq, t_len=t_len, hpg=hpg, n_cmp_pad=n_cmp_pad)
    kv_k = lambda b, g, i: (b, 0, g)
    kv_v = lambda b, g, i: (b, 0, N_KV + g)
    return pl.pallas_call(
        kern,
        grid=(bsz, N_KV, nq),
        in_specs=[pl.BlockSpec((tq, gw), lambda b, g, i: (b * nq + i, g)),
                  pl.BlockSpec((gates_t.shape[0], tq), lambda b, g, i: (0, b * nq + i)),
                  pl.BlockSpec((1, 1, n_cmp_pad, HEAD_DIM), lambda b, g, i: (b, g, 0, 0)),
                  pl.BlockSpec((1, 1, HEAD_DIM, n_cmp_pad), lambda b, g, i: (b, g, 0, 0)),
                  pl.BlockSpec((1, t_len, HEAD_DIM), kv_k), pl.BlockSpec((1, t_len, HEAD_DIM), kv_v),
                  pl.BlockSpec((1, t_len, HEAD_DIM), kv_k), pl.BlockSpec((1, t_len, HEAD_DIM), kv_v)],
        out_specs=pl.BlockSpec((tq, gw), lambda b, g, i: (b * nq + i, g)),
        out_shape=jax.ShapeDtypeStruct((m, qcols), BF16),
        scratch_shapes=[pltpu.VMEM((t_len, HEAD_DIM), BF16), pltpu.VMEM((HEAD_DIM + ONES_ROWS, t_len), BF16),
                        pltpu.VMEM((t_len, HEAD_DIM), BF16), pltpu.VMEM((HEAD_DIM + ONES_ROWS, t_len), BF16),
                        pltpu.VMEM((HEAD_DIM, hpg * tq), BF16),
                        pltpu.VMEM((n_slc, tq), F32), pltpu.VMEM((n_slc, tq), F32),
                        pltpu.VMEM((t_len, tq), F32), pltpu.VMEM((n_wb, KEY_CHUNK, tq), F32),
                        pltpu.VMEM((1, hpg * tq), F32),
                        pltpu.VMEM((HEAD_DIM + ONES_ROWS, hpg * tq), F32),
                        pltpu.VMEM((HEAD_DIM, hpg * tq), F32)],
        compiler_params=_cparams(("parallel", "parallel", "arbitrary")),
        name="nsa_prompt",
    )(q, gates_t, kc, vct, slc_rows, slc_rows, win_rows, win_rows)


def _nsa_sample_select_kernel(q_ref, kc_ref, vct_ref, ocmp_ref, idx_ref, *, qpos, n_slc, nsp):
    q = q_ref[0, 0].astype(BF16)
    ncp = kc_ref.shape[2]
    s = _dot_nt(q, kc_ref[0, 0])
    m_idx = lax.broadcasted_iota(jnp.int32, (1, ncp), 1)
    mask = m_idx * CMP_STRIDE + (CMP_BLOCK - 1) <= qpos
    s = jnp.where(mask, s, NEG)
    e = jnp.where(mask, jnp.exp(s - jnp.max(s, axis=-1, keepdims=True)), 0.0)
    p = e / jnp.maximum(jnp.sum(e, axis=-1, keepdims=True), TINY)
    ocmp_ref[0, 0] = _dot_nt(p.astype(BF16), vct_ref[0, 0])

    ratio = SLC_BLOCK // CMP_STRIDE
    p_grp = jnp.broadcast_to(jnp.sum(p, axis=0, keepdims=True), (8, ncp))
    mm = lax.broadcasted_iota(jnp.int32, (ncp, nsp), 0)
    jj = lax.broadcasted_iota(jnp.int32, (ncp, nsp), 1)
    pool_t = ((mm >= ratio * jj - 1) & (mm <= ratio * jj + ratio - 1)).astype(BF16)
    p_slc = sum(_dot(piece, pool_t) for piece in _split3(p_grp))[0:1, :]
    blk = lax.broadcasted_iota(jnp.int32, (1, nsp), 1)
    cur = qpos // SLC_BLOCK
    forced = (blk == 0) | (blk == cur) | (blk == cur - 1)
    score = jnp.where(forced, FORCE, jnp.where(blk * SLC_BLOCK <= qpos, p_slc, -1.0))
    score = jnp.where(blk < n_slc, score, -2.0)

    ii = lax.broadcasted_iota(jnp.int32, (nsp, nsp), 0)
    jx = lax.broadcasted_iota(jnp.int32, (nsp, nsp), 1)
    mine = jnp.broadcast_to(score, (nsp, nsp))
    other = mine.T
    beats = (other > mine) | ((other == mine) & (ii < jx))
    rank = jnp.sum(beats.astype(F32), axis=0, keepdims=True)
    sel = (rank < float(N_SEL)).astype(F32)
    sel_other = jnp.broadcast_to(sel, (nsp, nsp)).T
    pos = jnp.sum(jnp.where(ii < jx, sel_other, 0.0), axis=0, keepdims=True)
    kk = lax.broadcasted_iota(jnp.int32, (N_SEL, nsp), 0).astype(F32)
    jrow = lax.broadcasted_iota(jnp.int32, (N_SEL, nsp), 1).astype(F32)
    hit = (sel > 0.5) & (pos == kk)
    idx = jnp.sum(jnp.where(hit, jrow, 0.0), axis=1, keepdims=True)
    idx_ref[0, 0] = jnp.broadcast_to(idx, (N_SEL, HEAD_DIM)).astype(jnp.int32)


def _nsa_sample_select(q, kc, vct, qpos, n_slc):
    dbs, _, hpg, _ = q.shape
    ncp = kc.shape[2]
    nsp = -(-n_slc // HEAD_DIM) * HEAD_DIM
    assert n_slc >= N_SEL
    blk4 = lambda s2, s3: pl.BlockSpec((1, 1, s2, s3), lambda b, g: (b, g, 0, 0))
    ocmp, idx = pl.pallas_call(
        functools.partial(_nsa_sample_select_kernel, qpos=qpos, n_slc=n_slc, nsp=nsp),
        grid=(dbs, N_KV),
        in_specs=[blk4(hpg, HEAD_DIM), blk4(ncp, HEAD_DIM), blk4(HEAD_DIM, ncp)],
        out_specs=[blk4(hpg, HEAD_DIM), blk4(N_SEL, HEAD_DIM)],
        out_shape=[jax.ShapeDtypeStruct((dbs, N_KV, hpg, HEAD_DIM), F32),
                   jax.ShapeDtypeStruct((dbs, N_KV, N_SEL, HEAD_DIM), jnp.int32)],
        compiler_params=_cparams(("parallel", "parallel")),
        name="nsa_sample_select",
    )(q, kc, vct)
    return ocmp, idx[..., 0]


def _nsa_sample_attend_kernel(pt_ref, idx_ref, q_ref, gate_ref, ocmp_ref, snew_ref, wnew_ref,
                              win_ref, *rest, qpos, n_cached, win_buf):
    blocks, o_ref = rest[:N_SEL], rest[N_SEL]
    b, g = pl.program_id(0), pl.program_id(1)
    slots = 2 * N_KV

    def group_rows(ref, n, col):
        return ref[pl.ds(col * N_KV + g, n, stride=slots), :]

    q = q_ref[0, 0].astype(BF16)
    qf = q.astype(F32)

    def attend(s, mask, v, k_new, v_new):
        s_self = jnp.sum(qf * k_new.astype(BF16).astype(F32), axis=-1, keepdims=True)
        s = jnp.where(mask, s, NEG)
        m = jnp.maximum(jnp.max(s, axis=-1, keepdims=True), s_self)
        e = jnp.where(mask, jnp.exp(s - m), 0.0)
        e_self = jnp.exp(s_self - m)
        l = jnp.sum(e, axis=-1, keepdims=True) + e_self
        acc = _dot(e.astype(BF16), v) + e_self.astype(BF16).astype(F32) * v_new.astype(BF16).astype(F32)
        return acc / jnp.maximum(l, TINY)

    n_keys = N_SEL * SLC_BLOCK
    k_all = jnp.concatenate([group_rows(blk, SLC_BLOCK, 0) for blk in blocks], axis=0).astype(BF16)
    v_all = jnp.concatenate([group_rows(blk, SLC_BLOCK, 1) for blk in blocks], axis=0).astype(BF16)
    lane = lax.broadcasted_iota(jnp.int32, (1, n_keys), 1)
    slot = lax.shift_right_arithmetic(lane, SLC_SHIFT)
    blk_id = jnp.zeros((1, n_keys), jnp.int32)
    for k in range(N_SEL):
        blk_id = jnp.where(slot == k, idx_ref[(b * N_KV + g) * N_SEL + k], blk_id)
    kpos = blk_id * SLC_BLOCK + (lane & (SLC_BLOCK - 1))
    slc_mask = (blk_id < n_cached) & (kpos <= qpos)
    o_slc = attend(_dot_nt(q, k_all), slc_mask, v_all,
                   snew_ref[0, pl.ds(g, 1), :], snew_ref[0, pl.ds(N_KV + g, 1), :])

    wi = lax.broadcasted_iota(jnp.int32, (1, win_buf), 1)
    dlt = win_buf - wi
    win_mask = (dlt >= 0) & (dlt < WINDOW) & (qpos - dlt >= 0)
    o_win = attend(_dot_nt(q, group_rows(win_ref, win_buf, 0).astype(BF16)), win_mask,
                   group_rows(win_ref, win_buf, 1).astype(BF16),
                   wnew_ref[0, pl.ds(g, 1), :], wnew_ref[0, pl.ds(N_KV + g, 1), :])

    gate = gate_ref[0, 0]
    o_ref[0, 0] = gate[:, 0:1] * ocmp_ref[0, 0] + gate[:, 1:2] * o_slc + gate[:, 2:3] * o_win


def _nsa_sample_attend(q, gates, ocmp, idx, page_table, slc_cache, slc_new, win_cache, win_new, qpos):
    dbs, _, hpg, _ = q.shape
    slots = 2 * N_KV
    win_buf = win_cache.shape[0] // (dbs * slots)
    n_cached = page_table.shape[1] * PAGE_SIZE // SLC_BLOCK
    bpp = PAGE_SIZE // SLC_BLOCK

    def blk_map(b, g, pt, ix, *, k):
        blk = jnp.minimum(ix[(b * N_KV + g) * N_SEL + k], n_cached - 1)
        return pt[b, blk // bpp] * bpp + blk % bpp, 0

    blk4 = lambda s2, s3: pl.BlockSpec((1, 1, s2, s3), lambda b, g, pt, ix: (b, g, 0, 0))
    new_spec = pl.BlockSpec((1, slots, HEAD_DIM), lambda b, g, pt, ix: (b, 0, 0))
    win_spec = pl.BlockSpec((win_buf * slots, HEAD_DIM), lambda b, g, pt, ix: (b, 0))
    gathered = [pl.BlockSpec((SLC_BLOCK * slots, HEAD_DIM), functools.partial(blk_map, k=k))
                for k in range(N_SEL)]
    grid_spec = pltpu.PrefetchScalarGridSpec(
        num_scalar_prefetch=2,
        grid=(dbs, N_KV),
        in_specs=[blk4(hpg, HEAD_DIM), blk4(hpg, N_BRANCH), blk4(hpg, HEAD_DIM), new_spec, new_spec,
                  win_spec] + gathered,
        out_specs=blk4(hpg, HEAD_DIM),
    )
    return pl.pallas_call(
        functools.partial(_nsa_sample_attend_kernel, qpos=qpos, n_cached=n_cached, win_buf=win_buf),
        grid_spec=grid_spec,
        out_shape=jax.ShapeDtypeStruct((dbs, N_KV, hpg, HEAD_DIM), F32),
        compiler_params=_cparams(("parallel", "parallel")),
        name="nsa_sample_attend",
    )(page_table, idx.reshape(-1), q, gates, ocmp,
      slc_new.reshape(dbs, slots, HEAD_DIM), win_new.reshape(dbs, slots, HEAD_DIM),
      win_cache, *([slc_cache] * N_SEL))


def kernel(x_prompt, x_sample, c_prompt, c_sample, state_lru_h, state_conv, cache_cmp_kv, cache_slc_kv, cache_win_kv, page_table, ada_w, ada_b, norm1_w, norm2_w, lru_w_in, lru_b_in, lru_conv_w, lru_conv_b, lru_gate_w, lru_gate_b, lru_lambda, lru_w_out, lru_b_out, ffn_w13, ffn_w2, kv_ada_w, kv_ada_b, kv_norm_w, w_kv, k_norm_w, cmp_pe, cmp_w1, cmp_b1, cmp_w2, cmp_b2, w_qg, q_norm_w, w_o):
    bsz, t_len, d = x_prompt.shape
    dbs, dec_seq, _ = x_sample.shape
    depth = ada_w.shape[0]
    n_a = lru_w_in.shape[0]
    assert dec_seq == 1 and depth == 2 and n_a == 1 and w_qg.shape[0] == 1
    lw = lru_w_in.shape[2] // 2
    d_ff = ffn_w2.shape[1]
    n_pages = page_table.shape[1]
    past_len = n_pages * PAGE_SIZE
    qcols = w_o.shape[1]
    hpg = qcols // (N_KV * HEAD_DIM)
    hist = CONV_W - 1
    branch_cols = KV_ROW

    n_c = bsz + dbs
    c_all = jnp.pad(jnp.concatenate([c_prompt, c_sample], axis=0), ((0, (-n_c) % 8), (0, 0)))
    mods = [_matmul(c_all, ada_w, layer=l, bias=ada_b[l], a_silu=True, tn=1024, name="ada")
            for l in range(depth)]
    kv_mod = _matmul(c_all, kv_ada_w, bias=kv_ada_b, a_silu=True, tn=1024, name="kv_ada")

    def split_mod(mat, n, prompt):
        parts = [mat[:, i * d:(i + 1) * d] for i in range(n)]
        if prompt:
            return [p[:bsz].reshape(bsz, 1, d) for p in parts]
        return [p[bsz:n_c].reshape(1, dbs, d) for p in parts]

    w13 = ffn_w13
    w2 = ffn_w2.astype(BF16)
    tk2 = _pick(d_ff, (d_ff // 2,)) if d_ff > 4096 else d_ff
    w_in, w_out, wq, wo = lru_w_in, lru_w_out, w_qg, w_o
    wg_t = w_qg[0, :, qcols:].T
    pe = cmp_pe.reshape(2, -1)

    def ffn(x, l, mod, act_dtype):
        u = _norm_mod(x, norm2_w[l], mod[3], mod[4], act_dtype)
        act = _matmul(u, w13, layer=l, col0=0, col0_b=d_ff, ncols=d_ff, epi="swiglu", tn=256,
                      out_dtype=act_dtype, name="ffn_up")
        return _matmul(act, w2, layer=l, tk=tk2, epi="residual", res=x, gate=mod[5],
                       name="ffn_down")

    def lru_in(x, mod, act_dtype):
        u = _norm_mod(x, norm1_w[0], mod[0], mod[1], act_dtype)
        gy = _matmul(u, w_in, col0=0, ncols=lw, bias=lru_b_in[0], epi="gelu", out_dtype=act_dtype,
                     name="lru_in_y")
        xb = _matmul(u, w_in, col0=lw, ncols=lw, bias=lru_b_in[0], name="lru_in_x")
        return gy, xb

    def shared_kv(x, mod, act_dtype):
        s = _norm_mod(x, kv_norm_w, mod[0], mod[1], act_dtype)
        outs = []
        for br in range(N_BRANCH):
            if br == 0:
                outs.append(_matmul(s, w_kv, col0=0, ncols=branch_cols, tn=512, name="kv_cmp"))
            else:
                outs.append(_matmul(s, w_kv, col0=br * branch_cols, ncols=branch_cols, tn=512,
                                    epi="headnorm", hn_w=k_norm_w[br],
                                    n_norm_tiles=branch_cols // 2 // 512, name="kv_norm"))
        return outs

    def nsa_query(x, mod, act_dtype):
        u = _norm_mod(x, norm1_w[1], mod[0], mod[1], act_dtype)
        q = _matmul(u, wq, col0=0, ncols=qcols, epi="headnorm", hn_w=q_norm_w[0],
                    n_norm_tiles=qcols // 512, post_scale=HEAD_DIM ** -0.5, tn=512,
                    out_dtype=act_dtype, name="nsa_q")
        return q, _branch_gates_t(u, wg_t)

    lru_args = (lru_conv_w[0], lru_conv_b[0], lru_gate_w[0], lru_gate_b[0], lru_lambda[0])

    mp = [split_mod(m, 6, True) for m in mods]
    xp = x_prompt.reshape(bsz * t_len, d)
    gy, xb = lru_in(xp, mp[0], BF16)
    hg, conv_p, h_p = _lru(xb.reshape(bsz, t_len, lw), gy.reshape(bsz, t_len, lw),
                           jnp.zeros((bsz, hist, lw), F32), jnp.zeros((bsz, lw), F32), *lru_args)
    xp = _matmul(hg.reshape(bsz * t_len, lw), w_out, bias=lru_b_out[0], epi="residual", res=xp,
                 gate=mp[0][2], name="lru_out")
    xp = ffn(xp, 0, mp[0], BF16)
    cmp_p, slc_p, win_p = shared_kv(xp, split_mod(kv_mod, 2, True), BF16)
    q_p, gt_p = nsa_query(xp, mp[1], BF16)
    pq_p = _cmp_pq(cmp_p.reshape(bsz, t_len, KV_ROW), cmp_w1)
    kc_p, vct_p = _cmp_finish(pq_p, pe, cmp_w1, cmp_b1, cmp_w2, cmp_b2, k_norm_w[0])
    o_p = _nsa_prompt(q_p, gt_p, kc_p, vct_p, slc_p.reshape(bsz, t_len, KV_ROW),
                      win_p.reshape(bsz, t_len, KV_ROW), bsz, t_len)
    xp = _matmul(o_p, wo, epi="residual", res=xp, gate=mp[1][2], name="nsa_out")
    y_prompt = ffn(xp, 1, mp[1], BF16).reshape(bsz, t_len, d)

    ms = [split_mod(m, 6, False) for m in mods]
    xs = x_sample.reshape(dbs, d)
    gy, xb = lru_in(xs, ms[0], F32)
    hg, conv_s, h_s = _lru_step(xb, gy, jnp.swapaxes(state_conv[0], 0, 1), state_lru_h[0], *lru_args)
    xs = _matmul(hg, w_out, bias=lru_b_out[0], epi="residual", res=xs, gate=ms[0][2], name="lru_out")
    xs = ffn(xs, 0, ms[0], F32)
    cmp_s, slc_s, win_s = shared_kv(xs, split_mod(kv_mod, 2, False), F32)
    q_s, gt_s = nsa_query(xs, ms[1], F32)
    pq_s = _cmp_pq(cache_cmp_kv.reshape(-1, HEAD_DIM), cmp_w1, page_table=page_table)
    kc_s, vct_s = _cmp_finish(pq_s, pe, cmp_w1, cmp_b1, cmp_w2, cmp_b2, k_norm_w[0])
    n_slc_s = -(-(past_len + dec_seq) // SLC_BLOCK)
    q_s4 = q_s.reshape(dbs, N_KV, hpg, HEAD_DIM)
    ocmp_s, idx_s = _nsa_sample_select(q_s4, kc_s, vct_s, past_len, n_slc_s)
    gates_s = gt_s.T.reshape(dbs, N_KV, hpg, N_BRANCH)
    o_s = _nsa_sample_attend(q_s4, gates_s, ocmp_s, idx_s, page_table,
                             cache_slc_kv.reshape(-1, HEAD_DIM), slc_s,
                             cache_win_kv.reshape(-1, HEAD_DIM), win_s, past_len)
    xs = _matmul(o_s.reshape(dbs, qcols), wo, epi="residual", res=xs, gate=ms[1][2], name="nsa_out")
    y_sample = ffn(xs, 1, ms[1], F32).reshape(dbs, dec_seq, d)

    kv5 = lambda a, n, t: a.reshape(n, t, 2, N_KV, HEAD_DIM)
    win_buf = cache_win_kv.shape[1]
    win_s5 = kv5(win_s, dbs, 1)
    new_win_s = jnp.concatenate([cache_win_kv, win_s5], axis=1)[:, -win_buf:]
    return (y_prompt, y_sample, h_p[None], h_s[None], conv_p[None],
            jnp.swapaxes(conv_s, 0, 1)[None],
            kv5(cmp_p, bsz, t_len), kv5(cmp_s, dbs, 1), kv5(slc_p, bsz, t_len), kv5(slc_s, dbs, 1),
            kv5(win_p, bsz, t_len)[:, -WINDOW:], new_win_s)
```

```python
import functools

import jax
import jax.numpy as jnp
from jax import lax
from jax.experimental import pallas as pl
from jax.experimental.pallas import tpu as pltpu

F32 = jnp.float32
BF16 = jnp.bfloat16

HEAD_DIM = 128
N_KV = 4
N_BRANCH = 3
CMP_BLOCK = 32
CMP_STRIDE = 16
SLC_BLOCK = 64
N_SEL = 16
WINDOW = 512
CONV_W = 4
LRU_C = 8.0
PAGE_SIZE = 128
EPS = 1e-6
NEG = -1e30
FORCE = 1e4
TINY = 1e-30

KV_ROW = 2 * N_KV * HEAD_DIM
SLC_SHIFT = SLC_BLOCK.bit_length() - 1
SLC_PER_CHUNK = 4
KEY_CHUNK = SLC_PER_CHUNK * SLC_BLOCK
ONES_ROWS = 16
VMEM_LIMIT = 60 * 1024 * 1024


def _cparams(sem, vmem=None):
    return pltpu.CompilerParams(dimension_semantics=sem, vmem_limit_bytes=vmem or VMEM_LIMIT)


def _pick(n, cands):
    for c in cands:
        if n % c == 0:
            return c
    return n


def _dot(a, b):
    return jnp.dot(a, b, preferred_element_type=F32)


def _dot_nt(a, b):
    return lax.dot_general(a, b, (((1,), (1,)), ((), ())), preferred_element_type=F32)


def _sigmoid(x):
    return 0.5 * (jnp.tanh(0.5 * x) + 1.0)


def _split3(x):
    hi = x.astype(BF16)
    r1 = x - hi.astype(F32)
    mid = r1.astype(BF16)
    lo = (r1 - mid.astype(F32)).astype(BF16)
    return hi, mid, lo


def _norm_mod_kernel(x_ref, w_ref, shift_ref, scale_ref, o_ref):
    x = x_ref[...]
    y = x * lax.rsqrt(jnp.mean(x * x, axis=-1, keepdims=True) + EPS)
    y = y * w_ref[...]
    o_ref[...] = (y * (1.0 + scale_ref[0]) + shift_ref[0]).astype(o_ref.dtype)


def _norm_mod(x, w, shift, scale, out_dtype):
    m, d = x.shape
    nb, rb, _ = shift.shape
    rows_per_nb = m // nb
    tr = rows_per_nb if rb > 1 else _pick(rows_per_nb, (512, 256, 128, 64, 32, 16, 8))
    tiles_per_nb = rows_per_nb // tr
    mod_spec = pl.BlockSpec((1, rb, d), lambda i: (i // tiles_per_nb, 0, 0))
    return pl.pallas_call(
        _norm_mod_kernel,
        grid=(m // tr,),
        in_specs=[pl.BlockSpec((tr, d), lambda i: (i, 0)),
                  pl.BlockSpec((1, d), lambda i: (0, 0)), mod_spec, mod_spec],
        out_specs=pl.BlockSpec((tr, d), lambda i: (i, 0)),
        out_shape=jax.ShapeDtypeStruct((m, d), out_dtype),
        compiler_params=_cparams(("parallel",)),
        name="norm_mod",
    )(x, w.reshape(1, d), shift, scale)


def _head_rms(acc, w, post_scale):
    outs = []
    for h in range(acc.shape[1] // HEAD_DIM):
        sl = acc[:, h * HEAD_DIM:(h + 1) * HEAD_DIM]
        y = sl * lax.rsqrt(jnp.mean(sl * sl, axis=-1, keepdims=True) + EPS)
        y = y * w
        if post_scale is not None:
            y = y * post_scale
        outs.append(y)
    return jnp.concatenate(outs, axis=1)


def _mm_kernel(*refs, nk, ni, a_silu, has_bias, epi, n_norm_tiles, post_scale, has_small):
    it = iter(refs)
    a_ref, w_ref = next(it), next(it)
    w2_ref = next(it) if epi == "swiglu" else None
    b_ref = next(it) if has_bias else None
    res_ref, gate_ref = (next(it), next(it)) if epi == "residual" else (None, None)
    hn_ref = next(it) if epi == "headnorm" else None
    a2_ref = next(it) if has_small else None
    res2_ref, gate2_ref = (next(it), next(it)) if has_small and epi == "residual" else (None, None)
    o_ref = next(it)
    o2_ref = next(it) if has_small else None
    acc_ref = next(it) if nk > 1 else None
    acc2_ref = next(it) if nk > 1 and has_small else None

    w_bf = w_ref[...].astype(BF16)

    def run(lhs_ref, res, gate, out, acc_scr):
        a = lhs_ref[...]
        if a_silu:
            a = a * _sigmoid(a)
        a = a.astype(BF16)
        part = _dot(a, w_bf)

        def finish(acc):
            if has_bias:
                acc = acc + b_ref[...]
            if epi == "gelu":
                acc = jax.nn.gelu(acc)
            elif epi == "swiglu":
                acc = acc * _sigmoid(acc) * _dot(a, w2_ref[...].astype(BF16))
            elif epi == "residual":
                acc = res[...] + gate[0] * acc
            elif epi == "headnorm":
                normed = _head_rms(acc, hn_ref[...], post_scale)
                acc = jnp.where(pl.program_id(1) < n_norm_tiles, normed, acc)
            out[...] = acc.astype(out.dtype)

        if nk == 1:
            finish(part)
        else:
            k = pl.program_id(2)

            @pl.when(k == 0)
            def _():
                acc_scr[...] = part

            @pl.when(k > 0)
            def _():
                acc_scr[...] += part

            @pl.when(k == nk - 1)
            def _():
                finish(acc_scr[...])

    run(a_ref, res_ref, gate_ref, o_ref, acc_ref)
    if has_small:
        last = pl.program_id(0) == ni - 1

        @pl.when(last)
        def _():
            run(a2_ref, res2_ref, gate2_ref, o2_ref, acc2_ref)

        @pl.when(jnp.logical_not(last))
        def _():
            o2_ref[...] = jnp.zeros(o2_ref.shape, o2_ref.dtype)


def _matmul(a, w, *, layer=0, col0=0, ncols=None, col0_b=None, bias=None, epi="none", res=None,
            gate=None, hn_w=None, n_norm_tiles=0, post_scale=None, a_silu=False, out_dtype=F32,
            small=None, tm=None, tn=None, tk=None, single_buffer_a=False, name="matmul"):
    m, kdim = a.shape
    ncols = ncols if ncols is not None else w.shape[-1]
    tm = tm or min(m, 1024, m // gate.shape[0] if gate is not None and gate.shape[1] == 1 else m)
    tn = tn or _pick(ncols, (512, 256, 128))
    tk = tk or kdim
    nk = kdim // tk
    assert w.shape[-2] == kdim
    assert m % tm == 0 and ncols % tn == 0 and kdim % tk == 0 and col0 % tn == 0
    assert epi != "swiglu" or (nk == 1 and col0_b % tn == 0)
    jo = col0 // tn
    spec = pl.BlockSpec

    def w_spec(j0):
        if w.ndim == 3:
            return spec((None, tk, tn), lambda i, j, k: (layer, k, j + j0))
        return spec((tk, tn), lambda i, j, k: (k, j + j0))

    a_mode = dict(pipeline_mode=pl.Buffered(1)) if single_buffer_a and nk == 1 else {}
    in_specs = [spec((tm, tk), lambda i, j, k: (i, k), **a_mode), w_spec(jo)]
    args = [a, w]
    if epi == "swiglu":
        in_specs.append(w_spec(col0_b // tn))
        args.append(w)
    if bias is not None:
        in_specs.append(spec((1, tn), lambda i, j, k: (0, j + jo)))
        args.append(bias.reshape(1, -1))
    if epi == "residual":
        nb, rb, _ = gate.shape
        tiles_per_nb = (m // nb) // tm
        assert (m // nb) % tm == 0
        assert tiles_per_nb >= 1 and (rb == 1 or rb == tm)
        in_specs += [spec((tm, tn), lambda i, j, k: (i, j)),
                     spec((1, rb, tn), lambda i, j, k: (i // tiles_per_nb, 0, j))]
        args += [res, gate]
    if epi == "headnorm":
        in_specs.append(spec((1, HEAD_DIM), lambda i, j, k: (0, 0)))
        args.append(hn_w.reshape(1, HEAD_DIM))
    ni, nj = m // tm, ncols // tn
    out_specs = [spec((tm, tn), lambda i, j, k: (i, j))]
    out_shape = [jax.ShapeDtypeStruct((m, ncols), out_dtype)]
    scratch = [pltpu.VMEM((tm, tn), F32)] if nk > 1 else []
    if small is not None:
        a2, res2, gate2 = small
        m2 = a2.shape[0]
        in_specs.append(spec((m2, tk), lambda i, j, k: (0, k)))
        args.append(a2)
        if epi == "residual":
            in_specs += [spec((m2, tn), lambda i, j, k: (0, j)),
                         spec((1, m2, tn), lambda i, j, k: (0, 0, j))]
            args += [res2, gate2]
        out_specs.append(spec((m2, tn), lambda i, j, k: (0, jnp.where(i == ni - 1, j, nj))))
        out_shape.append(jax.ShapeDtypeStruct((m2, ncols + tn), F32))
        if nk > 1:
            scratch.append(pltpu.VMEM((m2, tn), F32))
    kern = functools.partial(_mm_kernel, nk=nk, ni=ni, a_silu=a_silu, has_bias=bias is not None,
                             epi=epi, n_norm_tiles=n_norm_tiles, post_scale=post_scale,
                             has_small=small is not None)
    outs = pl.pallas_call(
        kern,
        grid=(ni, nj, nk),
        in_specs=in_specs,
        out_specs=out_specs,
        out_shape=out_shape,
        scratch_shapes=scratch,
        compiler_params=_cparams(("arbitrary",) * 3 if small is not None
                                 else ("parallel", "parallel", "arbitrary")),
        name=name,
    )(*args)
    return (outs[0], outs[1][:, :ncols]) if small is not None else outs[0]


def _gate_kernel(wg_ref, u_ref, o_ref):
    o_ref[...] = _sigmoid(_dot_nt(wg_ref[...].astype(BF16), u_ref[...].astype(BF16)))


def _branch_gates_t(u, wg_t):
    m, kdim = u.shape
    r = wg_t.shape[0]
    tm = _pick(m, (1024, 512, 256, 128))
    return pl.pallas_call(
        _gate_kernel,
        grid=(m // tm,),
        in_specs=[pl.BlockSpec((r, kdim), lambda i: (0, 0)),
                  pl.BlockSpec((tm, kdim), lambda i: (i, 0))],
        out_specs=pl.BlockSpec((r, tm), lambda i: (0, i)),
        out_shape=jax.ShapeDtypeStruct((r, m), F32),
        compiler_params=_cparams(("parallel",)),
        name="branch_gates",
    )(wg_t, u)


def _lru_coeffs(xb, gw_r, gw_i, gb, lam):
    xb16 = xb.astype(BF16)
    r = _sigmoid(_dot(xb16, gw_r) + gb[0:1, :])
    gi = _sigmoid(_dot(xb16, gw_i) + gb[1:2, :])
    z = -lam
    softplus = jnp.maximum(z, 0.0) + jnp.log1p(jnp.exp(-jnp.abs(z)))
    log_a = -LRU_C * r * softplus
    a = jnp.exp(log_a)
    return a, jnp.sqrt(-jnp.tanh(log_a) * (1.0 + a * a)) * gi * xb


def _lru_step_kernel(x_ref, gy_ref, cprev_ref, hprev_ref, cw_ref, cb_ref, gw_ref, gb_ref, lam_ref,
                     hg_ref, nconv_ref, h_ref, *, bw, nbp):
    hist = CONV_W - 1
    x = x_ref[...]
    conv = cprev_ref[0] * cw_ref[0:1, :]
    for k in range(1, hist):
        conv = conv + cprev_ref[k] * cw_ref[k:k + 1, :]
    xc = cb_ref[...] + (conv + x * cw_ref[hist:hist + 1, :])
    for blk in range(nbp):
        cols = slice(blk * bw, (blk + 1) * bw)
        a, b = _lru_coeffs(xc[:, cols], gw_ref[0, blk].astype(BF16), gw_ref[1, blk].astype(BF16),
                           gb_ref[:, cols], lam_ref[:, cols])
        h = a * hprev_ref[:, cols] + b
        h_ref[:, cols] = h
        hg_ref[:, cols] = (h * gy_ref[:, cols].astype(F32)).astype(hg_ref.dtype)
    for k in range(hist - 1):
        nconv_ref[k] = cprev_ref[k + 1]
    nconv_ref[hist - 1] = x


def _lru_step(x, gy, conv_prev, h_prev, conv_w, conv_b, gate_w, gate_b, lam):
    b, w = x.shape
    nblk, bw = gate_w.shape[1], gate_w.shape[2]
    nbp = _pick(nblk, (4, 2, 1))
    cw = nbp * bw
    hist = CONV_W - 1
    rows = pl.BlockSpec((b, cw), lambda c: (0, c))
    hrows = pl.BlockSpec((hist, b, cw), lambda c: (0, 0, c))
    chan = lambda r: pl.BlockSpec((r, cw), lambda c: (0, c))
    return pl.pallas_call(
        functools.partial(_lru_step_kernel, bw=bw, nbp=nbp),
        grid=(w // cw,),
        in_specs=[rows, rows, hrows, rows, chan(CONV_W), chan(1),
                  pl.BlockSpec((2, nbp, bw, bw), lambda c: (0, c, 0, 0)), chan(2), chan(1)],
        out_specs=[rows, hrows, rows],
        out_shape=[jax.ShapeDtypeStruct((b, w), gy.dtype), jax.ShapeDtypeStruct((hist, b, w), F32),
                   jax.ShapeDtypeStruct((b, w), F32)],
        compiler_params=_cparams(("parallel",)),
        name="rglru_step",
    )(x, gy, conv_prev, h_prev, conv_w, conv_b.reshape(1, w), gate_w, gate_b, lam.reshape(1, w))


def _lru_kernel(x_ref, gy_ref, cprev_ref, hprev_ref, cw_ref, cb_ref, gw_ref, gb_ref, lam_ref,
                hg_ref, nconv_ref, hlast_ref,
                xbuf, gw_scr, a_scr, b_scr, hs_scr, h_scr, *, tt, nt, bw, nbp):
    t = pl.program_id(2)
    hist = CONV_W - 1
    base = 8

    @pl.when(t == 0)
    def _():
        xbuf[base - hist:base, :] = cprev_ref[0]
        h_scr[...] = hprev_ref[0]
        gw_scr[...] = gw_ref[...].astype(BF16)

    @pl.when(t > 0)
    def _():
        xbuf[base - hist:base, :] = xbuf[base + tt - hist:base + tt, :]

    xbuf[base:base + tt, :] = x_ref[0]
    conv = xbuf[base - hist:base - hist + tt, :] * cw_ref[0:1, :]
    for k in range(1, CONV_W):
        conv = conv + xbuf[base - hist + k:base - hist + k + tt, :] * cw_ref[k:k + 1, :]
    xc = cb_ref[...] + conv

    for blk in range(nbp):
        cols = slice(blk * bw, (blk + 1) * bw)
        a_scr[:, cols], b_scr[:, cols] = _lru_coeffs(
            xc[:, cols], gw_scr[0, blk], gw_scr[1, blk], gb_ref[:, cols], lam_ref[:, cols])

    def step(i, h):
        h = a_scr[pl.ds(i, 1), :] * h + b_scr[pl.ds(i, 1), :]
        hs_scr[pl.ds(i, 1), :] = h
        return h

    h_scr[...] = lax.fori_loop(0, tt, step, h_scr[...], unroll=min(tt, 8))
    hg_ref[0] = (hs_scr[...] * gy_ref[0].astype(F32)).astype(hg_ref.dtype)

    @pl.when(t == nt - 1)
    def _():
        nconv_ref[0] = xbuf[base + tt - hist:base + tt, :]
        hlast_ref[0] = h_scr[...]


def _lru(x, gy, conv_prev, h_prev, conv_w, conv_b, gate_w, gate_b, lam):
    b, t, w = x.shape
    nblk, bw = gate_w.shape[1], gate_w.shape[2]
    nbp = _pick(nblk, (4, 2, 1))
    cw = nbp * bw
    tt = _pick(t, (256, 128, 64, 32, 16, 8))
    nt = t // tt
    hist = CONV_W - 1
    kern = functools.partial(_lru_kernel, tt=tt, nt=nt, bw=bw, nbp=nbp)
    row = lambda bi, c, ti: (bi, ti, c)
    fixed = lambda bi, c, ti: (bi, 0, c)
    chan = lambda bi, c, ti: (0, c)
    hg, nconv, hlast = pl.pallas_call(
        kern,
        grid=(b, w // cw, nt),
        in_specs=[pl.BlockSpec((1, tt, cw), row), pl.BlockSpec((1, tt, cw), row),
                  pl.BlockSpec((1, hist, cw), fixed), pl.BlockSpec((1, 1, cw), fixed),
                  pl.BlockSpec((CONV_W, cw), chan), pl.BlockSpec((1, cw), chan),
                  pl.BlockSpec((2, nbp, bw, bw), lambda bi, c, ti: (0, c, 0, 0)),
                  pl.BlockSpec((2, cw), chan), pl.BlockSpec((1, cw), chan)],
        out_specs=[pl.BlockSpec((1, tt, cw), row), pl.BlockSpec((1, hist, cw), fixed),
                   pl.BlockSpec((1, 1, cw), fixed)],
        out_shape=[jax.ShapeDtypeStruct((b, t, w), BF16),
                   jax.ShapeDtypeStruct((b, hist, w), F32),
                   jax.ShapeDtypeStruct((b, 1, w), F32)],
        scratch_shapes=[pltpu.VMEM((8 + tt, cw), F32), pltpu.VMEM((2, nbp, bw, bw), BF16),
                        pltpu.VMEM((tt, cw), F32), pltpu.VMEM((tt, cw), F32),
                        pltpu.VMEM((tt, cw), F32), pltpu.VMEM((1, cw), F32)],
        compiler_params=_cparams(("parallel", "parallel", "arbitrary")),
        name="rglru",
    )(x, gy, conv_prev, h_prev.reshape(b, 1, w), conv_w, conv_b.reshape(1, w), gate_w, gate_b,
      lam.reshape(1, w))
    return hg, nconv, hlast.reshape(b, w)


def _cmp_w_pq(w1_ref, kv):
    half = CMP_STRIDE * HEAD_DIM
    return jnp.concatenate([w1_ref[kv, :half, :], w1_ref[kv, half:, :]], axis=1).astype(BF16)


def _cmp_pq_kernel(x_ref, w1_ref, o_ref, slab):
    nc = slab.shape[0] // CMP_STRIDE
    for kv in range(2):
        rows = []
        for g in range(N_KV):
            off = (kv * N_KV + g) * HEAD_DIM
            slab[...] = x_ref[0, :, off:off + HEAD_DIM]
            rows.append(jnp.concatenate(
                [slab[pl.ds(r, nc, stride=CMP_STRIDE), :].astype(BF16) for r in range(CMP_STRIDE)], axis=1))
        pq = _dot(jnp.concatenate(rows, axis=0), _cmp_w_pq(w1_ref, kv))
        for g in range(N_KV):
            c0 = (kv * N_KV + g) * 2 * HEAD_DIM
            o_ref[0, :, c0:c0 + 2 * HEAD_DIM] = pq[g * nc:(g + 1) * nc, :]


def _cmp_pq_paged_kernel(pt_ref, *refs, n_src):
    src, (w1_ref, o_ref, p_scr, q_scr) = refs[:n_src], refs[n_src:]
    slots = 2 * N_KV
    cpp = PAGE_SIZE // CMP_STRIDE
    chunk_rows = CMP_STRIDE * slots
    lhs = jnp.concatenate(
        [jnp.concatenate([s[c * chunk_rows + r * slots:c * chunk_rows + (r + 1) * slots, :]
                          for r in range(CMP_STRIDE)], axis=1)
         for s in src for c in range(cpp)], axis=0).astype(BF16)
    both = _dot(lhs, jnp.concatenate([_cmp_w_pq(w1_ref, 0), _cmp_w_pq(w1_ref, 1)], axis=1))
    is_v = (lax.broadcasted_iota(jnp.int32, (both.shape[0], 1), 0) & (slots - 1)) >= N_KV
    pq = jnp.where(is_v, both[:, 2 * HEAD_DIM:], both[:, :2 * HEAD_DIM])
    p_scr[...] = pq[:, :HEAD_DIM]
    q_scr[...] = pq[:, HEAD_DIM:]
    nc = n_src * cpp
    for slot in range(slots):
        c0 = slot * 2 * HEAD_DIM
        o_ref[0, :, c0:c0 + HEAD_DIM] = p_scr[pl.ds(slot, nc, stride=slots), :]
        o_ref[0, :, c0 + HEAD_DIM:c0 + 2 * HEAD_DIM] = q_scr[pl.ds(slot, nc, stride=slots), :]


def _cmp_pq(rows, w1, *, page_table=None, pages_per_step=16):
    out_cols = 2 * N_KV * 2 * HEAD_DIM
    if page_table is None:
        nb, t_len, _ = rows.shape
        nch = t_len // CMP_STRIDE
        tc = _pick(nch, (128, 64, 32, 16, 8))
        return pl.pallas_call(
            _cmp_pq_kernel,
            grid=(nb, nch // tc),
            in_specs=[pl.BlockSpec((1, tc * CMP_STRIDE, KV_ROW), lambda b, c: (b, c, 0)),
                      pl.BlockSpec(w1.shape, lambda b, c: (0, 0, 0))],
            out_specs=pl.BlockSpec((1, tc, out_cols), lambda b, c: (b, c, 0)),
            out_shape=jax.ShapeDtypeStruct((nb, nch, out_cols), F32),
            scratch_shapes=[pltpu.VMEM((tc * CMP_STRIDE, HEAD_DIM), F32)],
            compiler_params=_cparams(("parallel", "parallel")),
            name="cmp_pq",
        )(rows, w1)
    nb, n_pages = page_table.shape
    page_rows = PAGE_SIZE * 2 * N_KV
    cpp = PAGE_SIZE // CMP_STRIDE
    pps = _pick(n_pages, (pages_per_step, 8, 4, 2, 1))
    src_specs = [pl.BlockSpec((page_rows, HEAD_DIM),
                              functools.partial(lambda b, s, pt, p: (pt[b, s * pps + p], 0), p=p))
                 for p in range(pps)]
    grid_spec = pltpu.PrefetchScalarGridSpec(
        num_scalar_prefetch=1,
        grid=(nb, n_pages // pps),
        in_specs=src_specs + [pl.BlockSpec(w1.shape, lambda b, s, pt: (0, 0, 0))],
        out_specs=pl.BlockSpec((1, pps * cpp, out_cols), lambda b, s, pt: (b, s, 0)),
        scratch_shapes=[pltpu.VMEM((pps * cpp * 2 * N_KV, HEAD_DIM), F32)] * 2,
    )
    return pl.pallas_call(
        functools.partial(_cmp_pq_paged_kernel, n_src=pps),
        grid_spec=grid_spec,
        out_shape=jax.ShapeDtypeStruct((nb, n_pages * cpp, out_cols), F32),
        compiler_params=_cparams(("parallel", "parallel")),
        name="cmp_pq_paged",
    )(page_table, *([rows] * pps), w1)


def _cmp_finish_kernel(pq_ref, pe_ref, w1_ref, b1_ref, w2_ref, b2_ref, kn_ref, kc_ref, vct_ref, *, nch):
    is_block = lax.broadcasted_iota(jnp.int32, (nch, 1), 0) < nch - 1
    for kv in range(2):
        pe8 = jnp.broadcast_to(pe_ref[kv], (8, pe_ref.shape[2])).astype(BF16)
        const = _dot(pe8, w1_ref[kv].astype(BF16))[0:1, :] + b1_ref[kv]
        w2 = w2_ref[kv].astype(BF16)
        for g in range(N_KV):
            c0 = (kv * N_KV + g) * 2 * HEAD_DIM
            nxt = jnp.concatenate([pq_ref[0, 1:nch, c0 + HEAD_DIM:c0 + 2 * HEAD_DIM],
                                   jnp.zeros((1, HEAD_DIM), F32)], axis=0)
            hid = pq_ref[0, :, c0:c0 + HEAD_DIM] + nxt + const
            out = _dot(jax.nn.gelu(hid).astype(BF16), w2) + b2_ref[kv]
            if kv == 0:
                out = out * lax.rsqrt(jnp.mean(out * out, axis=-1, keepdims=True) + EPS) * kn_ref[...]
            out = jnp.where(is_block, out, 0.0)
            if kv == 0:
                kc_ref[0, g] = out.astype(kc_ref.dtype)
            else:
                vct_ref[0, g] = out.T.astype(vct_ref.dtype)


def _cmp_finish(pq, pe, w1, b1, w2, b2, k_norm):
    nb, nch, cols = pq.shape
    full = lambda shape: pl.BlockSpec(shape, lambda b: (0,) * len(shape))
    return pl.pallas_call(
        functools.partial(_cmp_finish_kernel, nch=nch),
        grid=(nb,),
        in_specs=[pl.BlockSpec((1, nch, cols), lambda b: (b, 0, 0)),
                  full((2, 1, CMP_BLOCK * HEAD_DIM)), full(w1.shape), full((2, 1, HEAD_DIM)),
                  full(w2.shape), full((2, 1, HEAD_DIM)), full((1, HEAD_DIM))],
        out_specs=[pl.BlockSpec((1, N_KV, nch, HEAD_DIM), lambda b: (b, 0, 0, 0)),
                   pl.BlockSpec((1, N_KV, HEAD_DIM, nch), lambda b: (b, 0, 0, 0))],
        out_shape=[jax.ShapeDtypeStruct((nb, N_KV, nch, HEAD_DIM), BF16),
                   jax.ShapeDtypeStruct((nb, N_KV, HEAD_DIM, nch), BF16)],
        compiler_params=_cparams(("parallel",)),
        name="cmp_finish",
    )(pq, pe.reshape(2, 1, -1), w1, b1.reshape(2, 1, -1), w2, b2.reshape(2, 1, -1),
      k_norm.reshape(1, HEAD_DIM))


def _per_head(fn, x, hpg):
    tq = x.shape[1] // hpg
    return jnp.concatenate([fn(x[:, h * tq:(h + 1) * tq]) for h in range(hpg)], axis=1)


def _flash_update(s, m_scr, acc_scr, v_aug):
    m_old = m_scr[...]
    m_new = jnp.maximum(m_old, jnp.max(s, axis=0, keepdims=True))
    p = jnp.exp((s - m_new).astype(BF16))
    acc_scr[...] = jnp.exp(m_old - m_new) * acc_scr[...] + _dot(v_aug, p)
    m_scr[...] = m_new


def _nsa_prompt_kernel(q_ref, gt_ref, kc_ref, vct_ref, ks_ref, vs_ref, kw_ref, vw_ref, o_ref,
                       ks_scr, vst_scr, kw_scr, vwt_scr, qt_scr, score_scr, sel_scr,
                       bias_scr, wbias_scr, m_scr, acc_scr, o_scr, *, tq, t_len, hpg, n_cmp_pad):
    g = pl.program_id(1)
    qi = pl.program_id(2)
    n_chunks = t_len // KEY_CHUNK
    n_slc = t_len // SLC_BLOCK
    win_chunks = WINDOW // KEY_CHUNK
    n_wb = tq // KEY_CHUNK + win_chunks
    nl = hpg * tq

    @pl.when(qi == 0)
    def _():
        for c in range(n_chunks):
            rows = slice(c * KEY_CHUNK, (c + 1) * KEY_CHUNK)
            ks_scr[rows, :] = ks_ref[0, rows, :].astype(BF16)
            kw_scr[rows, :] = kw_ref[0, rows, :].astype(BF16)
            vst_scr[0:HEAD_DIM, rows] = vs_ref[0, rows, :].T.astype(BF16)
            vwt_scr[0:HEAD_DIM, rows] = vw_ref[0, rows, :].T.astype(BF16)
        ones = jnp.ones((ONES_ROWS, t_len), BF16)
        vst_scr[HEAD_DIM:, :] = ones
        vwt_scr[HEAD_DIM:, :] = ones
        kl = lax.broadcasted_iota(jnp.int32, (KEY_CHUNK, tq), 0)
        for c in range(n_wb):
            dlt = (win_chunks - c) * KEY_CHUNK + lax.broadcasted_iota(jnp.int32, (KEY_CHUNK, tq), 1) - kl
            wbias_scr[c] = jnp.where((dlt >= 0) & (dlt < WINDOW), 0.0, NEG)

    for h in range(hpg):
        qt_scr[:, h * tq:(h + 1) * tq] = (
            q_ref[:, h * HEAD_DIM:(h + 1) * HEAD_DIM].astype(F32).T.astype(BF16))
    qpos = qi * tq + lax.broadcasted_iota(jnp.int32, (1, tq), 1)

    def gate_row(branch):
        return jnp.concatenate(
            [gt_ref[pl.ds((g * hpg + h) * N_BRANCH + branch, 1), :] for h in range(hpg)], axis=1)

    cmp_idx = lax.broadcasted_iota(jnp.int32, (n_cmp_pad, tq), 0)
    cmp_mask = cmp_idx * CMP_STRIDE + (CMP_BLOCK - 1) <= qpos
    s = _per_head(lambda x: jnp.where(cmp_mask, x, NEG), _dot(kc_ref[0, 0], qt_scr[...]), hpg)
    e = _per_head(lambda x: jnp.where(cmp_mask, x, 0.0),
                  jnp.exp(s - jnp.max(s, axis=0, keepdims=True)), hpg)
    p = e / jnp.maximum(jnp.sum(e, axis=0, keepdims=True), TINY)
    o_scr[...] = gate_row(0) * _dot(vct_ref[0, 0], p.astype(BF16))
    p_grp = p[:, 0:tq]
    for h in range(1, hpg):
        p_grp = p_grp + p[:, h * tq:(h + 1) * tq]

    ratio = SLC_BLOCK // CMP_STRIDE
    jj = lax.broadcasted_iota(jnp.int32, (n_slc, n_cmp_pad), 0)
    mm = lax.broadcasted_iota(jnp.int32, (n_slc, n_cmp_pad), 1)
    pool = ((mm >= ratio * jj - 1) & (mm <= ratio * jj + ratio - 1)).astype(BF16)
    p_slc = sum(_dot(pool, piece) for piece in _split3(p_grp))
    blk = lax.broadcasted_iota(jnp.int32, (n_slc, tq), 0)
    cur = lax.shift_right_arithmetic(qpos, SLC_SHIFT)
    forced = (blk == 0) | (blk == cur) | (blk == cur - 1)
    valid = blk * SLC_BLOCK <= qpos
    score = jnp.where(forced, FORCE, jnp.where(valid, p_slc, -1.0))
    score_scr[...] = score

    def rank_step(j, rank):
        other = score_scr[pl.ds(j, 1), :]
        beats = (other > score) | ((other == score) & (j < blk))
        return rank + beats.astype(jnp.int32)

    rank = lax.fori_loop(0, n_slc, rank_step, jnp.zeros((n_slc, tq), jnp.int32))
    sel_scr[...] = (rank < min(N_SEL, n_slc)).astype(F32)

    def slc_bias(c, _):
        for r in range(SLC_PER_CHUNK):
            sel_row = sel_scr[pl.ds(c * SLC_PER_CHUNK + r, 1), :]
            kpos = c * KEY_CHUNK + r * SLC_BLOCK + lax.broadcasted_iota(jnp.int32, (SLC_BLOCK, tq), 0)
            ok = (sel_row > 0.5) & (kpos <= qpos)
            bias_scr[pl.ds(pl.multiple_of(c * KEY_CHUNK + r * SLC_BLOCK, SLC_BLOCK), SLC_BLOCK), :] = (
                jnp.where(ok, 0.0, NEG))
        return 0

    n_live = (qi * tq + tq + KEY_CHUNK - 1) // KEY_CHUNK
    n_pairs = (n_live + 1) // 2
    lax.fori_loop(0, 2 * n_pairs, slc_bias, 0)

    def scores(k_scr, chunk, bias):
        keys = pl.ds(pl.multiple_of(chunk * KEY_CHUNK, KEY_CHUNK), KEY_CHUNK)
        return _per_head(lambda x: x + bias(keys), _dot(k_scr[keys, :], qt_scr[...]), hpg), keys

    def reset():
        m_scr[...] = jnp.full((1, nl), NEG, F32)
        acc_scr[...] = jnp.zeros((HEAD_DIM + ONES_ROWS, nl), F32)

    def add_branch(branch):
        inv = 1.0 / jnp.maximum(acc_scr[HEAD_DIM:HEAD_DIM + 1, :], TINY)
        o_scr[...] += (gate_row(branch) * inv) * acc_scr[0:HEAD_DIM, :]

    reset()

    def slc_pair(i, _):
        tiles = [scores(ks_scr, 2 * i + u, lambda keys: bias_scr[keys, :]) for u in range(2)]
        for s, keys in tiles:
            _flash_update(s, m_scr, acc_scr, vst_scr[:, keys])
        return 0

    lax.fori_loop(0, n_pairs, slc_pair, 0)
    add_branch(1)

    reset()
    first_key_chunk = qi * (tq // KEY_CHUNK) - win_chunks
    tiles = []
    for c in reversed(range(n_wb)):
        chunk = first_key_chunk + c
        band = jnp.where(chunk >= 0, wbias_scr[c], NEG)
        tiles.append(scores(kw_scr, jnp.maximum(chunk, 0), lambda keys: band))
    for s, keys in tiles:
        _flash_update(s, m_scr, acc_scr, vwt_scr[:, keys])
    add_branch(2)

    for h in range(hpg):
        o_ref[:, h * HEAD_DIM:(h + 1) * HEAD_DIM] = o_scr[:, h * tq:(h + 1) * tq].T.astype(o_ref.dtype)


def _nsa_prompt(q, gates_t, kc, vct, slc_rows, win_rows, bsz, t_len):
    m, qcols = q.shape
    hpg = qcols // (N_KV * HEAD_DIM)
    gw = hpg * HEAD_DIM
    tq = KEY_CHUNK
    assert t_len % (2 * KEY_CHUNK) == 0 and WINDOW % KEY_CHUNK == 0 and tq % KEY_CHUNK == 0
    nq = t_len // tq
    n_cmp_pad = kc.shape[2]
    n_slc = t_len // SLC_BLOCK
    n_wb = tq // KEY_CHUNK + WINDOW // KEY_CHUNK
    kern = functools.partial(_nsa_prompt_kernel, tq=tq, t_len=t_len, hpg=hpg, n_cmp_pad=n_cmp_pad)
    kv_k = lambda b, g, i: (b, 0, g)
    kv_v = lambda b, g, i: (b, 0, N_KV + g)
    return pl.pallas_call(
        kern,
        grid=(bsz, N_KV, nq),
        in_specs=[pl.BlockSpec((tq, gw), lambda b, g, i: (b * nq + i, g)),
                  pl.BlockSpec((gates_t.shape[0], tq), lambda b, g, i: (0, b * nq + i)),
                  pl.BlockSpec((1, 1, n_cmp_pad, HEAD_DIM), lambda b, g, i: (b, g, 0, 0)),
                  pl.BlockSpec((1, 1, HEAD_DIM, n_cmp_pad), lambda b, g, i: (b, g, 0, 0)),
                  pl.BlockSpec((1, t_len, HEAD_DIM), kv_k), pl.BlockSpec((1, t_len, HEAD_DIM), kv_v),
                  pl.BlockSpec((1, t_len, HEAD_DIM), kv_k), pl.BlockSpec((1, t_len, HEAD_DIM), kv_v)],
        out_specs=pl.BlockSpec((tq, gw), lambda b, g, i: (b * nq + i, g)),
        out_shape=jax.ShapeDtypeStruct((m, qcols), BF16),
        scratch_shapes=[pltpu.VMEM((t_len, HEAD_DIM), BF16), pltpu.VMEM((HEAD_DIM + ONES_ROWS, t_len), BF16),
                        pltpu.VMEM((t_len, HEAD_DIM), BF16), pltpu.VMEM((HEAD_DIM + ONES_ROWS, t_len), BF16),
                        pltpu.VMEM((HEAD_DIM, hpg * tq), BF16),
                        pltpu.VMEM((n_slc, tq), F32), pltpu.VMEM((n_slc, tq), F32),
                        pltpu.VMEM((t_len, tq), F32), pltpu.VMEM((n_wb, KEY_CHUNK, tq), F32),
                        pltpu.VMEM((1, hpg * tq), F32),
                        pltpu.VMEM((HEAD_DIM + ONES_ROWS, hpg * tq), F32),
                        pltpu.VMEM((HEAD_DIM, hpg * tq), F32)],
        compiler_params=_cparams(("parallel", "parallel", "arbitrary")),
        name="nsa_prompt",
    )(q, gates_t, kc, vct, slc_rows, slc_rows, win_rows, win_rows)


def _nsa_sample_select_kernel(q_ref, kc_ref, vct_ref, ocmp_ref, idx_ref, *, qpos, n_slc, nsp):
    q = q_ref[0, 0].astype(BF16)
    ncp = kc_ref.shape[2]
    s = _dot_nt(q, kc_ref[0, 0])
    m_idx = lax.broadcasted_iota(jnp.int32, (1, ncp), 1)
    mask = m_idx * CMP_STRIDE + (CMP_BLOCK - 1) <= qpos
    s = jnp.where(mask, s, NEG)
    e = jnp.where(mask, jnp.exp(s - jnp.max(s, axis=-1, keepdims=True)), 0.0)
    p = e / jnp.maximum(jnp.sum(e, axis=-1, keepdims=True), TINY)
    ocmp_ref[0, 0] = _dot_nt(p.astype(BF16), vct_ref[0, 0])

    ratio = SLC_BLOCK // CMP_STRIDE
    p_grp = jnp.broadcast_to(jnp.sum(p, axis=0, keepdims=True), (8, ncp))
    mm = lax.broadcasted_iota(jnp.int32, (ncp, nsp), 0)
    jj = lax.broadcasted_iota(jnp.int32, (ncp, nsp), 1)
    pool_t = ((mm >= ratio * jj - 1) & (mm <= ratio * jj + ratio - 1)).astype(BF16)
    p_slc = sum(_dot(piece, pool_t) for piece in _split3(p_grp))[0:1, :]
    blk = lax.broadcasted_iota(jnp.int32, (1, nsp), 1)
    cur = qpos // SLC_BLOCK
    forced = (blk == 0) | (blk == cur) | (blk == cur - 1)
    score = jnp.where(forced, FORCE, jnp.where(blk * SLC_BLOCK <= qpos, p_slc, -1.0))
    score = jnp.where(blk < n_slc, score, -2.0)

    ii = lax.broadcasted_iota(jnp.int32, (nsp, nsp), 0)
    jx = lax.broadcasted_iota(jnp.int32, (nsp, nsp), 1)
    mine = jnp.broadcast_to(score, (nsp, nsp))
    other = mine.T
    beats = (other > mine) | ((other == mine) & (ii < jx))
    rank = jnp.sum(beats.astype(F32), axis=0, keepdims=True)
    sel = (rank < float(N_SEL)).astype(F32)
    sel_other = jnp.broadcast_to(sel, (nsp, nsp)).T
    pos = jnp.sum(jnp.where(ii < jx, sel_other, 0.0), axis=0, keepdims=True)
    kk = lax.broadcasted_iota(jnp.int32, (N_SEL, nsp), 0).astype(F32)
    jrow = lax.broadcasted_iota(jnp.int32, (N_SEL, nsp), 1).astype(F32)
    hit = (sel > 0.5) & (pos == kk)
    idx = jnp.sum(jnp.where(hit, jrow, 0.0), axis=1, keepdims=True)
    idx_ref[0, 0] = jnp.broadcast_to(idx, (N_SEL, HEAD_DIM)).astype(jnp.int32)


def _nsa_sample_select(q, kc, vct, qpos, n_slc):
    dbs, _, hpg, _ = q.shape
    ncp = kc.shape[2]
    nsp = -(-n_slc // HEAD_DIM) * HEAD_DIM
    assert n_slc >= N_SEL
    blk4 = lambda s2, s3: pl.BlockSpec((1, 1, s2, s3), lambda b, g: (b, g, 0, 0))
    ocmp, idx = pl.pallas_call(
        functools.partial(_nsa_sample_select_kernel, qpos=qpos, n_slc=n_slc, nsp=nsp),
        grid=(dbs, N_KV),
        in_specs=[blk4(hpg, HEAD_DIM), blk4(ncp, HEAD_DIM), blk4(HEAD_DIM, ncp)],
        out_specs=[blk4(hpg, HEAD_DIM), blk4(N_SEL, HEAD_DIM)],
        out_shape=[jax.ShapeDtypeStruct((dbs, N_KV, hpg, HEAD_DIM), F32),
                   jax.ShapeDtypeStruct((dbs, N_KV, N_SEL, HEAD_DIM), jnp.int32)],
        compiler_params=_cparams(("parallel", "parallel")),
        name="nsa_sample_select",
    )(q, kc, vct)
    return ocmp, idx[..., 0]


def _nsa_sample_attend_kernel(pt_ref, idx_ref, q_ref, gate_ref, ocmp_ref, snew_ref, wnew_ref,
                              win_ref, *rest, qpos, n_cached, win_buf):
    blocks, o_ref = rest[:N_SEL], rest[N_SEL]
    b, g = pl.program_id(0), pl.program_id(1)
    slots = 2 * N_KV

    def group_rows(ref, n, col):
        return ref[pl.ds(col * N_KV + g, n, stride=slots), :]

    q = q_ref[0, 0].astype(BF16)
    qf = q.astype(F32)

    def attend(s, mask, v, k_new, v_new):
        s_self = jnp.sum(qf * k_new.astype(BF16).astype(F32), axis=-1, keepdims=True)
        s = jnp.where(mask, s, NEG)
        m = jnp.maximum(jnp.max(s, axis=-1, keepdims=True), s_self)
        e = jnp.where(mask, jnp.exp(s - m), 0.0)
        e_self = jnp.exp(s_self - m)
        l = jnp.sum(e, axis=-1, keepdims=True) + e_self
        acc = _dot(e.astype(BF16), v) + e_self.astype(BF16).astype(F32) * v_new.astype(BF16).astype(F32)
        return acc / jnp.maximum(l, TINY)

    n_keys = N_SEL * SLC_BLOCK
    k_all = jnp.concatenate([group_rows(blk, SLC_BLOCK, 0) for blk in blocks], axis=0).astype(BF16)
    v_all = jnp.concatenate([group_rows(blk, SLC_BLOCK, 1) for blk in blocks], axis=0).astype(BF16)
    lane = lax.broadcasted_iota(jnp.int32, (1, n_keys), 1)
    slot = lax.shift_right_arithmetic(lane, SLC_SHIFT)
    blk_id = jnp.zeros((1, n_keys), jnp.int32)
    for k in range(N_SEL):
        blk_id = jnp.where(slot == k, idx_ref[(b * N_KV + g) * N_SEL + k], blk_id)
    kpos = blk_id * SLC_BLOCK + (lane & (SLC_BLOCK - 1))
    slc_mask = (blk_id < n_cached) & (kpos <= qpos)
    o_slc = attend(_dot_nt(q, k_all), slc_mask, v_all,
                   snew_ref[0, pl.ds(g, 1), :], snew_ref[0, pl.ds(N_KV + g, 1), :])

    wi = lax.broadcasted_iota(jnp.int32, (1, win_buf), 1)
    dlt = win_buf - wi
    win_mask = (dlt >= 0) & (dlt < WINDOW) & (qpos - dlt >= 0)
    o_win = attend(_dot_nt(q, group_rows(win_ref, win_buf, 0).astype(BF16)), win_mask,
                   group_rows(win_ref, win_buf, 1).astype(BF16),
                   wnew_ref[0, pl.ds(g, 1), :], wnew_ref[0, pl.ds(N_KV + g, 1), :])

    gate = gate_ref[0, 0]
    o_ref[0, 0] = gate[:, 0:1] * ocmp_ref[0, 0] + gate[:, 1:2] * o_slc + gate[:, 2:3] * o_win


def _nsa_sample_attend(q, gates, ocmp, idx, page_table, slc_cache, slc_new, win_cache, win_new, qpos):
    dbs, _, hpg, _ = q.shape
    slots = 2 * N_KV
    win_buf = win_cache.shape[0] // (dbs * slots)
    n_cached = page_table.shape[1] * PAGE_SIZE // SLC_BLOCK
    bpp = PAGE_SIZE // SLC_BLOCK

    def blk_map(b, g, pt, ix, *, k):
        blk = jnp.minimum(ix[(b * N_KV + g) * N_SEL + k], n_cached - 1)
        return pt[b, blk // bpp] * bpp + blk % bpp, 0

    blk4 = lambda s2, s3: pl.BlockSpec((1, 1, s2, s3), lambda b, g, pt, ix: (b, g, 0, 0))
    new_spec = pl.BlockSpec((1, slots, HEAD_DIM), lambda b, g, pt, ix: (b, 0, 0))
    win_spec = pl.BlockSpec((win_buf * slots, HEAD_DIM), lambda b, g, pt, ix: (b, 0))
    gathered = [pl.BlockSpec((SLC_BLOCK * slots, HEAD_DIM), functools.partial(blk_map, k=k))
                for k in range(N_SEL)]
    grid_spec = pltpu.PrefetchScalarGridSpec(
        num_scalar_prefetch=2,
        grid=(dbs, N_KV),
        in_specs=[blk4(hpg, HEAD_DIM), blk4(hpg, N_BRANCH), blk4(hpg, HEAD_DIM), new_spec, new_spec,
                  win_spec] + gathered,
        out_specs=blk4(hpg, HEAD_DIM),
    )
    return pl.pallas_call(
        functools.partial(_nsa_sample_attend_kernel, qpos=qpos, n_cached=n_cached, win_buf=win_buf),
        grid_spec=grid_spec,
        out_shape=jax.ShapeDtypeStruct((dbs, N_KV, hpg, HEAD_DIM), F32),
        compiler_params=_cparams(("parallel", "parallel")),
        name="nsa_sample_attend",
    )(page_table, idx.reshape(-1), q, gates, ocmp,
      slc_new.reshape(dbs, slots, HEAD_DIM), win_new.reshape(dbs, slots, HEAD_DIM),
      win_cache, *([slc_cache] * N_SEL))


def kernel(x_prompt, x_sample, c_prompt, c_sample, state_lru_h, state_conv, cache_cmp_kv, cache_slc_kv, cache_win_kv, page_table, ada_w, ada_b, norm1_w, norm2_w, lru_w_in, lru_b_in, lru_conv_w, lru_conv_b, lru_gate_w, lru_gate_b, lru_lambda, lru_w_out, lru_b_out, ffn_w13, ffn_w2, kv_ada_w, kv_ada_b, kv_norm_w, w_kv, k_norm_w, cmp_pe, cmp_w1, cmp_b1, cmp_w2, cmp_b2, w_qg, q_norm_w, w_o):
    bsz, t_len, d = x_prompt.shape
    dbs, dec_seq, _ = x_sample.shape
    depth = ada_w.shape[0]
    n_a = lru_w_in.shape[0]
    assert dec_seq == 1 and depth == 2 and n_a == 1 and w_qg.shape[0] == 1
    lw = lru_w_in.shape[2] // 2
    d_ff = ffn_w2.shape[1]
    n_pages = page_table.shape[1]
    past_len = n_pages * PAGE_SIZE
    qcols = w_o.shape[1]
    hpg = qcols // (N_KV * HEAD_DIM)
    hist = CONV_W - 1
    branch_cols = KV_ROW

    n_c = bsz + dbs
    c_all = jnp.pad(jnp.concatenate([c_prompt, c_sample], axis=0), ((0, (-n_c) % 8), (0, 0)))
    mods = [_matmul(c_all, ada_w, layer=l, bias=ada_b[l], a_silu=True, tn=1024, name="ada")
            for l in range(depth)]
    kv_mod = _matmul(c_all, kv_ada_w, bias=kv_ada_b, a_silu=True, tn=1024, name="kv_ada")

    def split_mod(mat, n, prompt):
        parts = [mat[:, i * d:(i + 1) * d] for i in range(n)]
        if prompt:
            return [p[:bsz].reshape(bsz, 1, d) for p in parts]
        return [p[bsz:n_c].reshape(1, dbs, d) for p in parts]

    w13 = ffn_w13
    w2 = ffn_w2.astype(BF16)
    tk2 = _pick(d_ff, (d_ff // 2,)) if d_ff > 4096 else d_ff
    w_in, w_out, wq, wo = lru_w_in, lru_w_out, w_qg, w_o
    wg_t = w_qg[0, :, qcols:].T
    pe = cmp_pe.reshape(2, -1)

    def mm2(a_p, a_s, w, *, res=(None, None), gate=(None, None), **kw):
        return _matmul(a_p, w, res=res[0], gate=gate[0], small=(a_s, res[1], gate[1]), **kw)

    def norm2(x, w, mod, shift_i, scale_i):
        return (_norm_mod(x[0], w, mod[0][shift_i], mod[0][scale_i], BF16),
                _norm_mod(x[1], w, mod[1][shift_i], mod[1][scale_i], F32))

    def ffn(x, l, mod):
        u = norm2(x, norm2_w[l], mod, 3, 4)
        act = mm2(*u, w13, layer=l, col0=0, col0_b=d_ff, ncols=d_ff, epi="swiglu", tn=256,
                  out_dtype=BF16, name="ffn_up")
        return mm2(*act, w2, layer=l, tk=tk2, epi="residual", res=x, gate=(mod[0][5], mod[1][5]),
                   name="ffn_down")

    mod_l = [(split_mod(m, 6, True), split_mod(m, 6, False)) for m in mods]
    mod_kv = (split_mod(kv_mod, 2, True), split_mod(kv_mod, 2, False))
    wide = dict(tm=min(2048, t_len), tn=256, single_buffer_a=True)
    x = (x_prompt.reshape(bsz * t_len, d), x_sample.reshape(dbs, d))

    u = norm2(x, norm1_w[0], mod_l[0], 0, 1)
    gy = mm2(*u, w_in, col0=0, ncols=lw, bias=lru_b_in[0], epi="gelu", out_dtype=BF16, name="lru_in_y",
             **wide)
    xb = mm2(*u, w_in, col0=lw, ncols=lw, bias=lru_b_in[0], name="lru_in_x", **wide)
    lru_args = (lru_conv_w[0], lru_conv_b[0], lru_gate_w[0], lru_gate_b[0], lru_lambda[0])
    hg_p, conv_p, h_p = _lru(xb[0].reshape(bsz, t_len, lw), gy[0].reshape(bsz, t_len, lw),
                             jnp.zeros((bsz, hist, lw), F32), jnp.zeros((bsz, lw), F32), *lru_args)
    hg_s, conv_s, h_s = _lru_step(xb[1], gy[1], jnp.swapaxes(state_conv[0], 0, 1), state_lru_h[0],
                                  *lru_args)
    x = mm2(hg_p.reshape(bsz * t_len, lw), hg_s, w_out, bias=lru_b_out[0], epi="residual", res=x,
            gate=(mod_l[0][0][2], mod_l[0][1][2]), name="lru_out", **wide)
    x = ffn(x, 0, mod_l[0])

    s = norm2(x, kv_norm_w, mod_kv, 0, 1)
    kv_rows = [mm2(*s, w_kv, col0=0, ncols=branch_cols, name="kv_cmp", **wide)]
    for br in range(1, N_BRANCH):
        kv_rows.append(mm2(*s, w_kv, col0=br * branch_cols, ncols=branch_cols, epi="headnorm",
                           hn_w=k_norm_w[br], n_norm_tiles=branch_cols // 2 // wide["tn"],
                           name="kv_norm", **wide))
    (cmp_p, cmp_s), (slc_p, slc_s), (win_p, win_s) = kv_rows

    u = norm2(x, norm1_w[1], mod_l[1], 0, 1)
    q_p, q_s = mm2(*u, wq, col0=0, ncols=qcols, epi="headnorm", hn_w=q_norm_w[0],
                   n_norm_tiles=qcols // wide["tn"], post_scale=HEAD_DIM ** -0.5, out_dtype=BF16,
                   name="nsa_q", **wide)
    gt_p, gt_s = _branch_gates_t(u[0], wg_t), _branch_gates_t(u[1], wg_t)

    pq_p = _cmp_pq(cmp_p.reshape(bsz, t_len, KV_ROW), cmp_w1)
    kc_p, vct_p = _cmp_finish(pq_p, pe, cmp_w1, cmp_b1, cmp_w2, cmp_b2, k_norm_w[0])
    o_p = _nsa_prompt(q_p, gt_p, kc_p, vct_p, slc_p.reshape(bsz, t_len, KV_ROW),
                      win_p.reshape(bsz, t_len, KV_ROW), bsz, t_len)

    pq_s = _cmp_pq(cache_cmp_kv.reshape(-1, HEAD_DIM), cmp_w1, page_table=page_table)
    kc_s, vct_s = _cmp_finish(pq_s, pe, cmp_w1, cmp_b1, cmp_w2, cmp_b2, k_norm_w[0])
    n_slc_s = -(-(past_len + dec_seq) // SLC_BLOCK)
    q_s4 = q_s.reshape(dbs, N_KV, hpg, HEAD_DIM)
    ocmp_s, idx_s = _nsa_sample_select(q_s4, kc_s, vct_s, past_len, n_slc_s)
    gates_s = gt_s.T.reshape(dbs, N_KV, hpg, N_BRANCH)
    o_s = _nsa_sample_attend(q_s4, gates_s, ocmp_s, idx_s, page_table,
                             cache_slc_kv.reshape(-1, HEAD_DIM), slc_s,
                             cache_win_kv.reshape(-1, HEAD_DIM), win_s, past_len)

    x = mm2(o_p, o_s.reshape(dbs, qcols), wo, epi="residual", res=x,
            gate=(mod_l[1][0][2], mod_l[1][1][2]), name="nsa_out", **wide)
    x = ffn(x, 1, mod_l[1])
    y_prompt, y_sample = x[0].reshape(bsz, t_len, d), x[1].reshape(dbs, dec_seq, d)

    kv5 = lambda a, n, t: a.reshape(n, t, 2, N_KV, HEAD_DIM)
    win_buf = cache_win_kv.shape[1]
    win_s5 = kv5(win_s, dbs, 1)
    new_win_s = jnp.concatenate([cache_win_kv, win_s5], axis=1)[:, -win_buf:]
    return (y_prompt, y_sample, h_p[None], h_s[None], conv_p[None],
            jnp.swapaxes(conv_s, 0, 1)[None],
            kv5(cmp_p, bsz, t_len), kv5(cmp_s, dbs, 1), kv5(slc_p, bsz, t_len), kv5(slc_s, dbs, 1),
            kv5(win_p, bsz, t_len)[:, -WINDOW:], new_win_s)
```

```python
import functools

import jax
import jax.numpy as jnp
from jax import lax
from jax.experimental import pallas as pl
from jax.experimental.pallas import tpu as pltpu

F32 = jnp.float32
BF16 = jnp.bfloat16

HEAD_DIM = 128
N_KV = 4
N_BRANCH = 3
CMP_BLOCK = 32
CMP_STRIDE = 16
SLC_BLOCK = 64
N_SEL = 16
WINDOW = 512
CONV_W = 4
LRU_C = 8.0
PAGE_SIZE = 128
EPS = 1e-6
NEG = -1e30
FORCE = 1e4
TINY = 1e-30

KV_ROW = 2 * N_KV * HEAD_DIM
SLC_SHIFT = SLC_BLOCK.bit_length() - 1
SLC_PER_CHUNK = 4
KEY_CHUNK = SLC_PER_CHUNK * SLC_BLOCK
ONES_ROWS = 16
VMEM_LIMIT = 56 * 1024 * 1024


def _cparams(sem, vmem=None):
    return pltpu.CompilerParams(dimension_semantics=sem, vmem_limit_bytes=vmem or VMEM_LIMIT)


def _pick(n, cands):
    for c in cands:
        if n % c == 0:
            return c
    return n


def _dot(a, b):
    return jnp.dot(a, b, preferred_element_type=F32)


def _dot_nt(a, b):
    return lax.dot_general(a, b, (((1,), (1,)), ((), ())), preferred_element_type=F32)


def _sigmoid(x):
    return 0.5 * (jnp.tanh(0.5 * x) + 1.0)


def _split3(x):
    hi = x.astype(BF16)
    r1 = x - hi.astype(F32)
    mid = r1.astype(BF16)
    lo = (r1 - mid.astype(F32)).astype(BF16)
    return hi, mid, lo


def _norm_mod_kernel(x_ref, w_ref, shift_ref, scale_ref, o_ref):
    x = x_ref[...]
    y = x * lax.rsqrt(jnp.mean(x * x, axis=-1, keepdims=True) + EPS)
    y = y * w_ref[...]
    o_ref[...] = (y * (1.0 + scale_ref[0]) + shift_ref[0]).astype(o_ref.dtype)


def _norm_mod(x, w, shift, scale, out_dtype):
    m, d = x.shape
    nb, rb, _ = shift.shape
    rows_per_nb = m // nb
    tr = rows_per_nb if rb > 1 else _pick(rows_per_nb, (512, 256, 128, 64, 32, 16, 8))
    tiles_per_nb = rows_per_nb // tr
    mod_spec = pl.BlockSpec((1, rb, d), lambda i: (i // tiles_per_nb, 0, 0))
    return pl.pallas_call(
        _norm_mod_kernel,
        grid=(m // tr,),
        in_specs=[pl.BlockSpec((tr, d), lambda i: (i, 0)),
                  pl.BlockSpec((1, d), lambda i: (0, 0)), mod_spec, mod_spec],
        out_specs=pl.BlockSpec((tr, d), lambda i: (i, 0)),
        out_shape=jax.ShapeDtypeStruct((m, d), out_dtype),
        compiler_params=_cparams(("parallel",)),
        name="norm_mod",
    )(x, w.reshape(1, d), shift, scale)


def _head_rms(acc, w, post_scale):
    outs = []
    for h in range(acc.shape[1] // HEAD_DIM):
        sl = acc[:, h * HEAD_DIM:(h + 1) * HEAD_DIM]
        y = sl * lax.rsqrt(jnp.mean(sl * sl, axis=-1, keepdims=True) + EPS)
        y = y * w
        if post_scale is not None:
            y = y * post_scale
        outs.append(y)
    return jnp.concatenate(outs, axis=1)


def _mm_kernel(*refs, nk, ni, a_silu, has_bias, epi, n_norm_tiles, post_scale, has_small):
    it = iter(refs)
    a_ref, w_ref = next(it), next(it)
    w2_ref = next(it) if epi == "swiglu" else None
    b_ref = next(it) if has_bias else None
    res_ref, gate_ref = (next(it), next(it)) if epi == "residual" else (None, None)
    hn_ref = next(it) if epi == "headnorm" else None
    a2_ref = next(it) if has_small else None
    res2_ref, gate2_ref = (next(it), next(it)) if has_small and epi == "residual" else (None, None)
    o_ref = next(it)
    o2_ref = next(it) if has_small else None
    acc_ref = next(it) if nk > 1 else None
    acc2_ref = next(it) if nk > 1 and has_small else None

    w_bf = w_ref[...].astype(BF16)

    def run(lhs_ref, res, gate, out, acc_scr):
        a = lhs_ref[...]
        if a_silu:
            a = a * _sigmoid(a)
        a = a.astype(BF16)
        part = _dot(a, w_bf)

        def finish(acc):
            if has_bias:
                acc = acc + b_ref[...]
            if epi == "gelu":
                acc = jax.nn.gelu(acc)
            elif epi == "swiglu":
                acc = acc * _sigmoid(acc) * _dot(a, w2_ref[...].astype(BF16))
            elif epi == "residual":
                acc = res[...] + gate[0] * acc
            elif epi == "headnorm":
                normed = _head_rms(acc, hn_ref[...], post_scale)
                acc = jnp.where(pl.program_id(1) < n_norm_tiles, normed, acc)
            out[...] = acc.astype(out.dtype)

        if nk == 1:
            finish(part)
        else:
            k = pl.program_id(2)

            @pl.when(k == 0)
            def _():
                acc_scr[...] = part

            @pl.when((k > 0) & (k < nk - 1))
            def _():
                acc_scr[...] += part

            @pl.when(k == nk - 1)
            def _():
                finish(acc_scr[...] + part)

    run(a_ref, res_ref, gate_ref, o_ref, acc_ref)
    if has_small:
        last = pl.program_id(0) == ni - 1

        @pl.when(last)
        def _():
            run(a2_ref, res2_ref, gate2_ref, o2_ref, acc2_ref)

        @pl.when(jnp.logical_not(last))
        def _():
            o2_ref[...] = jnp.zeros(o2_ref.shape, o2_ref.dtype)


def _matmul(a, w, *, layer=0, col0=0, ncols=None, col0_b=None, bias=None, epi="none", res=None,
            gate=None, hn_w=None, n_norm_tiles=0, post_scale=None, a_silu=False, out_dtype=F32,
            small=None, tm=None, tn=None, tk=None, name="matmul"):
    m, kdim = a.shape
    ncols = ncols if ncols is not None else w.shape[-1]
    tm = tm or min(m, 1024, m // gate.shape[0] if gate is not None and gate.shape[1] == 1 else m)
    tn = tn or _pick(ncols, (512, 256, 128))
    tk = tk or kdim
    nk = kdim // tk
    assert w.shape[-2] == kdim
    assert m % tm == 0 and ncols % tn == 0 and kdim % tk == 0 and col0 % tn == 0
    assert epi != "swiglu" or (nk == 1 and col0_b % tn == 0)
    jo = col0 // tn
    spec = pl.BlockSpec

    def w_spec(j0):
        if w.ndim == 3:
            return spec((None, tk, tn), lambda i, j, k: (layer, k, j + j0))
        return spec((tk, tn), lambda i, j, k: (k, j + j0))

    in_specs = [spec((tm, tk), lambda i, j, k: (i, k)), w_spec(jo)]
    args = [a, w]
    if epi == "swiglu":
        in_specs.append(w_spec(col0_b // tn))
        args.append(w)
    if bias is not None:
        in_specs.append(spec((1, tn), lambda i, j, k: (0, j + jo)))
        args.append(bias.reshape(1, -1))
    if epi == "residual":
        nb, rb, _ = gate.shape
        tiles_per_nb = (m // nb) // tm
        assert (m // nb) % tm == 0
        assert tiles_per_nb >= 1 and (rb == 1 or rb == tm)
        in_specs += [spec((tm, tn), lambda i, j, k: (i, j)),
                     spec((1, rb, tn), lambda i, j, k: (i // tiles_per_nb, 0, j))]
        args += [res, gate]
    if epi == "headnorm":
        in_specs.append(spec((1, HEAD_DIM), lambda i, j, k: (0, 0)))
        args.append(hn_w.reshape(1, HEAD_DIM))
    ni, nj = m // tm, ncols // tn
    out_specs = [spec((tm, tn), lambda i, j, k: (i, j))]
    out_shape = [jax.ShapeDtypeStruct((m, ncols), out_dtype)]
    scratch = [pltpu.VMEM((tm, tn), F32)] if nk > 1 else []
    if small is not None:
        a2, res2, gate2 = small
        m2 = a2.shape[0]
        in_specs.append(spec((m2, tk), lambda i, j, k: (0, k)))
        args.append(a2)
        if epi == "residual":
            in_specs += [spec((m2, tn), lambda i, j, k: (0, j)),
                         spec((1, m2, tn), lambda i, j, k: (0, 0, j))]
            args += [res2, gate2]
        out_specs.append(spec((m2, tn), lambda i, j, k: (0, jnp.where(i == ni - 1, j, nj))))
        out_shape.append(jax.ShapeDtypeStruct((m2, ncols + tn), F32))
        if nk > 1:
            scratch.append(pltpu.VMEM((m2, tn), F32))
    kern = functools.partial(_mm_kernel, nk=nk, ni=ni, a_silu=a_silu, has_bias=bias is not None,
                             epi=epi, n_norm_tiles=n_norm_tiles, post_scale=post_scale,
                             has_small=small is not None)
    outs = pl.pallas_call(
        kern,
        grid=(ni, nj, nk),
        in_specs=in_specs,
        out_specs=out_specs,
        out_shape=out_shape,
        scratch_shapes=scratch,
        compiler_params=_cparams(("arbitrary",) * 3 if small is not None
                                 else ("parallel", "parallel", "arbitrary")),
        name=name,
    )(*args)
    return (outs[0], outs[1][:, :ncols]) if small is not None else outs[0]


def _gate_kernel(wg_ref, u_ref, o_ref):
    o_ref[...] = _sigmoid(_dot_nt(wg_ref[...].astype(BF16), u_ref[...].astype(BF16)))


def _branch_gates_t(u, wg_t):
    m, kdim = u.shape
    r = wg_t.shape[0]
    tm = _pick(m, (1024, 512, 256, 128))
    return pl.pallas_call(
        _gate_kernel,
        grid=(m // tm,),
        in_specs=[pl.BlockSpec((r, kdim), lambda i: (0, 0)),
                  pl.BlockSpec((tm, kdim), lambda i: (i, 0))],
        out_specs=pl.BlockSpec((r, tm), lambda i: (0, i)),
        out_shape=jax.ShapeDtypeStruct((r, m), F32),
        compiler_params=_cparams(("parallel",)),
        name="branch_gates",
    )(wg_t, u)


def _lru_coeffs(xb, gw_r, gw_i, gb, lam):
    xb16 = xb.astype(BF16)
    r = _sigmoid(_dot(xb16, gw_r) + gb[0:1, :])
    gi = _sigmoid(_dot(xb16, gw_i) + gb[1:2, :])
    z = -lam
    softplus = jnp.maximum(z, 0.0) + jnp.log1p(jnp.exp(-jnp.abs(z)))
    log_a = -LRU_C * r * softplus
    a = jnp.exp(log_a)
    y = -jnp.tanh(log_a) * (1.0 + a * a)
    return a, jnp.where(y > 0.0, y * lax.rsqrt(y), 0.0) * gi * xb


def _lru_step_kernel(x_ref, gy_ref, cprev_ref, hprev_ref, cw_ref, cb_ref, gw_ref, gb_ref, lam_ref,
                     hg_ref, nconv_ref, h_ref, *, bw, nbp):
    hist = CONV_W - 1
    x = x_ref[...]
    conv = cprev_ref[0] * cw_ref[0:1, :]
    for k in range(1, hist):
        conv = conv + cprev_ref[k] * cw_ref[k:k + 1, :]
    xc = cb_ref[...] + (conv + x * cw_ref[hist:hist + 1, :])
    for blk in range(nbp):
        cols = slice(blk * bw, (blk + 1) * bw)
        a, b = _lru_coeffs(xc[:, cols], gw_ref[0, blk].astype(BF16), gw_ref[1, blk].astype(BF16),
                           gb_ref[:, cols], lam_ref[:, cols])
        h = a * hprev_ref[:, cols] + b
        h_ref[:, cols] = h
        hg_ref[:, cols] = (h * gy_ref[:, cols].astype(F32)).astype(hg_ref.dtype)
    for k in range(hist - 1):
        nconv_ref[k] = cprev_ref[k + 1]
    nconv_ref[hist - 1] = x


def _lru_step(x, gy, conv_prev, h_prev, conv_w, conv_b, gate_w, gate_b, lam):
    b, w = x.shape
    nblk, bw = gate_w.shape[1], gate_w.shape[2]
    nbp = _pick(nblk, (4, 2, 1))
    cw = nbp * bw
    hist = CONV_W - 1
    rows = pl.BlockSpec((b, cw), lambda c: (0, c))
    hrows = pl.BlockSpec((hist, b, cw), lambda c: (0, 0, c))
    chan = lambda r: pl.BlockSpec((r, cw), lambda c: (0, c))
    return pl.pallas_call(
        functools.partial(_lru_step_kernel, bw=bw, nbp=nbp),
        grid=(w // cw,),
        in_specs=[rows, rows, hrows, rows, chan(CONV_W), chan(1),
                  pl.BlockSpec((2, nbp, bw, bw), lambda c: (0, c, 0, 0)), chan(2), chan(1)],
        out_specs=[rows, hrows, rows],
        out_shape=[jax.ShapeDtypeStruct((b, w), gy.dtype), jax.ShapeDtypeStruct((hist, b, w), F32),
                   jax.ShapeDtypeStruct((b, w), F32)],
        compiler_params=_cparams(("parallel",)),
        name="rglru_step",
    )(x, gy, conv_prev, h_prev, conv_w, conv_b.reshape(1, w), gate_w, gate_b, lam.reshape(1, w))


def _lru_kernel(x_ref, gy_ref, cprev_ref, hprev_ref, cw_ref, cb_ref, gw_ref, gb_ref, lam_ref,
                hg_ref, nconv_ref, hlast_ref,
                xbuf, gw_scr, a_scr, b_scr, hs_scr, h_scr, *, tt, nt, bw, nbp):
    t = pl.program_id(2)
    hist = CONV_W - 1
    base = 8

    @pl.when(t == 0)
    def _():
        xbuf[base - hist:base, :] = cprev_ref[0]
        h_scr[...] = hprev_ref[0]
        gw_scr[...] = gw_ref[...].astype(BF16)

    @pl.when(t > 0)
    def _():
        xbuf[base - hist:base, :] = xbuf[base + tt - hist:base + tt, :]

    xbuf[base:base + tt, :] = x_ref[0]
    conv = xbuf[base - hist:base - hist + tt, :] * cw_ref[0:1, :]
    for k in range(1, CONV_W):
        conv = conv + xbuf[base - hist + k:base - hist + k + tt, :] * cw_ref[k:k + 1, :]
    xc = cb_ref[...] + conv

    for blk in range(nbp):
        cols = slice(blk * bw, (blk + 1) * bw)
        a_scr[:, cols], b_scr[:, cols] = _lru_coeffs(
            xc[:, cols], gw_scr[0, blk], gw_scr[1, blk], gb_ref[:, cols], lam_ref[:, cols])

    def step(i, h):
        h = a_scr[pl.ds(i, 1), :] * h + b_scr[pl.ds(i, 1), :]
        hs_scr[pl.ds(i, 1), :] = h
        return h

    h_scr[...] = lax.fori_loop(0, tt, step, h_scr[...], unroll=min(tt, 8))
    hg_ref[0] = (hs_scr[...] * gy_ref[0].astype(F32)).astype(hg_ref.dtype)

    @pl.when(t == nt - 1)
    def _():
        nconv_ref[0] = xbuf[base + tt - hist:base + tt, :]
        hlast_ref[0] = h_scr[...]


def _lru(x, gy, conv_prev, h_prev, conv_w, conv_b, gate_w, gate_b, lam):
    b, t, w = x.shape
    nblk, bw = gate_w.shape[1], gate_w.shape[2]
    nbp = _pick(nblk, (4, 2, 1))
    cw = nbp * bw
    tt = _pick(t, (256, 128, 64, 32, 16, 8))
    nt = t // tt
    hist = CONV_W - 1
    kern = functools.partial(_lru_kernel, tt=tt, nt=nt, bw=bw, nbp=nbp)
    row = lambda bi, c, ti: (bi, ti, c)
    fixed = lambda bi, c, ti: (bi, 0, c)
    chan = lambda bi, c, ti: (0, c)
    hg, nconv, hlast = pl.pallas_call(
        kern,
        grid=(b, w // cw, nt),
        in_specs=[pl.BlockSpec((1, tt, cw), row), pl.BlockSpec((1, tt, cw), row),
                  pl.BlockSpec((1, hist, cw), fixed), pl.BlockSpec((1, 1, cw), fixed),
                  pl.BlockSpec((CONV_W, cw), chan), pl.BlockSpec((1, cw), chan),
                  pl.BlockSpec((2, nbp, bw, bw), lambda bi, c, ti: (0, c, 0, 0)),
                  pl.BlockSpec((2, cw), chan), pl.BlockSpec((1, cw), chan)],
        out_specs=[pl.BlockSpec((1, tt, cw), row), pl.BlockSpec((1, hist, cw), fixed),
                   pl.BlockSpec((1, 1, cw), fixed)],
        out_shape=[jax.ShapeDtypeStruct((b, t, w), BF16),
                   jax.ShapeDtypeStruct((b, hist, w), F32),
                   jax.ShapeDtypeStruct((b, 1, w), F32)],
        scratch_shapes=[pltpu.VMEM((8 + tt, cw), F32), pltpu.VMEM((2, nbp, bw, bw), BF16),
                        pltpu.VMEM((tt, cw), F32), pltpu.VMEM((tt, cw), F32),
                        pltpu.VMEM((tt, cw), F32), pltpu.VMEM((1, cw), F32)],
        compiler_params=_cparams(("parallel", "parallel", "arbitrary")),
        name="rglru",
    )(x, gy, conv_prev, h_prev.reshape(b, 1, w), conv_w, conv_b.reshape(1, w), gate_w, gate_b,
      lam.reshape(1, w))
    return hg, nconv, hlast.reshape(b, w)


def _cmp_w_pq(w1_ref, kv):
    half = CMP_STRIDE * HEAD_DIM
    return jnp.concatenate([w1_ref[kv, :half, :], w1_ref[kv, half:, :]], axis=1).astype(BF16)


def _cmp_pq_kernel(x_ref, w1_ref, o_ref, slab):
    nc = slab.shape[0] // CMP_STRIDE
    for kv in range(2):
        rows = []
        for g in range(N_KV):
            off = (kv * N_KV + g) * HEAD_DIM
            slab[...] = x_ref[0, :, off:off + HEAD_DIM]
            rows.append(jnp.concatenate(
                [slab[pl.ds(r, nc, stride=CMP_STRIDE), :].astype(BF16) for r in range(CMP_STRIDE)], axis=1))
        pq = _dot(jnp.concatenate(rows, axis=0), _cmp_w_pq(w1_ref, kv))
        for g in range(N_KV):
            c0 = (kv * N_KV + g) * 2 * HEAD_DIM
            o_ref[0, :, c0:c0 + 2 * HEAD_DIM] = pq[g * nc:(g + 1) * nc, :]


def _cmp_pq_paged_kernel(pt_ref, *refs, n_src):
    src, (w1_ref, o_ref, p_scr, q_scr) = refs[:n_src], refs[n_src:]
    slots = 2 * N_KV
    cpp = PAGE_SIZE // CMP_STRIDE
    chunk_rows = CMP_STRIDE * slots
    lhs = jnp.concatenate(
        [jnp.concatenate([s[c * chunk_rows + r * slots:c * chunk_rows + (r + 1) * slots, :]
                          for r in range(CMP_STRIDE)], axis=1)
         for s in src for c in range(cpp)], axis=0).astype(BF16)
    both = _dot(lhs, jnp.concatenate([_cmp_w_pq(w1_ref, 0), _cmp_w_pq(w1_ref, 1)], axis=1))
    is_v = (lax.broadcasted_iota(jnp.int32, (both.shape[0], 1), 0) & (slots - 1)) >= N_KV
    pq = jnp.where(is_v, both[:, 2 * HEAD_DIM:], both[:, :2 * HEAD_DIM])
    p_scr[...] = pq[:, :HEAD_DIM]
    q_scr[...] = pq[:, HEAD_DIM:]
    nc = n_src * cpp
    for slot in range(slots):
        c0 = slot * 2 * HEAD_DIM
        o_ref[0, :, c0:c0 + HEAD_DIM] = p_scr[pl.ds(slot, nc, stride=slots), :]
        o_ref[0, :, c0 + HEAD_DIM:c0 + 2 * HEAD_DIM] = q_scr[pl.ds(slot, nc, stride=slots), :]


def _cmp_pq(rows, w1, *, page_table=None, pages_per_step=16):
    out_cols = 2 * N_KV * 2 * HEAD_DIM
    if page_table is None:
        nb, t_len, _ = rows.shape
        nch = t_len // CMP_STRIDE
        tc = _pick(nch, (128, 64, 32, 16, 8))
        return pl.pallas_call(
            _cmp_pq_kernel,
            grid=(nb, nch // tc),
            in_specs=[pl.BlockSpec((1, tc * CMP_STRIDE, KV_ROW), lambda b, c: (b, c, 0)),
                      pl.BlockSpec(w1.shape, lambda b, c: (0, 0, 0))],
            out_specs=pl.BlockSpec((1, tc, out_cols), lambda b, c: (b, c, 0)),
            out_shape=jax.ShapeDtypeStruct((nb, nch, out_cols), F32),
            scratch_shapes=[pltpu.VMEM((tc * CMP_STRIDE, HEAD_DIM), F32)],
            compiler_params=_cparams(("parallel", "parallel")),
            name="cmp_pq",
        )(rows, w1)
    nb, n_pages = page_table.shape
    page_rows = PAGE_SIZE * 2 * N_KV
    cpp = PAGE_SIZE // CMP_STRIDE
    pps = _pick(n_pages, (pages_per_step, 8, 4, 2, 1))
    src_specs = [pl.BlockSpec((page_rows, HEAD_DIM),
                              functools.partial(lambda b, s, pt, p: (pt[b, s * pps + p], 0), p=p))
                 for p in range(pps)]
    grid_spec = pltpu.PrefetchScalarGridSpec(
        num_scalar_prefetch=1,
        grid=(nb, n_pages // pps),
        in_specs=src_specs + [pl.BlockSpec(w1.shape, lambda b, s, pt: (0, 0, 0))],
        out_specs=pl.BlockSpec((1, pps * cpp, out_cols), lambda b, s, pt: (b, s, 0)),
        scratch_shapes=[pltpu.VMEM((pps * cpp * 2 * N_KV, HEAD_DIM), F32)] * 2,
    )
    return pl.pallas_call(
        functools.partial(_cmp_pq_paged_kernel, n_src=pps),
        grid_spec=grid_spec,
        out_shape=jax.ShapeDtypeStruct((nb, n_pages * cpp, out_cols), F32),
        compiler_params=_cparams(("parallel", "parallel")),
        name="cmp_pq_paged",
    )(page_table, *([rows] * pps), w1)


def _cmp_finish_kernel(pq_ref, pe_ref, w1_ref, b1_ref, w2_ref, b2_ref, kn_ref, kc_ref, vct_ref, *, nch):
    is_block = lax.broadcasted_iota(jnp.int32, (nch, 1), 0) < nch - 1
    for kv in range(2):
        pe8 = jnp.broadcast_to(pe_ref[kv], (8, pe_ref.shape[2])).astype(BF16)
        const = _dot(pe8, w1_ref[kv].astype(BF16))[0:1, :] + b1_ref[kv]
        w2 = w2_ref[kv].astype(BF16)
        for g in range(N_KV):
            c0 = (kv * N_KV + g) * 2 * HEAD_DIM
            nxt = jnp.concatenate([pq_ref[0, 1:nch, c0 + HEAD_DIM:c0 + 2 * HEAD_DIM],
                                   jnp.zeros((1, HEAD_DIM), F32)], axis=0)
            hid = pq_ref[0, :, c0:c0 + HEAD_DIM] + nxt + const
            out = _dot(jax.nn.gelu(hid).astype(BF16), w2) + b2_ref[kv]
            if kv == 0:
                out = out * lax.rsqrt(jnp.mean(out * out, axis=-1, keepdims=True) + EPS) * kn_ref[...]
            out = jnp.where(is_block, out, 0.0)
            if kv == 0:
                kc_ref[0, g] = out.astype(kc_ref.dtype)
            else:
                vct_ref[0, g] = out.T.astype(vct_ref.dtype)


def _cmp_finish(pq, pe, w1, b1, w2, b2, k_norm):
    nb, nch, cols = pq.shape
    full = lambda shape: pl.BlockSpec(shape, lambda b: (0,) * len(shape))
    return pl.pallas_call(
        functools.partial(_cmp_finish_kernel, nch=nch),
        grid=(nb,),
        in_specs=[pl.BlockSpec((1, nch, cols), lambda b: (b, 0, 0)),
                  full((2, 1, CMP_BLOCK * HEAD_DIM)), full(w1.shape), full((2, 1, HEAD_DIM)),
                  full(w2.shape), full((2, 1, HEAD_DIM)), full((1, HEAD_DIM))],
        out_specs=[pl.BlockSpec((1, N_KV, nch, HEAD_DIM), lambda b: (b, 0, 0, 0)),
                   pl.BlockSpec((1, N_KV, HEAD_DIM, nch), lambda b: (b, 0, 0, 0))],
        out_shape=[jax.ShapeDtypeStruct((nb, N_KV, nch, HEAD_DIM), BF16),
                   jax.ShapeDtypeStruct((nb, N_KV, HEAD_DIM, nch), BF16)],
        compiler_params=_cparams(("parallel",)),
        name="cmp_finish",
    )(pq, pe.reshape(2, 1, -1), w1, b1.reshape(2, 1, -1), w2, b2.reshape(2, 1, -1),
      k_norm.reshape(1, HEAD_DIM))


def _per_head(fn, x, hpg):
    tq = x.shape[1] // hpg
    return jnp.concatenate([fn(x[:, h * tq:(h + 1) * tq]) for h in range(hpg)], axis=1)


def _flash_update(s, m_scr, acc_scr, v_aug):
    m_old = m_scr[...]
    m_new = jnp.maximum(m_old, jnp.max(s, axis=0, keepdims=True))
    p = jnp.exp((s - m_new).astype(BF16))
    acc_scr[...] = jnp.exp(m_old - m_new) * acc_scr[...] + _dot(v_aug, p)
    m_scr[...] = m_new


def _nsa_prompt_kernel(q_ref, gt_ref, kc_ref, vct_ref, ks_ref, vs_ref, kw_ref, vw_ref, o_ref,
                       ks_scr, vst_scr, kw_scr, vwt_scr, qt_scr, score_scr, sel_scr,
                       bias_scr, wbias_scr, m_scr, acc_scr, o_scr, *, tq, t_len, hpg, n_cmp_pad):
    g = pl.program_id(1)
    qi = pl.program_id(2)
    n_chunks = t_len // KEY_CHUNK
    n_slc = t_len // SLC_BLOCK
    win_chunks = WINDOW // KEY_CHUNK
    n_wb = tq // KEY_CHUNK + win_chunks
    nl = hpg * tq

    @pl.when(qi == 0)
    def _():
        for c in range(n_chunks):
            rows = slice(c * KEY_CHUNK, (c + 1) * KEY_CHUNK)
            ks_scr[rows, :] = ks_ref[0, rows, :].astype(BF16)
            kw_scr[rows, :] = kw_ref[0, rows, :].astype(BF16)
            vst_scr[0:HEAD_DIM, rows] = vs_ref[0, rows, :].T.astype(BF16)
            vwt_scr[0:HEAD_DIM, rows] = vw_ref[0, rows, :].T.astype(BF16)
        ones = jnp.ones((ONES_ROWS, t_len), BF16)
        vst_scr[HEAD_DIM:, :] = ones
        vwt_scr[HEAD_DIM:, :] = ones
        kl = lax.broadcasted_iota(jnp.int32, (KEY_CHUNK, tq), 0)
        for c in range(n_wb):
            dlt = (win_chunks - c) * KEY_CHUNK + lax.broadcasted_iota(jnp.int32, (KEY_CHUNK, tq), 1) - kl
            wbias_scr[c] = jnp.where((dlt >= 0) & (dlt < WINDOW), 0.0, NEG)

    for h in range(hpg):
        qt_scr[:, h * tq:(h + 1) * tq] = (
            q_ref[:, h * HEAD_DIM:(h + 1) * HEAD_DIM].astype(F32).T.astype(BF16))
    qpos = qi * tq + lax.broadcasted_iota(jnp.int32, (1, tq), 1)

    def gate_row(branch):
        return jnp.concatenate(
            [gt_ref[pl.ds((g * hpg + h) * N_BRANCH + branch, 1), :] for h in range(hpg)], axis=1)

    cmp_idx = lax.broadcasted_iota(jnp.int32, (n_cmp_pad, tq), 0)
    cmp_mask = cmp_idx * CMP_STRIDE + (CMP_BLOCK - 1) <= qpos
    s = _per_head(lambda x: jnp.where(cmp_mask, x, NEG), _dot(kc_ref[0, 0], qt_scr[...]), hpg)
    e = _per_head(lambda x: jnp.where(cmp_mask, x, 0.0),
                  jnp.exp(s - jnp.max(s, axis=0, keepdims=True)), hpg)
    p = e / jnp.maximum(jnp.sum(e, axis=0, keepdims=True), TINY)
    o_scr[...] = gate_row(0) * _dot(vct_ref[0, 0], p.astype(BF16))
    p_grp = p[:, 0:tq]
    for h in range(1, hpg):
        p_grp = p_grp + p[:, h * tq:(h + 1) * tq]

    ratio = SLC_BLOCK // CMP_STRIDE
    jj = lax.broadcasted_iota(jnp.int32, (n_slc, n_cmp_pad), 0)
    mm = lax.broadcasted_iota(jnp.int32, (n_slc, n_cmp_pad), 1)
    pool = ((mm >= ratio * jj - 1) & (mm <= ratio * jj + ratio - 1)).astype(BF16)
    p_slc = sum(_dot(pool, piece) for piece in _split3(p_grp))
    blk = lax.broadcasted_iota(jnp.int32, (n_slc, tq), 0)
    cur = lax.shift_right_arithmetic(qpos, SLC_SHIFT)
    forced = (blk == 0) | (blk == cur) | (blk == cur - 1)
    valid = blk * SLC_BLOCK <= qpos
    score = jnp.where(forced, FORCE, jnp.where(valid, p_slc, -1.0))
    score_scr[...] = score

    def rank_step(j, rank):
        other = score_scr[pl.ds(j, 1), :]
        beats = (other > score) | ((other == score) & (j < blk))
        return rank + beats.astype(jnp.int32)

    rank = lax.fori_loop(0, n_slc, rank_step, jnp.zeros((n_slc, tq), jnp.int32),
                         unroll=_pick(n_slc, (4, 2, 1)))
    sel_scr[...] = (rank < min(N_SEL, n_slc)).astype(F32)

    def slc_bias(c, _):
        for r in range(SLC_PER_CHUNK):
            sel_row = sel_scr[pl.ds(c * SLC_PER_CHUNK + r, 1), :]
            kpos = c * KEY_CHUNK + r * SLC_BLOCK + lax.broadcasted_iota(jnp.int32, (SLC_BLOCK, tq), 0)
            ok = (sel_row > 0.5) & (kpos <= qpos)
            bias_scr[pl.ds(pl.multiple_of(c * KEY_CHUNK + r * SLC_BLOCK, SLC_BLOCK), SLC_BLOCK), :] = (
                jnp.where(ok, 0.0, NEG))
        return 0

    n_live = (qi * tq + tq + KEY_CHUNK - 1) // KEY_CHUNK
    n_pairs = (n_live + 1) // 2
    lax.fori_loop(0, 2 * n_pairs, slc_bias, 0)

    def scores(k_scr, chunk, bias):
        keys = pl.ds(pl.multiple_of(chunk * KEY_CHUNK, KEY_CHUNK), KEY_CHUNK)
        return _per_head(lambda x: x + bias(keys), _dot(k_scr[keys, :], qt_scr[...]), hpg), keys

    def reset():
        m_scr[...] = jnp.full((1, nl), NEG, F32)
        acc_scr[...] = jnp.zeros((HEAD_DIM + ONES_ROWS, nl), F32)

    def add_branch(branch):
        inv = 1.0 / jnp.maximum(acc_scr[HEAD_DIM:HEAD_DIM + 1, :], TINY)
        o_scr[...] += (gate_row(branch) * inv) * acc_scr[0:HEAD_DIM, :]

    reset()

    def slc_pair(i, _):
        tiles = [scores(ks_scr, 2 * i + u, lambda keys: bias_scr[keys, :]) for u in range(2)]
        for s, keys in tiles:
            _flash_update(s, m_scr, acc_scr, vst_scr[:, keys])
        return 0

    lax.fori_loop(0, n_pairs, slc_pair, 0)
    add_branch(1)

    reset()
    first_key_chunk = qi * (tq // KEY_CHUNK) - win_chunks
    tiles = []
    for c in reversed(range(n_wb)):
        chunk = first_key_chunk + c
        band = jnp.where(chunk >= 0, wbias_scr[c], NEG)
        tiles.append(scores(kw_scr, jnp.maximum(chunk, 0), lambda keys: band))
    for s, keys in tiles:
        _flash_update(s, m_scr, acc_scr, vwt_scr[:, keys])
    add_branch(2)

    for h in range(hpg):
        o_ref[:, h * HEAD_DIM:(h + 1) * HEAD_DIM] = o_scr[:, h * tq:(h + 1) * tq].T.astype(o_ref.dtype)


def _nsa_prompt(q, gates_t, kc, vct, slc_rows, win_rows, bsz, t_len):
    m, qcols = q.shape
    hpg = qcols // (N_KV * HEAD_DIM)
    gw = hpg * HEAD_DIM
    tq = KEY_CHUNK
    assert t_len % (2 * KEY_CHUNK) == 0 and WINDOW % KEY_CHUNK == 0 and tq % KEY_CHUNK == 0
    nq = t_len // tq
    n_cmp_pad = kc.shape[2]
    n_slc = t_len // SLC_BLOCK
    n_wb = tq // KEY_CHUNK + WINDOW // KEY_CHUNK
    kern = functools.partial(_nsa_prompt_kernel, tq=tq, t_len=t_len, hpg=hpg, n_cmp_pad=n_cmp_pad)
    kv_k = lambda b, g, i: (b, 0, g)
    kv_v = lambda b, g, i: (b, 0, N_KV + g)
    return pl.pallas_call(
        kern,
        grid=(bsz, N_KV, nq),
        in_specs=[pl.BlockSpec((tq, gw), lambda b, g, i: (b * nq + i, g)),
                  pl.BlockSpec((gates_t.shape[0], tq), lambda b, g, i: (0, b * nq + i)),
                  pl.BlockSpec((1, 1, n_cmp_pad, HEAD_DIM), lambda b, g, i: (b, g, 0, 0)),
                  pl.BlockSpec((1, 1, HEAD_DIM, n_cmp_pad), lambda b, g, i: (b, g, 0, 0)),
                  pl.BlockSpec((1, t_len, HEAD_DIM), kv_k), pl.BlockSpec((1, t_len, HEAD_DIM), kv_v),
                  pl.BlockSpec((1, t_len, HEAD_DIM), kv_k), pl.BlockSpec((1, t_len, HEAD_DIM), kv_v)],
        out_specs=pl.BlockSpec((tq, gw), lambda b, g, i: (b * nq + i, g)),
        out_shape=jax.ShapeDtypeStruct((m, qcols), BF16),
        scratch_shapes=[pltpu.VMEM((t_len, HEAD_DIM), BF16), pltpu.VMEM((HEAD_DIM + ONES_ROWS, t_len), BF16),
                        pltpu.VMEM((t_len, HEAD_DIM), BF16), pltpu.VMEM((HEAD_DIM + ONES_ROWS, t_len), BF16),
                        pltpu.VMEM((HEAD_DIM, hpg * tq), BF16),
                        pltpu.VMEM((n_slc, tq), F32), pltpu.VMEM((n_slc, tq), F32),
                        pltpu.VMEM((t_len, tq), F32), pltpu.VMEM((n_wb, KEY_CHUNK, tq), F32),
                        pltpu.VMEM((1, hpg * tq), F32),
                        pltpu.VMEM((HEAD_DIM + ONES_ROWS, hpg * tq), F32),
                        pltpu.VMEM((HEAD_DIM, hpg * tq), F32)],
        compiler_params=_cparams(("parallel", "parallel", "arbitrary")),
        name="nsa_prompt",
    )(q, gates_t, kc, vct, slc_rows, slc_rows, win_rows, win_rows)


def _nsa_sample_select_kernel(q_ref, kc_ref, vct_ref, ocmp_ref, idx_ref, *, qpos, n_slc, nsp):
    q = q_ref[0, 0].astype(BF16)
    ncp = kc_ref.shape[2]
    s = _dot_nt(q, kc_ref[0, 0])
    m_idx = lax.broadcasted_iota(jnp.int32, (1, ncp), 1)
    mask = m_idx * CMP_STRIDE + (CMP_BLOCK - 1) <= qpos
    s = jnp.where(mask, s, NEG)
    e = jnp.where(mask, jnp.exp(s - jnp.max(s, axis=-1, keepdims=True)), 0.0)
    p = e / jnp.maximum(jnp.sum(e, axis=-1, keepdims=True), TINY)
    ocmp_ref[0, 0] = _dot_nt(p.astype(BF16), vct_ref[0, 0])

    ratio = SLC_BLOCK // CMP_STRIDE
    p_grp = jnp.broadcast_to(jnp.sum(p, axis=0, keepdims=True), (8, ncp))
    mm = lax.broadcasted_iota(jnp.int32, (ncp, nsp), 0)
    jj = lax.broadcasted_iota(jnp.int32, (ncp, nsp), 1)
    pool_t = ((mm >= ratio * jj - 1) & (mm <= ratio * jj + ratio - 1)).astype(BF16)
    p_slc = sum(_dot(piece, pool_t) for piece in _split3(p_grp))[0:1, :]
    blk = lax.broadcasted_iota(jnp.int32, (1, nsp), 1)
    cur = qpos // SLC_BLOCK
    forced = (blk == 0) | (blk == cur) | (blk == cur - 1)
    score = jnp.where(forced, FORCE, jnp.where(blk * SLC_BLOCK <= qpos, p_slc, -1.0))
    score = jnp.where(blk < n_slc, score, -2.0)

    ii = lax.broadcasted_iota(jnp.int32, (nsp, nsp), 0)
    jx = lax.broadcasted_iota(jnp.int32, (nsp, nsp), 1)
    mine = jnp.broadcast_to(score, (nsp, nsp))
    other = mine.T
    beats = (other > mine) | ((other == mine) & (ii < jx))
    rank = jnp.sum(beats.astype(F32), axis=0, keepdims=True)
    sel = (rank < float(N_SEL)).astype(F32)
    sel_other = jnp.broadcast_to(sel, (nsp, nsp)).T
    pos = jnp.sum(jnp.where(ii < jx, sel_other, 0.0), axis=0, keepdims=True)
    kk = lax.broadcasted_iota(jnp.int32, (N_SEL, nsp), 0).astype(F32)
    jrow = lax.broadcasted_iota(jnp.int32, (N_SEL, nsp), 1).astype(F32)
    hit = (sel > 0.5) & (pos == kk)
    idx = jnp.sum(jnp.where(hit, jrow, 0.0), axis=1, keepdims=True)
    idx_ref[0, 0] = jnp.broadcast_to(idx, (N_SEL, HEAD_DIM)).astype(jnp.int32)


def _nsa_sample_select(q, kc, vct, qpos, n_slc):
    dbs, _, hpg, _ = q.shape
    ncp = kc.shape[2]
    nsp = -(-n_slc // HEAD_DIM) * HEAD_DIM
    assert n_slc >= N_SEL
    blk4 = lambda s2, s3: pl.BlockSpec((1, 1, s2, s3), lambda b, g: (b, g, 0, 0))
    ocmp, idx = pl.pallas_call(
        functools.partial(_nsa_sample_select_kernel, qpos=qpos, n_slc=n_slc, nsp=nsp),
        grid=(dbs, N_KV),
        in_specs=[blk4(hpg, HEAD_DIM), blk4(ncp, HEAD_DIM), blk4(HEAD_DIM, ncp)],
        out_specs=[blk4(hpg, HEAD_DIM), blk4(N_SEL, HEAD_DIM)],
        out_shape=[jax.ShapeDtypeStruct((dbs, N_KV, hpg, HEAD_DIM), F32),
                   jax.ShapeDtypeStruct((dbs, N_KV, N_SEL, HEAD_DIM), jnp.int32)],
        compiler_params=_cparams(("parallel", "parallel")),
        name="nsa_sample_select",
    )(q, kc, vct)
    return ocmp, idx[..., 0]


def _nsa_sample_attend_kernel(pt_ref, idx_ref, q_ref, gate_ref, ocmp_ref, snew_ref, wnew_ref,
                              win_ref, *rest, qpos, n_cached, win_buf):
    blocks, o_ref = rest[:N_SEL], rest[N_SEL]
    b, g = pl.program_id(0), pl.program_id(1)
    slots = 2 * N_KV

    def group_rows(ref, n, col):
        return ref[pl.ds(col * N_KV + g, n, stride=slots), :]

    q = q_ref[0, 0].astype(BF16)
    qf = q.astype(F32)

    def attend(s, mask, v, k_new, v_new):
        s_self = jnp.sum(qf * k_new.astype(BF16).astype(F32), axis=-1, keepdims=True)
        s = jnp.where(mask, s, NEG)
        m = jnp.maximum(jnp.max(s, axis=-1, keepdims=True), s_self)
        e = jnp.where(mask, jnp.exp(s - m), 0.0)
        e_self = jnp.exp(s_self - m)
        l = jnp.sum(e, axis=-1, keepdims=True) + e_self
        acc = _dot(e.astype(BF16), v) + e_self.astype(BF16).astype(F32) * v_new.astype(BF16).astype(F32)
        return acc / jnp.maximum(l, TINY)

    n_keys = N_SEL * SLC_BLOCK
    k_all = jnp.concatenate([group_rows(blk, SLC_BLOCK, 0) for blk in blocks], axis=0).astype(BF16)
    v_all = jnp.concatenate([group_rows(blk, SLC_BLOCK, 1) for blk in blocks], axis=0).astype(BF16)
    lane = lax.broadcasted_iota(jnp.int32, (1, n_keys), 1)
    slot = lax.shift_right_arithmetic(lane, SLC_SHIFT)
    blk_id = jnp.zeros((1, n_keys), jnp.int32)
    for k in range(N_SEL):
        blk_id = jnp.where(slot == k, idx_ref[(b * N_KV + g) * N_SEL + k], blk_id)
    kpos = blk_id * SLC_BLOCK + (lane & (SLC_BLOCK - 1))
    slc_mask = (blk_id < n_cached) & (kpos <= qpos)
    o_slc = attend(_dot_nt(q, k_all), slc_mask, v_all,
                   snew_ref[0, pl.ds(g, 1), :], snew_ref[0, pl.ds(N_KV + g, 1), :])

    wi = lax.broadcasted_iota(jnp.int32, (1, win_buf), 1)
    dlt = win_buf - wi
    win_mask = (dlt >= 0) & (dlt < WINDOW) & (qpos - dlt >= 0)
    o_win = attend(_dot_nt(q, group_rows(win_ref, win_buf, 0).astype(BF16)), win_mask,
                   group_rows(win_ref, win_buf, 1).astype(BF16),
                   wnew_ref[0, pl.ds(g, 1), :], wnew_ref[0, pl.ds(N_KV + g, 1), :])

    gate = gate_ref[0, 0]
    o_ref[0, 0] = gate[:, 0:1] * ocmp_ref[0, 0] + gate[:, 1:2] * o_slc + gate[:, 2:3] * o_win


def _nsa_sample_attend(q, gates, ocmp, idx, page_table, slc_cache, slc_new, win_cache, win_new, qpos):
    dbs, _, hpg, _ = q.shape
    slots = 2 * N_KV
    win_buf = win_cache.shape[0] // (dbs * slots)
    n_cached = page_table.shape[1] * PAGE_SIZE // SLC_BLOCK
    bpp = PAGE_SIZE // SLC_BLOCK

    def blk_map(b, g, pt, ix, *, k):
        blk = jnp.minimum(ix[(b * N_KV + g) * N_SEL + k], n_cached - 1)
        return pt[b, blk // bpp] * bpp + blk % bpp, 0

    blk4 = lambda s2, s3: pl.BlockSpec((1, 1, s2, s3), lambda b, g, pt, ix: (b, g, 0, 0))
    new_spec = pl.BlockSpec((1, slots, HEAD_DIM), lambda b, g, pt, ix: (b, 0, 0))
    win_spec = pl.BlockSpec((win_buf * slots, HEAD_DIM), lambda b, g, pt, ix: (b, 0))
    gathered = [pl.BlockSpec((SLC_BLOCK * slots, HEAD_DIM), functools.partial(blk_map, k=k))
                for k in range(N_SEL)]
    grid_spec = pltpu.PrefetchScalarGridSpec(
        num_scalar_prefetch=2,
        grid=(dbs, N_KV),
        in_specs=[blk4(hpg, HEAD_DIM), blk4(hpg, N_BRANCH), blk4(hpg, HEAD_DIM), new_spec, new_spec,
                  win_spec] + gathered,
        out_specs=blk4(hpg, HEAD_DIM),
    )
    return pl.pallas_call(
        functools.partial(_nsa_sample_attend_kernel, qpos=qpos, n_cached=n_cached, win_buf=win_buf),
        grid_spec=grid_spec,
        out_shape=jax.ShapeDtypeStruct((dbs, N_KV, hpg, HEAD_DIM), F32),
        compiler_params=_cparams(("parallel", "parallel")),
        name="nsa_sample_attend",
    )(page_table, idx.reshape(-1), q, gates, ocmp,
      slc_new.reshape(dbs, slots, HEAD_DIM), win_new.reshape(dbs, slots, HEAD_DIM),
      win_cache, *([slc_cache] * N_SEL))


def kernel(x_prompt, x_sample, c_prompt, c_sample, state_lru_h, state_conv, cache_cmp_kv, cache_slc_kv, cache_win_kv, page_table, ada_w, ada_b, norm1_w, norm2_w, lru_w_in, lru_b_in, lru_conv_w, lru_conv_b, lru_gate_w, lru_gate_b, lru_lambda, lru_w_out, lru_b_out, ffn_w13, ffn_w2, kv_ada_w, kv_ada_b, kv_norm_w, w_kv, k_norm_w, cmp_pe, cmp_w1, cmp_b1, cmp_w2, cmp_b2, w_qg, q_norm_w, w_o):
    bsz, t_len, d = x_prompt.shape
    dbs, dec_seq, _ = x_sample.shape
    depth = ada_w.shape[0]
    n_a = lru_w_in.shape[0]
    assert dec_seq == 1 and depth == 2 and n_a == 1 and w_qg.shape[0] == 1
    lw = lru_w_in.shape[2] // 2
    d_ff = ffn_w2.shape[1]
    n_pages = page_table.shape[1]
    past_len = n_pages * PAGE_SIZE
    qcols = w_o.shape[1]
    hpg = qcols // (N_KV * HEAD_DIM)
    hist = CONV_W - 1
    branch_cols = KV_ROW

    n_c = bsz + dbs
    c_all = jnp.pad(jnp.concatenate([c_prompt, c_sample], axis=0), ((0, (-n_c) % 8), (0, 0)))
    mods = [_matmul(c_all, ada_w, layer=l, bias=ada_b[l], a_silu=True, tn=1024, name="ada")
            for l in range(depth)]
    kv_mod = _matmul(c_all, kv_ada_w, bias=kv_ada_b, a_silu=True, tn=1024, name="kv_ada")

    def split_mod(mat, n, prompt):
        parts = [mat[:, i * d:(i + 1) * d] for i in range(n)]
        if prompt:
            return [p[:bsz].reshape(bsz, 1, d) for p in parts]
        return [p[bsz:n_c].reshape(1, dbs, d) for p in parts]

    w13 = ffn_w13
    w2 = ffn_w2.astype(BF16)
    tk2 = _pick(d_ff, (d_ff // 2,)) if d_ff > 4096 else d_ff
    w_in, w_out, wq, wo = lru_w_in, lru_w_out, w_qg, w_o
    wg_t = w_qg[0, :, qcols:].T
    pe = cmp_pe.reshape(2, -1)

    def mm2(a_p, a_s, w, *, res=(None, None), gate=(None, None), **kw):
        return _matmul(a_p, w, res=res[0], gate=gate[0], small=(a_s, res[1], gate[1]), **kw)

    def norm2(x, w, mod, shift_i, scale_i):
        return (_norm_mod(x[0], w, mod[0][shift_i], mod[0][scale_i], BF16),
                _norm_mod(x[1], w, mod[1][shift_i], mod[1][scale_i], F32))

    def ffn(x, l, mod):
        u = norm2(x, norm2_w[l], mod, 3, 4)
        act = mm2(*u, w13, layer=l, col0=0, col0_b=d_ff, ncols=d_ff, epi="swiglu", tn=256,
                  out_dtype=BF16, name="ffn_up")
        return mm2(*act, w2, layer=l, tk=tk2, epi="residual", res=x, gate=(mod[0][5], mod[1][5]),
                   name="ffn_down")

    mod_l = [(split_mod(m, 6, True), split_mod(m, 6, False)) for m in mods]
    mod_kv = (split_mod(kv_mod, 2, True), split_mod(kv_mod, 2, False))
    wide = dict(tm=min(1024, t_len), tn=512)
    x = (x_prompt.reshape(bsz * t_len, d), x_sample.reshape(dbs, d))

    u = norm2(x, norm1_w[0], mod_l[0], 0, 1)
    gy = mm2(*u, w_in, col0=0, ncols=lw, bias=lru_b_in[0], epi="gelu", out_dtype=BF16, name="lru_in_y",
             **wide)
    xb = mm2(*u, w_in, col0=lw, ncols=lw, bias=lru_b_in[0], name="lru_in_x", **wide)
    lru_args = (lru_conv_w[0], lru_conv_b[0], lru_gate_w[0], lru_gate_b[0], lru_lambda[0])
    hg_p, conv_p, h_p = _lru(xb[0].reshape(bsz, t_len, lw), gy[0].reshape(bsz, t_len, lw),
                             jnp.zeros((bsz, hist, lw), F32), jnp.zeros((bsz, lw), F32), *lru_args)
    hg_s, conv_s, h_s = _lru_step(xb[1], gy[1], jnp.swapaxes(state_conv[0], 0, 1), state_lru_h[0],
                                  *lru_args)
    x = mm2(hg_p.reshape(bsz * t_len, lw), hg_s, w_out, bias=lru_b_out[0], epi="residual", res=x,
            gate=(mod_l[0][0][2], mod_l[0][1][2]), name="lru_out", **wide)
    x = ffn(x, 0, mod_l[0])

    s = norm2(x, kv_norm_w, mod_kv, 0, 1)
    kv_rows = [mm2(*s, w_kv, col0=0, ncols=branch_cols, name="kv_cmp", **wide)]
    for br in range(1, N_BRANCH):
        kv_rows.append(mm2(*s, w_kv, col0=br * branch_cols, ncols=branch_cols, epi="headnorm",
                           hn_w=k_norm_w[br], n_norm_tiles=branch_cols // 2 // wide["tn"],
                           name="kv_norm", **wide))
    (cmp_p, cmp_s), (slc_p, slc_s), (win_p, win_s) = kv_rows

    u = norm2(x, norm1_w[1], mod_l[1], 0, 1)
    q_p, q_s = mm2(*u, wq, col0=0, ncols=qcols, epi="headnorm", hn_w=q_norm_w[0],
                   n_norm_tiles=qcols // wide["tn"], post_scale=HEAD_DIM ** -0.5, out_dtype=BF16,
                   name="nsa_q", **wide)
    gt_p, gt_s = _branch_gates_t(u[0], wg_t), _branch_gates_t(u[1], wg_t)

    pq_p = _cmp_pq(cmp_p.reshape(bsz, t_len, KV_ROW), cmp_w1)
    kc_p, vct_p = _cmp_finish(pq_p, pe, cmp_w1, cmp_b1, cmp_w2, cmp_b2, k_norm_w[0])
    o_p = _nsa_prompt(q_p, gt_p, kc_p, vct_p, slc_p.reshape(bsz, t_len, KV_ROW),
                      win_p.reshape(bsz, t_len, KV_ROW), bsz, t_len)

    pq_s = _cmp_pq(cache_cmp_kv.reshape(-1, HEAD_DIM), cmp_w1, page_table=page_table)
    kc_s, vct_s = _cmp_finish(pq_s, pe, cmp_w1, cmp_b1, cmp_w2, cmp_b2, k_norm_w[0])
    n_slc_s = -(-(past_len + dec_seq) // SLC_BLOCK)
    q_s4 = q_s.reshape(dbs, N_KV, hpg, HEAD_DIM)
    ocmp_s, idx_s = _nsa_sample_select(q_s4, kc_s, vct_s, past_len, n_slc_s)
    gates_s = gt_s.T.reshape(dbs, N_KV, hpg, N_BRANCH)
    o_s = _nsa_sample_attend(q_s4, gates_s, ocmp_s, idx_s, page_table,
                             cache_slc_kv.reshape(-1, HEAD_DIM), slc_s,
                             cache_win_kv.reshape(-1, HEAD_DIM), win_s, past_len)

    x = mm2(o_p, o_s.reshape(dbs, qcols), wo, epi="residual", res=x,
            gate=(mod_l[1][0][2], mod_l[1][1][2]), name="nsa_out", **wide)
    x = ffn(x, 1, mod_l[1])
    y_prompt, y_sample = x[0].reshape(bsz, t_len, d), x[1].reshape(dbs, dec_seq, d)

    kv5 = lambda a, n, t: a.reshape(n, t, 2, N_KV, HEAD_DIM)
    win_buf = cache_win_kv.shape[1]
    win_s5 = kv5(win_s, dbs, 1)
    new_win_s = jnp.concatenate([cache_win_kv, win_s5], axis=1)[:, -win_buf:]
    return (y_prompt, y_sample, h_p[None], h_s[None], conv_p[None],
            jnp.swapaxes(conv_s, 0, 1)[None],
            kv5(cmp_p, bsz, t_len), kv5(cmp_s, dbs, 1), kv5(slc_p, bsz, t_len), kv5(slc_s, dbs, 1),
            kv5(win_p, bsz, t_len)[:, -WINDOW:], new_win_s)
```

```python
import functools

import jax
import jax.numpy as jnp
from jax import lax
from jax.experimental import pallas as pl
from jax.experimental.pallas import tpu as pltpu

F32 = jnp.float32
BF16 = jnp.bfloat16

HEAD_DIM = 128
N_KV = 4
N_BRANCH = 3
CMP_BLOCK = 32
CMP_STRIDE = 16
SLC_BLOCK = 64
N_SEL = 16
WINDOW = 512
CONV_W = 4
LRU_C = 8.0
PAGE_SIZE = 128
EPS = 1e-6
NEG = -1e30
FORCE = 1e4
TINY = 1e-30

KV_ROW = 2 * N_KV * HEAD_DIM
SLC_SHIFT = SLC_BLOCK.bit_length() - 1
SLC_PER_CHUNK = 4
KEY_CHUNK = SLC_PER_CHUNK * SLC_BLOCK
BF16_ROWS = 16
ONES_ROWS = BF16_ROWS
VMEM_LIMIT = 56 * 1024 * 1024


def _cparams(sem, vmem=None):
    return pltpu.CompilerParams(dimension_semantics=sem, vmem_limit_bytes=vmem or VMEM_LIMIT)


def _pick(n, cands):
    for c in cands:
        if n % c == 0:
            return c
    return n


def _dot(a, b):
    return jnp.dot(a, b, preferred_element_type=F32)


def _dot_nt(a, b):
    return lax.dot_general(a, b, (((1,), (1,)), ((), ())), preferred_element_type=F32)


def _sigmoid(x):
    return 0.5 * (jnp.tanh(0.5 * x) + 1.0)


def _split3(x):
    hi = x.astype(BF16)
    r1 = x - hi.astype(F32)
    mid = r1.astype(BF16)
    lo = (r1 - mid.astype(F32)).astype(BF16)
    return hi, mid, lo


def _norm_mod_kernel(x_ref, w_ref, shift_ref, scale_ref, o_ref):
    x = x_ref[...]
    y = x * lax.rsqrt(jnp.mean(x * x, axis=-1, keepdims=True) + EPS)
    y = y * w_ref[...]
    o_ref[...] = (y * (1.0 + scale_ref[0]) + shift_ref[0]).astype(o_ref.dtype)


def _norm_mod(x, w, shift, scale, out_dtype):
    m, d = x.shape
    nb, rb, _ = shift.shape
    rows_per_nb = m // nb
    tr = rows_per_nb if rb > 1 else _pick(rows_per_nb, (512, 256, 128, 64, 32, 16, 8))
    tiles_per_nb = rows_per_nb // tr
    mod_spec = pl.BlockSpec((1, rb, d), lambda i: (i // tiles_per_nb, 0, 0))
    return pl.pallas_call(
        _norm_mod_kernel,
        grid=(m // tr,),
        in_specs=[pl.BlockSpec((tr, d), lambda i: (i, 0)),
                  pl.BlockSpec((1, d), lambda i: (0, 0)), mod_spec, mod_spec],
        out_specs=pl.BlockSpec((tr, d), lambda i: (i, 0)),
        out_shape=jax.ShapeDtypeStruct((m, d), out_dtype),
        compiler_params=_cparams(("parallel",)),
        name="norm_mod",
    )(x, w.reshape(1, d), shift, scale)


def _head_rms(acc, w, post_scale):
    outs = []
    for h in range(acc.shape[1] // HEAD_DIM):
        sl = acc[:, h * HEAD_DIM:(h + 1) * HEAD_DIM]
        y = sl * lax.rsqrt(jnp.mean(sl * sl, axis=-1, keepdims=True) + EPS)
        y = y * w
        if post_scale is not None:
            y = y * post_scale
        outs.append(y)
    return jnp.concatenate(outs, axis=1)


def _mm_kernel(*refs, nk, ni, a_silu, has_bias, epi, n_norm_tiles, post_scale, has_small):
    it = iter(refs)
    a_ref, w_ref = next(it), next(it)
    w2_ref = next(it) if epi == "swiglu" else None
    b_ref = next(it) if has_bias else None
    res_ref, gate_ref = (next(it), next(it)) if epi == "residual" else (None, None)
    hn_ref = next(it) if epi == "headnorm" else None
    a2_ref = next(it) if has_small else None
    res2_ref, gate2_ref = (next(it), next(it)) if has_small and epi == "residual" else (None, None)
    o_ref = next(it)
    o2_ref = next(it) if has_small else None
    acc_ref = next(it) if nk > 1 else None
    acc2_ref = next(it) if nk > 1 and has_small else None
    stack_ref = next(it) if nk == 1 and has_small else None

    def dots(a):
        second = _dot(a, w2_ref[...].astype(BF16)) if epi == "swiglu" else None
        return _dot(a, w_ref[...].astype(BF16)), second

    def finish(acc, second, res, gate, out):
        if has_bias:
            acc = acc + b_ref[...]
        if epi == "gelu":
            acc = jax.nn.gelu(acc)
        elif epi == "swiglu":
            acc = acc * _sigmoid(acc) * second
        elif epi == "residual":
            acc = res[...] + gate[0] * acc
        elif epi == "headnorm":
            normed = _head_rms(acc, hn_ref[...], post_scale)
            acc = jnp.where(pl.program_id(1) < n_norm_tiles, normed, acc)
        out[...] = acc.astype(out.dtype)

    def run(lhs_ref, res, gate, out, acc_scr):
        a = lhs_ref[...]
        if a_silu:
            a = a * _sigmoid(a)
        part, second = dots(a.astype(BF16))
        if nk == 1:
            finish(part, second, res, gate, out)
        else:
            k = pl.program_id(2)

            @pl.when(k == 0)
            def _():
                acc_scr[...] = part

            @pl.when((k > 0) & (k < nk - 1))
            def _():
                acc_scr[...] += part

            @pl.when(k == nk - 1)
            def _():
                finish(acc_scr[...] + part, None, res, gate, out)

    if not has_small:
        run(a_ref, res_ref, gate_ref, o_ref, acc_ref)
        return

    last = pl.program_id(0) == ni - 1

    @pl.when(jnp.logical_not(last))
    def _():
        run(a_ref, res_ref, gate_ref, o_ref, acc_ref)
        o2_ref[...] = jnp.zeros(o2_ref.shape, o2_ref.dtype)

    if nk > 1:
        @pl.when(last)
        def _():
            run(a_ref, res_ref, gate_ref, o_ref, acc_ref)
            run(a2_ref, res2_ref, gate2_ref, o2_ref, acc2_ref)
        return

    tm, m2 = a_ref.shape[0], a2_ref.shape[0]

    @pl.when(last & (pl.program_id(1) == 0))
    def _():
        stack_ref[0:tm, :] = a_ref[...]
        pad = jnp.zeros((stack_ref.shape[0] - tm - m2, a2_ref.shape[1]), F32)
        stack_ref[tm:, :] = jnp.concatenate([a2_ref[...].astype(F32), pad], axis=0).astype(BF16)

    @pl.when(last)
    def _():
        part, second = dots(stack_ref[...])
        rows = lambda x, lo, hi: None if x is None else x[lo:hi]
        finish(part[0:tm], rows(second, 0, tm), res_ref, gate_ref, o_ref)
        finish(part[tm:tm + m2], rows(second, tm, tm + m2), res2_ref, gate2_ref, o2_ref)


def _matmul(a, w, *, layer=0, col0=0, ncols=None, col0_b=None, bias=None, epi="none", res=None,
            gate=None, hn_w=None, n_norm_tiles=0, post_scale=None, a_silu=False, out_dtype=F32,
            small=None, tm=None, tn=None, tk=None, name="matmul"):
    m, kdim = a.shape
    ncols = ncols if ncols is not None else w.shape[-1]
    tm = tm or min(m, 1024, m // gate.shape[0] if gate is not None and gate.shape[1] == 1 else m)
    tn = tn or _pick(ncols, (512, 256, 128))
    tk = tk or kdim
    nk = kdim // tk
    assert w.shape[-2] == kdim
    assert m % tm == 0 and ncols % tn == 0 and kdim % tk == 0 and col0 % tn == 0
    assert epi != "swiglu" or (nk == 1 and col0_b % tn == 0)
    jo = col0 // tn
    spec = pl.BlockSpec

    def w_spec(j0):
        if w.ndim == 3:
            return spec((None, tk, tn), lambda i, j, k: (layer, k, j + j0))
        return spec((tk, tn), lambda i, j, k: (k, j + j0))

    in_specs = [spec((tm, tk), lambda i, j, k: (i, k)), w_spec(jo)]
    args = [a, w]
    if epi == "swiglu":
        in_specs.append(w_spec(col0_b // tn))
        args.append(w)
    if bias is not None:
        in_specs.append(spec((1, tn), lambda i, j, k: (0, j + jo)))
        args.append(bias.reshape(1, -1))
    if epi == "residual":
        nb, rb, _ = gate.shape
        tiles_per_nb = (m // nb) // tm
        assert (m // nb) % tm == 0
        assert tiles_per_nb >= 1 and (rb == 1 or rb == tm)
        in_specs += [spec((tm, tn), lambda i, j, k: (i, j)),
                     spec((1, rb, tn), lambda i, j, k: (i // tiles_per_nb, 0, j))]
        args += [res, gate]
    if epi == "headnorm":
        in_specs.append(spec((1, HEAD_DIM), lambda i, j, k: (0, 0)))
        args.append(hn_w.reshape(1, HEAD_DIM))
    ni, nj = m // tm, ncols // tn
    out_specs = [spec((tm, tn), lambda i, j, k: (i, j))]
    out_shape = [jax.ShapeDtypeStruct((m, ncols), out_dtype)]
    scratch = [pltpu.VMEM((tm, tn), F32)] if nk > 1 else []
    if small is not None:
        a2, res2, gate2 = small
        m2 = a2.shape[0]
        in_specs.append(spec((m2, tk), lambda i, j, k: (0, k)))
        args.append(a2)
        if epi == "residual":
            in_specs += [spec((m2, tn), lambda i, j, k: (0, j)),
                         spec((1, m2, tn), lambda i, j, k: (0, 0, j))]
            args += [res2, gate2]
        out_specs.append(spec((m2, tn), lambda i, j, k: (0, jnp.where(i == ni - 1, j, nj))))
        out_shape.append(jax.ShapeDtypeStruct((m2, ncols + tn), F32))
        assert not a_silu and a.dtype == BF16 and m2 <= BF16_ROWS
        if nk > 1:
            scratch.append(pltpu.VMEM((m2, tn), F32))
        else:
            scratch.append(pltpu.VMEM((tm + BF16_ROWS, tk), BF16))
    kern = functools.partial(_mm_kernel, nk=nk, ni=ni, a_silu=a_silu, has_bias=bias is not None,
                             epi=epi, n_norm_tiles=n_norm_tiles, post_scale=post_scale,
                             has_small=small is not None)
    outs = pl.pallas_call(
        kern,
        grid=(ni, nj, nk),
        in_specs=in_specs,
        out_specs=out_specs,
        out_shape=out_shape,
        scratch_shapes=scratch,
        compiler_params=_cparams(("arbitrary",) * 3 if small is not None
                                 else ("parallel", "parallel", "arbitrary")),
        name=name,
    )(*args)
    return (outs[0], outs[1][:, :ncols]) if small is not None else outs[0]


def _gate_kernel(wg_ref, u_ref, o_ref):
    o_ref[...] = _sigmoid(_dot_nt(wg_ref[...].astype(BF16), u_ref[...].astype(BF16)))


def _branch_gates_t(u, wg_t):
    m, kdim = u.shape
    r = wg_t.shape[0]
    tm = _pick(m, (1024, 512, 256, 128))
    return pl.pallas_call(
        _gate_kernel,
        grid=(m // tm,),
        in_specs=[pl.BlockSpec((r, kdim), lambda i: (0, 0)),
                  pl.BlockSpec((tm, kdim), lambda i: (i, 0))],
        out_specs=pl.BlockSpec((r, tm), lambda i: (0, i)),
        out_shape=jax.ShapeDtypeStruct((r, m), F32),
        compiler_params=_cparams(("parallel",)),
        name="branch_gates",
    )(wg_t, u)


def _lru_coeffs(xb, gw_r, gw_i, gb, lam):
    xb16 = xb.astype(BF16)
    r = _sigmoid(_dot(xb16, gw_r) + gb[0:1, :])
    gi = _sigmoid(_dot(xb16, gw_i) + gb[1:2, :])
    z = -lam
    softplus = jnp.maximum(z, 0.0) + jnp.log1p(jnp.exp(-jnp.abs(z)))
    log_a = -LRU_C * r * softplus
    a = jnp.exp(log_a)
    y = -jnp.tanh(log_a) * (1.0 + a * a)
    return a, jnp.where(y > 0.0, y * lax.rsqrt(y), 0.0) * gi * xb


def _lru_step_kernel(x_ref, gy_ref, cprev_ref, hprev_ref, cw_ref, cb_ref, gw_ref, gb_ref, lam_ref,
                     hg_ref, nconv_ref, h_ref, *, bw, nbp):
    hist = CONV_W - 1
    x = x_ref[...]
    conv = cprev_ref[0] * cw_ref[0:1, :]
    for k in range(1, hist):
        conv = conv + cprev_ref[k] * cw_ref[k:k + 1, :]
    xc = cb_ref[...] + (conv + x * cw_ref[hist:hist + 1, :])
    for blk in range(nbp):
        cols = slice(blk * bw, (blk + 1) * bw)
        a, b = _lru_coeffs(xc[:, cols], gw_ref[0, blk].astype(BF16), gw_ref[1, blk].astype(BF16),
                           gb_ref[:, cols], lam_ref[:, cols])
        h = a * hprev_ref[:, cols] + b
        h_ref[:, cols] = h
        hg_ref[:, cols] = (h * gy_ref[:, cols].astype(F32)).astype(hg_ref.dtype)
    for k in range(hist - 1):
        nconv_ref[k] = cprev_ref[k + 1]
    nconv_ref[hist - 1] = x


def _lru_step(x, gy, conv_prev, h_prev, conv_w, conv_b, gate_w, gate_b, lam):
    b, w = x.shape
    nblk, bw = gate_w.shape[1], gate_w.shape[2]
    nbp = _pick(nblk, (4, 2, 1))
    cw = nbp * bw
    hist = CONV_W - 1
    rows = pl.BlockSpec((b, cw), lambda c: (0, c))
    hrows = pl.BlockSpec((hist, b, cw), lambda c: (0, 0, c))
    chan = lambda r: pl.BlockSpec((r, cw), lambda c: (0, c))
    return pl.pallas_call(
        functools.partial(_lru_step_kernel, bw=bw, nbp=nbp),
        grid=(w // cw,),
        in_specs=[rows, rows, hrows, rows, chan(CONV_W), chan(1),
                  pl.BlockSpec((2, nbp, bw, bw), lambda c: (0, c, 0, 0)), chan(2), chan(1)],
        out_specs=[rows, hrows, rows],
        out_shape=[jax.ShapeDtypeStruct((b, w), gy.dtype), jax.ShapeDtypeStruct((hist, b, w), F32),
                   jax.ShapeDtypeStruct((b, w), F32)],
        compiler_params=_cparams(("parallel",)),
        name="rglru_step",
    )(x, gy, conv_prev, h_prev, conv_w, conv_b.reshape(1, w), gate_w, gate_b, lam.reshape(1, w))


def _lru_kernel(x_ref, gy_ref, cprev_ref, hprev_ref, cw_ref, cb_ref, gw_ref, gb_ref, lam_ref,
                hg_ref, nconv_ref, hlast_ref,
                xbuf, gw_scr, a_scr, b_scr, hs_scr, h_scr, *, tt, nt, bw, nbp):
    t = pl.program_id(2)
    hist = CONV_W - 1
    base = 8

    @pl.when(t == 0)
    def _():
        xbuf[base - hist:base, :] = cprev_ref[0]
        h_scr[...] = hprev_ref[0]
        gw_scr[...] = gw_ref[...].astype(BF16)

    @pl.when(t > 0)
    def _():
        xbuf[base - hist:base, :] = xbuf[base + tt - hist:base + tt, :]

    xbuf[base:base + tt, :] = x_ref[0]
    conv = xbuf[base - hist:base - hist + tt, :] * cw_ref[0:1, :]
    for k in range(1, CONV_W):
        conv = conv + xbuf[base - hist + k:base - hist + k + tt, :] * cw_ref[k:k + 1, :]
    xc = cb_ref[...] + conv

    for blk in range(nbp):
        cols = slice(blk * bw, (blk + 1) * bw)
        a_scr[:, cols], b_scr[:, cols] = _lru_coeffs(
            xc[:, cols], gw_scr[0, blk], gw_scr[1, blk], gb_ref[:, cols], lam_ref[:, cols])

    def step(i, h):
        h = a_scr[pl.ds(i, 1), :] * h + b_scr[pl.ds(i, 1), :]
        hs_scr[pl.ds(i, 1), :] = h
        return h

    h_scr[...] = lax.fori_loop(0, tt, step, h_scr[...], unroll=min(tt, 8))
    hg_ref[0] = (hs_scr[...] * gy_ref[0].astype(F32)).astype(hg_ref.dtype)

    @pl.when(t == nt - 1)
    def _():
        nconv_ref[0] = xbuf[base + tt - hist:base + tt, :]
        hlast_ref[0] = h_scr[...]


def _lru(x, gy, conv_prev, h_prev, conv_w, conv_b, gate_w, gate_b, lam):
    b, t, w = x.shape
    nblk, bw = gate_w.shape[1], gate_w.shape[2]
    nbp = _pick(nblk, (4, 2, 1))
    cw = nbp * bw
    tt = _pick(t, (256, 128, 64, 32, 16, 8))
    nt = t // tt
    hist = CONV_W - 1
    kern = functools.partial(_lru_kernel, tt=tt, nt=nt, bw=bw, nbp=nbp)
    row = lambda bi, c, ti: (bi, ti, c)
    fixed = lambda bi, c, ti: (bi, 0, c)
    chan = lambda bi, c, ti: (0, c)
    hg, nconv, hlast = pl.pallas_call(
        kern,
        grid=(b, w // cw, nt),
        in_specs=[pl.BlockSpec((1, tt, cw), row), pl.BlockSpec((1, tt, cw), row),
                  pl.BlockSpec((1, hist, cw), fixed), pl.BlockSpec((1, 1, cw), fixed),
                  pl.BlockSpec((CONV_W, cw), chan), pl.BlockSpec((1, cw), chan),
                  pl.BlockSpec((2, nbp, bw, bw), lambda bi, c, ti: (0, c, 0, 0)),
                  pl.BlockSpec((2, cw), chan), pl.BlockSpec((1, cw), chan)],
        out_specs=[pl.BlockSpec((1, tt, cw), row), pl.BlockSpec((1, hist, cw), fixed),
                   pl.BlockSpec((1, 1, cw), fixed)],
        out_shape=[jax.ShapeDtypeStruct((b, t, w), BF16),
                   jax.ShapeDtypeStruct((b, hist, w), F32),
                   jax.ShapeDtypeStruct((b, 1, w), F32)],
        scratch_shapes=[pltpu.VMEM((8 + tt, cw), F32), pltpu.VMEM((2, nbp, bw, bw), BF16),
                        pltpu.VMEM((tt, cw), F32), pltpu.VMEM((tt, cw), F32),
                        pltpu.VMEM((tt, cw), F32), pltpu.VMEM((1, cw), F32)],
        compiler_params=_cparams(("parallel", "parallel", "arbitrary")),
        name="rglru",
    )(x, gy, conv_prev, h_prev.reshape(b, 1, w), conv_w, conv_b.reshape(1, w), gate_w, gate_b,
      lam.reshape(1, w))
    return hg, nconv, hlast.reshape(b, w)


def _cmp_w_pq(w1_ref, kv):
    half = CMP_STRIDE * HEAD_DIM
    return jnp.concatenate([w1_ref[kv, :half, :], w1_ref[kv, half:, :]], axis=1).astype(BF16)


def _cmp_pq_kernel(x_ref, w1_ref, o_ref, slab):
    nc = slab.shape[0] // CMP_STRIDE
    for kv in range(2):
        rows = []
        for g in range(N_KV):
            off = (kv * N_KV + g) * HEAD_DIM
            slab[...] = x_ref[0, :, off:off + HEAD_DIM]
            rows.append(jnp.concatenate(
                [slab[pl.ds(r, nc, stride=CMP_STRIDE), :].astype(BF16) for r in range(CMP_STRIDE)], axis=1))
        pq = _dot(jnp.concatenate(rows, axis=0), _cmp_w_pq(w1_ref, kv))
        for g in range(N_KV):
            c0 = (kv * N_KV + g) * 2 * HEAD_DIM
            o_ref[0, :, c0:c0 + 2 * HEAD_DIM] = pq[g * nc:(g + 1) * nc, :]


def _cmp_pq_paged_kernel(pt_ref, *refs, n_src):
    src, (w1_ref, o_ref, p_scr, q_scr) = refs[:n_src], refs[n_src:]
    slots = 2 * N_KV
    cpp = PAGE_SIZE // CMP_STRIDE
    chunk_rows = CMP_STRIDE * slots
    lhs = jnp.concatenate(
        [jnp.concatenate([s[c * chunk_rows + r * slots:c * chunk_rows + (r + 1) * slots, :]
                          for r in range(CMP_STRIDE)], axis=1)
         for s in src for c in range(cpp)], axis=0).astype(BF16)
    both = _dot(lhs, jnp.concatenate([_cmp_w_pq(w1_ref, 0), _cmp_w_pq(w1_ref, 1)], axis=1))
    is_v = (lax.broadcasted_iota(jnp.int32, (both.shape[0], 1), 0) & (slots - 1)) >= N_KV
    pq = jnp.where(is_v, both[:, 2 * HEAD_DIM:], both[:, :2 * HEAD_DIM])
    p_scr[...] = pq[:, :HEAD_DIM]
    q_scr[...] = pq[:, HEAD_DIM:]
    nc = n_src * cpp
    for slot in range(slots):
        c0 = slot * 2 * HEAD_DIM
        o_ref[0, :, c0:c0 + HEAD_DIM] = p_scr[pl.ds(slot, nc, stride=slots), :]
        o_ref[0, :, c0 + HEAD_DIM:c0 + 2 * HEAD_DIM] = q_scr[pl.ds(slot, nc, stride=slots), :]


def _cmp_pq(rows, w1, *, page_table=None, pages_per_step=16):
    out_cols = 2 * N_KV * 2 * HEAD_DIM
    if page_table is None:
        nb, t_len, _ = rows.shape
        nch = t_len // CMP_STRIDE
        tc = _pick(nch, (128, 64, 32, 16, 8))
        return pl.pallas_call(
            _cmp_pq_kernel,
            grid=(nb, nch // tc),
            in_specs=[pl.BlockSpec((1, tc * CMP_STRIDE, KV_ROW), lambda b, c: (b, c, 0)),
                      pl.BlockSpec(w1.shape, lambda b, c: (0, 0, 0))],
            out_specs=pl.BlockSpec((1, tc, out_cols), lambda b, c: (b, c, 0)),
            out_shape=jax.ShapeDtypeStruct((nb, nch, out_cols), F32),
            scratch_shapes=[pltpu.VMEM((tc * CMP_STRIDE, HEAD_DIM), F32)],
            compiler_params=_cparams(("parallel", "parallel")),
            name="cmp_pq",
        )(rows, w1)
    nb, n_pages = page_table.shape
    page_rows = PAGE_SIZE * 2 * N_KV
    cpp = PAGE_SIZE // CMP_STRIDE
    pps = _pick(n_pages, (pages_per_step, 8, 4, 2, 1))
    src_specs = [pl.BlockSpec((page_rows, HEAD_DIM),
                              functools.partial(lambda b, s, pt, p: (pt[b, s * pps + p], 0), p=p))
                 for p in range(pps)]
    grid_spec = pltpu.PrefetchScalarGridSpec(
        num_scalar_prefetch=1,
        grid=(nb, n_pages // pps),
        in_specs=src_specs + [pl.BlockSpec(w1.shape, lambda b, s, pt: (0, 0, 0))],
        out_specs=pl.BlockSpec((1, pps * cpp, out_cols), lambda b, s, pt: (b, s, 0)),
        scratch_shapes=[pltpu.VMEM((pps * cpp * 2 * N_KV, HEAD_DIM), F32)] * 2,
    )
    return pl.pallas_call(
        functools.partial(_cmp_pq_paged_kernel, n_src=pps),
        grid_spec=grid_spec,
        out_shape=jax.ShapeDtypeStruct((nb, n_pages * cpp, out_cols), F32),
        compiler_params=_cparams(("parallel", "parallel")),
        name="cmp_pq_paged",
    )(page_table, *([rows] * pps), w1)


def _cmp_finish_kernel(pq_ref, pe_ref, w1_ref, b1_ref, w2_ref, b2_ref, kn_ref, kc_ref, vct_ref, *, nch):
    is_block = lax.broadcasted_iota(jnp.int32, (nch, 1), 0) < nch - 1
    for kv in range(2):
        pe8 = jnp.broadcast_to(pe_ref[kv], (8, pe_ref.shape[2])).astype(BF16)
        const = _dot(pe8, w1_ref[kv].astype(BF16))[0:1, :] + b1_ref[kv]
        w2 = w2_ref[kv].astype(BF16)
        for g in range(N_KV):
            c0 = (kv * N_KV + g) * 2 * HEAD_DIM
            nxt = jnp.concatenate([pq_ref[0, 1:nch, c0 + HEAD_DIM:c0 + 2 * HEAD_DIM],
                                   jnp.zeros((1, HEAD_DIM), F32)], axis=0)
            hid = pq_ref[0, :, c0:c0 + HEAD_DIM] + nxt + const
            out = _dot(jax.nn.gelu(hid).astype(BF16), w2) + b2_ref[kv]
            if kv == 0:
                out = out * lax.rsqrt(jnp.mean(out * out, axis=-1, keepdims=True) + EPS) * kn_ref[...]
            out = jnp.where(is_block, out, 0.0)
            if kv == 0:
                kc_ref[0, g] = out.astype(kc_ref.dtype)
            else:
                vct_ref[0, g] = out.T.astype(vct_ref.dtype)


def _cmp_finish(pq, pe, w1, b1, w2, b2, k_norm):
    nb, nch, cols = pq.shape
    full = lambda shape: pl.BlockSpec(shape, lambda b: (0,) * len(shape))
    return pl.pallas_call(
        functools.partial(_cmp_finish_kernel, nch=nch),
        grid=(nb,),
        in_specs=[pl.BlockSpec((1, nch, cols), lambda b: (b, 0, 0)),
                  full((2, 1, CMP_BLOCK * HEAD_DIM)), full(w1.shape), full((2, 1, HEAD_DIM)),
                  full(w2.shape), full((2, 1, HEAD_DIM)), full((1, HEAD_DIM))],
        out_specs=[pl.BlockSpec((1, N_KV, nch, HEAD_DIM), lambda b: (b, 0, 0, 0)),
                   pl.BlockSpec((1, N_KV, HEAD_DIM, nch), lambda b: (b, 0, 0, 0))],
        out_shape=[jax.ShapeDtypeStruct((nb, N_KV, nch, HEAD_DIM), BF16),
                   jax.ShapeDtypeStruct((nb, N_KV, HEAD_DIM, nch), BF16)],
        compiler_params=_cparams(("parallel",)),
        name="cmp_finish",
    )(pq, pe.reshape(2, 1, -1), w1, b1.reshape(2, 1, -1), w2, b2.reshape(2, 1, -1),
      k_norm.reshape(1, HEAD_DIM))


def _per_head(fn, x, hpg):
    tq = x.shape[1] // hpg
    return jnp.concatenate([fn(x[:, h * tq:(h + 1) * tq]) for h in range(hpg)], axis=1)


def _flash_update(s, m_scr, acc_scr, v_aug):
    m_old = m_scr[...]
    m_new = jnp.maximum(m_old, jnp.max(s, axis=0, keepdims=True))
    p = jnp.exp((s - m_new).astype(BF16))
    acc_scr[...] = jnp.exp(m_old - m_new) * acc_scr[...] + _dot(v_aug, p)
    m_scr[...] = m_new


def _nsa_prompt_kernel(q_ref, gt_ref, kc_ref, vct_ref, ks_ref, vs_ref, kw_ref, vw_ref, o_ref,
                       ks_scr, vst_scr, kw_scr, vwt_scr, qt_scr, score_scr, sel_scr,
                       bias_scr, wbias_scr, m_scr, acc_scr, o_scr, *, tq, t_len, hpg, n_cmp_pad):
    g = pl.program_id(1)
    qi = pl.program_id(2)
    n_chunks = t_len // KEY_CHUNK
    n_slc = t_len // SLC_BLOCK
    win_chunks = WINDOW // KEY_CHUNK
    n_wb = tq // KEY_CHUNK + win_chunks
    nl = hpg * tq

    @pl.when(qi == 0)
    def _():
        for c in range(n_chunks):
            rows = slice(c * KEY_CHUNK, (c + 1) * KEY_CHUNK)
            ks_scr[rows, :] = ks_ref[0, rows, :].astype(BF16)
            kw_scr[rows, :] = kw_ref[0, rows, :].astype(BF16)
            vst_scr[0:HEAD_DIM, rows] = vs_ref[0, rows, :].T.astype(BF16)
            vwt_scr[0:HEAD_DIM, rows] = vw_ref[0, rows, :].T.astype(BF16)
        ones = jnp.ones((ONES_ROWS, t_len), BF16)
        vst_scr[HEAD_DIM:, :] = ones
        vwt_scr[HEAD_DIM:, :] = ones
        kl = lax.broadcasted_iota(jnp.int32, (KEY_CHUNK, tq), 0)
        for c in range(n_wb):
            dlt = (win_chunks - c) * KEY_CHUNK + lax.broadcasted_iota(jnp.int32, (KEY_CHUNK, tq), 1) - kl
            wbias_scr[c] = jnp.where((dlt >= 0) & (dlt < WINDOW), 0.0, NEG)

    for h in range(hpg):
        qt_scr[:, h * tq:(h + 1) * tq] = (
            q_ref[:, h * HEAD_DIM:(h + 1) * HEAD_DIM].astype(F32).T.astype(BF16))
    qpos = qi * tq + lax.broadcasted_iota(jnp.int32, (1, tq), 1)

    def gate_row(branch):
        return jnp.concatenate(
            [gt_ref[pl.ds((g * hpg + h) * N_BRANCH + branch, 1), :] for h in range(hpg)], axis=1)

    cmp_idx = lax.broadcasted_iota(jnp.int32, (n_cmp_pad, tq), 0)
    cmp_mask = cmp_idx * CMP_STRIDE + (CMP_BLOCK - 1) <= qpos
    s = _per_head(lambda x: jnp.where(cmp_mask, x, NEG), _dot(kc_ref[0, 0], qt_scr[...]), hpg)
    e = _per_head(lambda x: jnp.where(cmp_mask, x, 0.0),
                  jnp.exp(s - jnp.max(s, axis=0, keepdims=True)), hpg)
    p = e / jnp.maximum(jnp.sum(e, axis=0, keepdims=True), TINY)
    o_scr[...] = gate_row(0) * _dot(vct_ref[0, 0], p.astype(BF16))
    p_grp = p[:, 0:tq]
    for h in range(1, hpg):
        p_grp = p_grp + p[:, h * tq:(h + 1) * tq]

    ratio = SLC_BLOCK // CMP_STRIDE
    jj = lax.broadcasted_iota(jnp.int32, (n_slc, n_cmp_pad), 0)
    mm = lax.broadcasted_iota(jnp.int32, (n_slc, n_cmp_pad), 1)
    pool = ((mm >= ratio * jj - 1) & (mm <= ratio * jj + ratio - 1)).astype(BF16)
    p_slc = sum(_dot(pool, piece) for piece in _split3(p_grp))
    blk = lax.broadcasted_iota(jnp.int32, (n_slc, tq), 0)
    cur = lax.shift_right_arithmetic(qpos, SLC_SHIFT)
    forced = (blk == 0) | (blk == cur) | (blk == cur - 1)
    valid = blk * SLC_BLOCK <= qpos
    score = jnp.where(forced, FORCE, jnp.where(valid, p_slc, -1.0))
    score_scr[...] = score

    def rank_step(j, rank):
        other = score_scr[pl.ds(j, 1), :]
        beats = (other > score) | ((other == score) & (j < blk))
        return rank + beats.astype(jnp.int32)

    rank = lax.fori_loop(0, n_slc, rank_step, jnp.zeros((n_slc, tq), jnp.int32),
                         unroll=_pick(n_slc, (4, 2, 1)))
    sel_scr[...] = (rank < min(N_SEL, n_slc)).astype(F32)

    def slc_bias(c, _):
        for r in range(SLC_PER_CHUNK):
            sel_row = sel_scr[pl.ds(c * SLC_PER_CHUNK + r, 1), :]
            kpos = c * KEY_CHUNK + r * SLC_BLOCK + lax.broadcasted_iota(jnp.int32, (SLC_BLOCK, tq), 0)
            ok = (sel_row > 0.5) & (kpos <= qpos)
            bias_scr[pl.ds(pl.multiple_of(c * KEY_CHUNK + r * SLC_BLOCK, SLC_BLOCK), SLC_BLOCK), :] = (
                jnp.where(ok, 0.0, NEG))
        return 0

    n_live = (qi * tq + tq + KEY_CHUNK - 1) // KEY_CHUNK
    n_pairs = (n_live + 1) // 2
    lax.fori_loop(0, 2 * n_pairs, slc_bias, 0)

    def scores(k_scr, chunk, bias):
        keys = pl.ds(pl.multiple_of(chunk * KEY_CHUNK, KEY_CHUNK), KEY_CHUNK)
        return _per_head(lambda x: x + bias(keys), _dot(k_scr[keys, :], qt_scr[...]), hpg), keys

    def reset():
        m_scr[...] = jnp.full((1, nl), NEG, F32)
        acc_scr[...] = jnp.zeros((HEAD_DIM + ONES_ROWS, nl), F32)

    def add_branch(branch):
        inv = 1.0 / jnp.maximum(acc_scr[HEAD_DIM:HEAD_DIM + 1, :], TINY)
        o_scr[...] += (gate_row(branch) * inv) * acc_scr[0:HEAD_DIM, :]

    reset()

    def slc_pair(i, _):
        tiles = [scores(ks_scr, 2 * i + u, lambda keys: bias_scr[keys, :]) for u in range(2)]
        for s, keys in tiles:
            _flash_update(s, m_scr, acc_scr, vst_scr[:, keys])
        return 0

    lax.fori_loop(0, n_pairs, slc_pair, 0)
    add_branch(1)

    reset()
    first_key_chunk = qi * (tq // KEY_CHUNK) - win_chunks
    tiles = []
    for c in reversed(range(n_wb)):
        chunk = first_key_chunk + c
        band = jnp.where(chunk >= 0, wbias_scr[c], NEG)
        tiles.append(scores(kw_scr, jnp.maximum(chunk, 0), lambda keys: band))
    for s, keys in tiles:
        _flash_update(s, m_scr, acc_scr, vwt_scr[:, keys])
    add_branch(2)

    for h in range(hpg):
        o_ref[:, h * HEAD_DIM:(h + 1) * HEAD_DIM] = o_scr[:, h * tq:(h + 1) * tq].T.astype(o_ref.dtype)


def _nsa_prompt(q, gates_t, kc, vct, slc_rows, win_rows, bsz, t_len):
    m, qcols = q.shape
    hpg = qcols // (N_KV * HEAD_DIM)
    gw = hpg * HEAD_DIM
    tq = KEY_CHUNK
    assert t_len % (2 * KEY_CHUNK) == 0 and WINDOW % KEY_CHUNK == 0 and tq % KEY_CHUNK == 0
    nq = t_len // tq
    n_cmp_pad = kc.shape[2]
    n_slc = t_len // SLC_BLOCK
    n_wb = tq // KEY_CHUNK + WINDOW // KEY_CHUNK
    kern = functools.partial(_nsa_prompt_kernel, tq=tq, t_len=t_len, hpg=hpg, n_cmp_pad=n_cmp_pad)
    kv_k = lambda b, g, i: (b, 0, g)
    kv_v = lambda b, g, i: (b, 0, N_KV + g)
    return pl.pallas_call(
        kern,
        grid=(bsz, N_KV, nq),
        in_specs=[pl.BlockSpec((tq, gw), lambda b, g, i: (b * nq + i, g)),
                  pl.BlockSpec((gates_t.shape[0], tq), lambda b, g, i: (0, b * nq + i)),
                  pl.BlockSpec((1, 1, n_cmp_pad, HEAD_DIM), lambda b, g, i: (b, g, 0, 0)),
                  pl.BlockSpec((1, 1, HEAD_DIM, n_cmp_pad), lambda b, g, i: (b, g, 0, 0)),
                  pl.BlockSpec((1, t_len, HEAD_DIM), kv_k), pl.BlockSpec((1, t_len, HEAD_DIM), kv_v),
                  pl.BlockSpec((1, t_len, HEAD_DIM), kv_k), pl.BlockSpec((1, t_len, HEAD_DIM), kv_v)],
        out_specs=pl.BlockSpec((tq, gw), lambda b, g, i: (b * nq + i, g)),
        out_shape=jax.ShapeDtypeStruct((m, qcols), BF16),
        scratch_shapes=[pltpu.VMEM((t_len, HEAD_DIM), BF16), pltpu.VMEM((HEAD_DIM + ONES_ROWS, t_len), BF16),
                        pltpu.VMEM((t_len, HEAD_DIM), BF16), pltpu.VMEM((HEAD_DIM + ONES_ROWS, t_len), BF16),
                        pltpu.VMEM((HEAD_DIM, hpg * tq), BF16),
                        pltpu.VMEM((n_slc, tq), F32), pltpu.VMEM((n_slc, tq), F32),
                        pltpu.VMEM((t_len, tq), F32), pltpu.VMEM((n_wb, KEY_CHUNK, tq), F32),
                        pltpu.VMEM((1, hpg * tq), F32),
                        pltpu.VMEM((HEAD_DIM + ONES_ROWS, hpg * tq), F32),
                        pltpu.VMEM((HEAD_DIM, hpg * tq), F32)],
        compiler_params=_cparams(("parallel", "parallel", "arbitrary")),
        name="nsa_prompt",
    )(q, gates_t, kc, vct, slc_rows, slc_rows, win_rows, win_rows)


def _nsa_sample_select_kernel(q_ref, kc_ref, vct_ref, ocmp_ref, idx_ref, *, qpos, n_slc, nsp):
    q = q_ref[0, 0].astype(BF16)
    ncp = kc_ref.shape[2]
    s = _dot_nt(q, kc_ref[0, 0])
    m_idx = lax.broadcasted_iota(jnp.int32, (1, ncp), 1)
    mask = m_idx * CMP_STRIDE + (CMP_BLOCK - 1) <= qpos
    s = jnp.where(mask, s, NEG)
    e = jnp.where(mask, jnp.exp(s - jnp.max(s, axis=-1, keepdims=True)), 0.0)
    p = e / jnp.maximum(jnp.sum(e, axis=-1, keepdims=True), TINY)
    ocmp_ref[0, 0] = _dot_nt(p.astype(BF16), vct_ref[0, 0])

    ratio = SLC_BLOCK // CMP_STRIDE
    p_grp = jnp.broadcast_to(jnp.sum(p, axis=0, keepdims=True), (8, ncp))
    mm = lax.broadcasted_iota(jnp.int32, (ncp, nsp), 0)
    jj = lax.broadcasted_iota(jnp.int32, (ncp, nsp), 1)
    pool_t = ((mm >= ratio * jj - 1) & (mm <= ratio * jj + ratio - 1)).astype(BF16)
    p_slc = sum(_dot(piece, pool_t) for piece in _split3(p_grp))[0:1, :]
    blk = lax.broadcasted_iota(jnp.int32, (1, nsp), 1)
    cur = qpos // SLC_BLOCK
    forced = (blk == 0) | (blk == cur) | (blk == cur - 1)
    score = jnp.where(forced, FORCE, jnp.where(blk * SLC_BLOCK <= qpos, p_slc, -1.0))
    score = jnp.where(blk < n_slc, score, -2.0)

    ii = lax.broadcasted_iota(jnp.int32, (nsp, nsp), 0)
    jx = lax.broadcasted_iota(jnp.int32, (nsp, nsp), 1)
    mine = jnp.broadcast_to(score, (nsp, nsp))
    other = mine.T
    beats = (other > mine) | ((other == mine) & (ii < jx))
    rank = jnp.sum(beats.astype(F32), axis=0, keepdims=True)
    sel = (rank < float(N_SEL)).astype(F32)
    sel_other = jnp.broadcast_to(sel, (nsp, nsp)).T
    pos = jnp.sum(jnp.where(ii < jx, sel_other, 0.0), axis=0, keepdims=True)
    kk = lax.broadcasted_iota(jnp.int32, (N_SEL, nsp), 0).astype(F32)
    jrow = lax.broadcasted_iota(jnp.int32, (N_SEL, nsp), 1).astype(F32)
    hit = (sel > 0.5) & (pos == kk)
    idx = jnp.sum(jnp.where(hit, jrow, 0.0), axis=1, keepdims=True)
    idx_ref[0, 0] = jnp.broadcast_to(idx, (N_SEL, HEAD_DIM)).astype(jnp.int32)


def _nsa_sample_select(q, kc, vct, qpos, n_slc):
    dbs, _, hpg, _ = q.shape
    ncp = kc.shape[2]
    nsp = -(-n_slc // HEAD_DIM) * HEAD_DIM
    assert n_slc >= N_SEL
    blk4 = lambda s2, s3: pl.BlockSpec((1, 1, s2, s3), lambda b, g: (b, g, 0, 0))
    ocmp, idx = pl.pallas_call(
        functools.partial(_nsa_sample_select_kernel, qpos=qpos, n_slc=n_slc, nsp=nsp),
        grid=(dbs, N_KV),
        in_specs=[blk4(hpg, HEAD_DIM), blk4(ncp, HEAD_DIM), blk4(HEAD_DIM, ncp)],
        out_specs=[blk4(hpg, HEAD_DIM), blk4(N_SEL, HEAD_DIM)],
        out_shape=[jax.ShapeDtypeStruct((dbs, N_KV, hpg, HEAD_DIM), F32),
                   jax.ShapeDtypeStruct((dbs, N_KV, N_SEL, HEAD_DIM), jnp.int32)],
        compiler_params=_cparams(("parallel", "parallel")),
        name="nsa_sample_select",
    )(q, kc, vct)
    return ocmp, idx[..., 0]


def _nsa_sample_attend_kernel(pt_ref, idx_ref, q_ref, gate_ref, ocmp_ref, snew_ref, wnew_ref,
                              win_ref, *rest, qpos, n_cached, win_buf):
    blocks, o_ref = rest[:N_SEL], rest[N_SEL]
    b, g = pl.program_id(0), pl.program_id(1)
    slots = 2 * N_KV

    def group_rows(ref, n, col):
        return ref[pl.ds(col * N_KV + g, n, stride=slots), :]

    q = q_ref[0, 0].astype(BF16)
    qf = q.astype(F32)

    def attend(s, mask, v, k_new, v_new):
        s_self = jnp.sum(qf * k_new.astype(BF16).astype(F32), axis=-1, keepdims=True)
        s = jnp.where(mask, s, NEG)
        m = jnp.maximum(jnp.max(s, axis=-1, keepdims=True), s_self)
        e = jnp.where(mask, jnp.exp(s - m), 0.0)
        e_self = jnp.exp(s_self - m)
        l = jnp.sum(e, axis=-1, keepdims=True) + e_self
        acc = _dot(e.astype(BF16), v) + e_self.astype(BF16).astype(F32) * v_new.astype(BF16).astype(F32)
        return acc / jnp.maximum(l, TINY)

    n_keys = N_SEL * SLC_BLOCK
    k_all = jnp.concatenate([group_rows(blk, SLC_BLOCK, 0) for blk in blocks], axis=0).astype(BF16)
    v_all = jnp.concatenate([group_rows(blk, SLC_BLOCK, 1) for blk in blocks], axis=0).astype(BF16)
    lane = lax.broadcasted_iota(jnp.int32, (1, n_keys), 1)
    slot = lax.shift_right_arithmetic(lane, SLC_SHIFT)
    blk_id = jnp.zeros((1, n_keys), jnp.int32)
    for k in range(N_SEL):
        blk_id = jnp.where(slot == k, idx_ref[(b * N_KV + g) * N_SEL + k], blk_id)
    kpos = blk_id * SLC_BLOCK + (lane & (SLC_BLOCK - 1))
    slc_mask = (blk_id < n_cached) & (kpos <= qpos)
    o_slc = attend(_dot_nt(q, k_all), slc_mask, v_all,
                   snew_ref[0, pl.ds(g, 1), :], snew_ref[0, pl.ds(N_KV + g, 1), :])

    wi = lax.broadcasted_iota(jnp.int32, (1, win_buf), 1)
    dlt = win_buf - wi
    win_mask = (dlt >= 0) & (dlt < WINDOW) & (qpos - dlt >= 0)
    o_win = attend(_dot_nt(q, group_rows(win_ref, win_buf, 0).astype(BF16)), win_mask,
                   group_rows(win_ref, win_buf, 1).astype(BF16),
                   wnew_ref[0, pl.ds(g, 1), :], wnew_ref[0, pl.ds(N_KV + g, 1), :])

    gate = gate_ref[0, 0]
    o_ref[0, 0] = gate[:, 0:1] * ocmp_ref[0, 0] + gate[:, 1:2] * o_slc + gate[:, 2:3] * o_win


def _nsa_sample_attend(q, gates, ocmp, idx, page_table, slc_cache, slc_new, win_cache, win_new, qpos):
    dbs, _, hpg, _ = q.shape
    slots = 2 * N_KV
    win_buf = win_cache.shape[0] // (dbs * slots)
    n_cached = page_table.shape[1] * PAGE_SIZE // SLC_BLOCK
    bpp = PAGE_SIZE // SLC_BLOCK

    def blk_map(b, g, pt, ix, *, k):
        blk = jnp.minimum(ix[(b * N_KV + g) * N_SEL + k], n_cached - 1)
        return pt[b, blk // bpp] * bpp + blk % bpp, 0

    blk4 = lambda s2, s3: pl.BlockSpec((1, 1, s2, s3), lambda b, g, pt, ix: (b, g, 0, 0))
    new_spec = pl.BlockSpec((1, slots, HEAD_DIM), lambda b, g, pt, ix: (b, 0, 0))
    win_spec = pl.BlockSpec((win_buf * slots, HEAD_DIM), lambda b, g, pt, ix: (b, 0))
    gathered = [pl.BlockSpec((SLC_BLOCK * slots, HEAD_DIM), functools.partial(blk_map, k=k))
                for k in range(N_SEL)]
    grid_spec = pltpu.PrefetchScalarGridSpec(
        num_scalar_prefetch=2,
        grid=(dbs, N_KV),
        in_specs=[blk4(hpg, HEAD_DIM), blk4(hpg, N_BRANCH), blk4(hpg, HEAD_DIM), new_spec, new_spec,
                  win_spec] + gathered,
        out_specs=blk4(hpg, HEAD_DIM),
    )
    return pl.pallas_call(
        functools.partial(_nsa_sample_attend_kernel, qpos=qpos, n_cached=n_cached, win_buf=win_buf),
        grid_spec=grid_spec,
        out_shape=jax.ShapeDtypeStruct((dbs, N_KV, hpg, HEAD_DIM), F32),
        compiler_params=_cparams(("parallel", "parallel")),
        name="nsa_sample_attend",
    )(page_table, idx.reshape(-1), q, gates, ocmp,
      slc_new.reshape(dbs, slots, HEAD_DIM), win_new.reshape(dbs, slots, HEAD_DIM),
      win_cache, *([slc_cache] * N_SEL))


def kernel(x_prompt, x_sample, c_prompt, c_sample, state_lru_h, state_conv, cache_cmp_kv, cache_slc_kv, cache_win_kv, page_table, ada_w, ada_b, norm1_w, norm2_w, lru_w_in, lru_b_in, lru_conv_w, lru_conv_b, lru_gate_w, lru_gate_b, lru_lambda, lru_w_out, lru_b_out, ffn_w13, ffn_w2, kv_ada_w, kv_ada_b, kv_norm_w, w_kv, k_norm_w, cmp_pe, cmp_w1, cmp_b1, cmp_w2, cmp_b2, w_qg, q_norm_w, w_o):
    bsz, t_len, d = x_prompt.shape
    dbs, dec_seq, _ = x_sample.shape
    depth = ada_w.shape[0]
    n_a = lru_w_in.shape[0]
    assert dec_seq == 1 and depth == 2 and n_a == 1 and w_qg.shape[0] == 1
    lw = lru_w_in.shape[2] // 2
    d_ff = ffn_w2.shape[1]
    n_pages = page_table.shape[1]
    past_len = n_pages * PAGE_SIZE
    qcols = w_o.shape[1]
    hpg = qcols // (N_KV * HEAD_DIM)
    hist = CONV_W - 1
    branch_cols = KV_ROW

    n_c = bsz + dbs
    c_all = jnp.pad(jnp.concatenate([c_prompt, c_sample], axis=0), ((0, (-n_c) % 8), (0, 0)))
    mods = [_matmul(c_all, ada_w, layer=l, bias=ada_b[l], a_silu=True, tn=1024, name="ada")
            for l in range(depth)]
    kv_mod = _matmul(c_all, kv_ada_w, bias=kv_ada_b, a_silu=True, tn=1024, name="kv_ada")

    def split_mod(mat, n, prompt):
        parts = [mat[:, i * d:(i + 1) * d] for i in range(n)]
        if prompt:
            return [p[:bsz].reshape(bsz, 1, d) for p in parts]
        return [p[bsz:n_c].reshape(1, dbs, d) for p in parts]

    w13 = ffn_w13
    w2 = ffn_w2.astype(BF16)
    tk2 = _pick(d_ff, (d_ff // 2,)) if d_ff > 4096 else d_ff
    w_in, w_out, wq, wo = lru_w_in, lru_w_out, w_qg, w_o
    wg_t = w_qg[0, :, qcols:].T
    pe = cmp_pe.reshape(2, -1)

    def mm2(a_p, a_s, w, *, res=(None, None), gate=(None, None), **kw):
        return _matmul(a_p, w, res=res[0], gate=gate[0], small=(a_s, res[1], gate[1]), **kw)

    def norm2(x, w, mod, shift_i, scale_i):
        return (_norm_mod(x[0], w, mod[0][shift_i], mod[0][scale_i], BF16),
                _norm_mod(x[1], w, mod[1][shift_i], mod[1][scale_i], F32))

    def ffn(x, l, mod):
        u = norm2(x, norm2_w[l], mod, 3, 4)
        act = mm2(*u, w13, layer=l, col0=0, col0_b=d_ff, ncols=d_ff, epi="swiglu", tn=256,
                  out_dtype=BF16, name="ffn_up")
        return mm2(*act, w2, layer=l, tk=tk2, epi="residual", res=x, gate=(mod[0][5], mod[1][5]),
                   name="ffn_down")

    mod_l = [(split_mod(m, 6, True), split_mod(m, 6, False)) for m in mods]
    mod_kv = (split_mod(kv_mod, 2, True), split_mod(kv_mod, 2, False))
    wide = dict(tm=min(1024, t_len), tn=512)
    x = (x_prompt.reshape(bsz * t_len, d), x_sample.reshape(dbs, d))

    u = norm2(x, norm1_w[0], mod_l[0], 0, 1)
    gy = mm2(*u, w_in, col0=0, ncols=lw, bias=lru_b_in[0], epi="gelu", out_dtype=BF16, name="lru_in_y",
             **wide)
    xb = mm2(*u, w_in, col0=lw, ncols=lw, bias=lru_b_in[0], name="lru_in_x", **wide)
    lru_args = (lru_conv_w[0], lru_conv_b[0], lru_gate_w[0], lru_gate_b[0], lru_lambda[0])
    hg_p, conv_p, h_p = _lru(xb[0].reshape(bsz, t_len, lw), gy[0].reshape(bsz, t_len, lw),
                             jnp.zeros((bsz, hist, lw), F32), jnp.zeros((bsz, lw), F32), *lru_args)
    hg_s, conv_s, h_s = _lru_step(xb[1], gy[1], jnp.swapaxes(state_conv[0], 0, 1), state_lru_h[0],
                                  *lru_args)
    x = mm2(hg_p.reshape(bsz * t_len, lw), hg_s, w_out, bias=lru_b_out[0], epi="residual", res=x,
            gate=(mod_l[0][0][2], mod_l[0][1][2]), name="lru_out", **wide)
    x = ffn(x, 0, mod_l[0])

    s = norm2(x, kv_norm_w, mod_kv, 0, 1)
    kv_rows = [mm2(*s, w_kv, col0=0, ncols=branch_cols, name="kv_cmp", **wide)]
    for br in range(1, N_BRANCH):
        kv_rows.append(mm2(*s, w_kv, col0=br * branch_cols, ncols=branch_cols, epi="headnorm",
                           hn_w=k_norm_w[br], n_norm_tiles=branch_cols // 2 // wide["tn"],
                           name="kv_norm", **wide))
    (cmp_p, cmp_s), (slc_p, slc_s), (win_p, win_s) = kv_rows

    u = norm2(x, norm1_w[1], mod_l[1], 0, 1)
    q_p, q_s = mm2(*u, wq, col0=0, ncols=qcols, epi="headnorm", hn_w=q_norm_w[0],
                   n_norm_tiles=qcols // wide["tn"], post_scale=HEAD_DIM ** -0.5, out_dtype=BF16,
                   name="nsa_q", **wide)
    gt_p, gt_s = _branch_gates_t(u[0], wg_t), _branch_gates_t(u[1], wg_t)

    pq_p = _cmp_pq(cmp_p.reshape(bsz, t_len, KV_ROW), cmp_w1)
    kc_p, vct_p = _cmp_finish(pq_p, pe, cmp_w1, cmp_b1, cmp_w2, cmp_b2, k_norm_w[0])
    o_p = _nsa_prompt(q_p, gt_p, kc_p, vct_p, slc_p.reshape(bsz, t_len, KV_ROW),
                      win_p.reshape(bsz, t_len, KV_ROW), bsz, t_len)

    pq_s = _cmp_pq(cache_cmp_kv.reshape(-1, HEAD_DIM), cmp_w1, page_table=page_table)
    kc_s, vct_s = _cmp_finish(pq_s, pe, cmp_w1, cmp_b1, cmp_w2, cmp_b2, k_norm_w[0])
    n_slc_s = -(-(past_len + dec_seq) // SLC_BLOCK)
    q_s4 = q_s.reshape(dbs, N_KV, hpg, HEAD_DIM)
    ocmp_s, idx_s = _nsa_sample_select(q_s4, kc_s, vct_s, past_len, n_slc_s)
    gates_s = gt_s.T.reshape(dbs, N_KV, hpg, N_BRANCH)
    o_s = _nsa_sample_attend(q_s4, gates_s, ocmp_s, idx_s, page_table,
                             cache_slc_kv.reshape(-1, HEAD_DIM), slc_s,
                             cache_win_kv.reshape(-1, HEAD_DIM), win_s, past_len)

    x = mm2(o_p, o_s.reshape(dbs, qcols), wo, epi="residual", res=x,
            gate=(mod_l[1][0][2], mod_l[1][1][2]), name="nsa_out", **wide)
    x = ffn(x, 1, mod_l[1])
    y_prompt, y_sample = x[0].reshape(bsz, t_len, d), x[1].reshape(dbs, dec_seq, d)

    kv5 = lambda a, n, t: a.reshape(n, t, 2, N_KV, HEAD_DIM)
    win_buf = cache_win_kv.shape[1]
    win_s5 = kv5(win_s, dbs, 1)
    new_win_s = jnp.concatenate([cache_win_kv, win_s5], axis=1)[:, -win_buf:]
    return (y_prompt, y_sample, h_p[None], h_s[None], conv_p[None],
            jnp.swapaxes(conv_s, 0, 1)[None],
            kv5(cmp_p, bsz, t_len), kv5(cmp_s, dbs, 1), kv5(slc_p, bsz, t_len), kv5(slc_s, dbs, 1),
            kv5(win_p, bsz, t_len)[:, -WINDOW:], new_win_s)
```

```python
import functools

import jax
import jax.numpy as jnp
from jax import lax
from jax.experimental import pallas as pl
from jax.experimental.pallas import tpu as pltpu

F32 = jnp.float32
BF16 = jnp.bfloat16

HEAD_DIM = 128
N_KV = 4
N_BRANCH = 3
CMP_BLOCK = 32
CMP_STRIDE = 16
SLC_BLOCK = 64
N_SEL = 16
WINDOW = 512
CONV_W = 4
LRU_C = 8.0
PAGE_SIZE = 128
EPS = 1e-6
NEG = -1e30
FORCE = 1e4
TINY = 1e-30

KV_ROW = 2 * N_KV * HEAD_DIM
SLC_SHIFT = SLC_BLOCK.bit_length() - 1
SLC_PER_CHUNK = 4
KEY_CHUNK = SLC_PER_CHUNK * SLC_BLOCK
BF16_ROWS = 16
ONES_ROWS = BF16_ROWS
VMEM_LIMIT = 56 * 1024 * 1024


def _cparams(sem, vmem=None):
    return pltpu.CompilerParams(dimension_semantics=sem, vmem_limit_bytes=vmem or VMEM_LIMIT)


def _pick(n, cands):
    for c in cands:
        if n % c == 0:
            return c
    return n


def _dot(a, b):
    return jnp.dot(a, b, preferred_element_type=F32)


def _dot_nt(a, b):
    return lax.dot_general(a, b, (((1,), (1,)), ((), ())), preferred_element_type=F32)


def _sigmoid(x):
    return 0.5 * (jnp.tanh(0.5 * x) + 1.0)


def _split3(x):
    hi = x.astype(BF16)
    r1 = x - hi.astype(F32)
    mid = r1.astype(BF16)
    lo = (r1 - mid.astype(F32)).astype(BF16)
    return hi, mid, lo


def _norm_mod_kernel(x_ref, w_ref, shift_ref, scale_ref, o_ref):
    x = x_ref[...]
    y = x * lax.rsqrt(jnp.mean(x * x, axis=-1, keepdims=True) + EPS)
    y = y * w_ref[...]
    o_ref[...] = (y * (1.0 + scale_ref[0]) + shift_ref[0]).astype(o_ref.dtype)


def _norm_mod(x, w, shift, scale, out_dtype):
    m, d = x.shape
    nb, rb, _ = shift.shape
    rows_per_nb = m // nb
    tr = rows_per_nb if rb > 1 else _pick(rows_per_nb, (512, 256, 128, 64, 32, 16, 8))
    tiles_per_nb = rows_per_nb // tr
    mod_spec = pl.BlockSpec((1, rb, d), lambda i: (i // tiles_per_nb, 0, 0))
    return pl.pallas_call(
        _norm_mod_kernel,
        grid=(m // tr,),
        in_specs=[pl.BlockSpec((tr, d), lambda i: (i, 0)),
                  pl.BlockSpec((1, d), lambda i: (0, 0)), mod_spec, mod_spec],
        out_specs=pl.BlockSpec((tr, d), lambda i: (i, 0)),
        out_shape=jax.ShapeDtypeStruct((m, d), out_dtype),
        compiler_params=_cparams(("parallel",)),
        name="norm_mod",
    )(x, w.reshape(1, d), shift, scale)


def _head_rms(acc, w, post_scale):
    outs = []
    for h in range(acc.shape[1] // HEAD_DIM):
        sl = acc[:, h * HEAD_DIM:(h + 1) * HEAD_DIM]
        y = sl * lax.rsqrt(jnp.mean(sl * sl, axis=-1, keepdims=True) + EPS)
        y = y * w
        if post_scale is not None:
            y = y * post_scale
        outs.append(y)
    return jnp.concatenate(outs, axis=1)


def _mm_kernel(*refs, nk, ni, a_silu, has_bias, epi, n_norm_tiles, post_scale, has_small):
    it = iter(refs)
    a_ref, w_ref = next(it), next(it)
    w2_ref = next(it) if epi == "swiglu" else None
    b_ref = next(it) if has_bias else None
    res_ref, gate_ref = (next(it), next(it)) if epi == "residual" else (None, None)
    hn_ref = next(it) if epi == "headnorm" else None
    a2_ref = next(it) if has_small else None
    res2_ref, gate2_ref = (next(it), next(it)) if has_small and epi == "residual" else (None, None)
    o_ref = next(it)
    o2_ref = next(it) if has_small else None
    acc_ref = next(it) if nk > 1 else None
    acc2_ref = next(it) if nk > 1 and has_small else None
    stack_ref = next(it) if nk == 1 and has_small else None

    def dots(a):
        second = _dot(a, w2_ref[...].astype(BF16)) if epi == "swiglu" else None
        return _dot(a, w_ref[...].astype(BF16)), second

    def finish(acc, second, res, gate, out):
        if has_bias:
            acc = acc + b_ref[...]
        if epi == "gelu":
            acc = jax.nn.gelu(acc)
        elif epi == "swiglu":
            acc = acc * _sigmoid(acc) * second
        elif epi == "residual":
            acc = res[...] + gate[0] * acc
        elif epi == "headnorm":
            normed = _head_rms(acc, hn_ref[...], post_scale)
            acc = jnp.where(pl.program_id(1) < n_norm_tiles, normed, acc)
        out[...] = acc.astype(out.dtype)

    def run(lhs_ref, res, gate, out, acc_scr):
        a = lhs_ref[...]
        if a_silu:
            a = a * _sigmoid(a)
        part, second = dots(a.astype(BF16))
        if nk == 1:
            finish(part, second, res, gate, out)
        else:
            k = pl.program_id(2)

            @pl.when(k == 0)
            def _():
                acc_scr[...] = part

            @pl.when((k > 0) & (k < nk - 1))
            def _():
                acc_scr[...] += part

            @pl.when(k == nk - 1)
            def _():
                finish(acc_scr[...] + part, None, res, gate, out)

    if not has_small:
        run(a_ref, res_ref, gate_ref, o_ref, acc_ref)
        return

    last = pl.program_id(0) == ni - 1

    @pl.when(jnp.logical_not(last))
    def _():
        run(a_ref, res_ref, gate_ref, o_ref, acc_ref)
        o2_ref[...] = jnp.zeros(o2_ref.shape, o2_ref.dtype)

    if nk > 1:
        @pl.when(last)
        def _():
            run(a_ref, res_ref, gate_ref, o_ref, acc_ref)
            run(a2_ref, res2_ref, gate2_ref, o2_ref, acc2_ref)
        return

    tm, m2 = a_ref.shape[0], a2_ref.shape[0]

    @pl.when(last & (pl.program_id(1) == 0))
    def _():
        stack_ref[0:tm, :] = a_ref[...]
        pad = jnp.zeros((stack_ref.shape[0] - tm - m2, a2_ref.shape[1]), F32)
        stack_ref[tm:, :] = jnp.concatenate([a2_ref[...].astype(F32), pad], axis=0).astype(BF16)

    @pl.when(last)
    def _():
        part, second = dots(stack_ref[...])
        rows = lambda x, lo, hi: None if x is None else x[lo:hi]
        finish(part[0:tm], rows(second, 0, tm), res_ref, gate_ref, o_ref)
        finish(part[tm:tm + m2], rows(second, tm, tm + m2), res2_ref, gate2_ref, o2_ref)


def _matmul(a, w, *, layer=0, col0=0, ncols=None, col0_b=None, bias=None, epi="none", res=None,
            gate=None, hn_w=None, n_norm_tiles=0, post_scale=None, a_silu=False, out_dtype=F32,
            small=None, tm=None, tn=None, tk=None, name="matmul"):
    m, kdim = a.shape
    ncols = ncols if ncols is not None else w.shape[-1]
    tm = tm or min(m, 1024, m // gate.shape[0] if gate is not None and gate.shape[1] == 1 else m)
    tn = tn or _pick(ncols, (512, 256, 128))
    tk = tk or kdim
    nk = kdim // tk
    assert w.shape[-2] == kdim
    assert m % tm == 0 and ncols % tn == 0 and kdim % tk == 0 and col0 % tn == 0
    assert epi != "swiglu" or (nk == 1 and col0_b % tn == 0)
    jo = col0 // tn
    spec = pl.BlockSpec

    def w_spec(j0):
        if w.ndim == 3:
            return spec((None, tk, tn), lambda i, j, k: (layer, k, j + j0))
        return spec((tk, tn), lambda i, j, k: (k, j + j0))

    in_specs = [spec((tm, tk), lambda i, j, k: (i, k)), w_spec(jo)]
    args = [a, w]
    if epi == "swiglu":
        in_specs.append(w_spec(col0_b // tn))
        args.append(w)
    if bias is not None:
        in_specs.append(spec((1, tn), lambda i, j, k: (0, j + jo)))
        args.append(bias.reshape(1, -1))
    if epi == "residual":
        nb, rb, _ = gate.shape
        tiles_per_nb = (m // nb) // tm
        assert (m // nb) % tm == 0
        assert tiles_per_nb >= 1 and (rb == 1 or rb == tm)
        in_specs += [spec((tm, tn), lambda i, j, k: (i, j)),
                     spec((1, rb, tn), lambda i, j, k: (i // tiles_per_nb, 0, j))]
        args += [res, gate]
    if epi == "headnorm":
        in_specs.append(spec((1, HEAD_DIM), lambda i, j, k: (0, 0)))
        args.append(hn_w.reshape(1, HEAD_DIM))
    ni, nj = m // tm, ncols // tn
    out_specs = [spec((tm, tn), lambda i, j, k: (i, j))]
    out_shape = [jax.ShapeDtypeStruct((m, ncols), out_dtype)]
    scratch = [pltpu.VMEM((tm, tn), F32)] if nk > 1 else []
    if small is not None:
        a2, res2, gate2 = small
        m2 = a2.shape[0]
        in_specs.append(spec((m2, tk), lambda i, j, k: (0, k)))
        args.append(a2)
        if epi == "residual":
            in_specs += [spec((m2, tn), lambda i, j, k: (0, j)),
                         spec((1, m2, tn), lambda i, j, k: (0, 0, j))]
            args += [res2, gate2]
        out_specs.append(spec((m2, tn), lambda i, j, k: (0, jnp.where(i == ni - 1, j, nj))))
        out_shape.append(jax.ShapeDtypeStruct((m2, ncols + tn), F32))
        assert not a_silu and a.dtype == BF16 and m2 <= BF16_ROWS
        if nk > 1:
            scratch.append(pltpu.VMEM((m2, tn), F32))
        else:
            scratch.append(pltpu.VMEM((tm + BF16_ROWS, tk), BF16))
    kern = functools.partial(_mm_kernel, nk=nk, ni=ni, a_silu=a_silu, has_bias=bias is not None,
                             epi=epi, n_norm_tiles=n_norm_tiles, post_scale=post_scale,
                             has_small=small is not None)
    outs = pl.pallas_call(
        kern,
        grid=(ni, nj, nk),
        in_specs=in_specs,
        out_specs=out_specs,
        out_shape=out_shape,
        scratch_shapes=scratch,
        compiler_params=_cparams(("arbitrary",) * 3 if small is not None
                                 else ("parallel", "parallel", "arbitrary")),
        name=name,
    )(*args)
    return (outs[0], outs[1][:, :ncols]) if small is not None else outs[0]


def _gate_kernel(wg_ref, u_ref, o_ref):
    o_ref[...] = _sigmoid(_dot_nt(wg_ref[...].astype(BF16), u_ref[...].astype(BF16)))


def _branch_gates_t(u, wg_t):
    m, kdim = u.shape
    r = wg_t.shape[0]
    tm = _pick(m, (1024, 512, 256, 128))
    return pl.pallas_call(
        _gate_kernel,
        grid=(m // tm,),
        in_specs=[pl.BlockSpec((r, kdim), lambda i: (0, 0)),
                  pl.BlockSpec((tm, kdim), lambda i: (i, 0))],
        out_specs=pl.BlockSpec((r, tm), lambda i: (0, i)),
        out_shape=jax.ShapeDtypeStruct((r, m), F32),
        compiler_params=_cparams(("parallel",)),
        name="branch_gates",
    )(wg_t, u)


def _lru_coeffs(xb, gw_r, gw_i, gb, lam):
    xb16 = xb.astype(BF16)
    r = _sigmoid(_dot(xb16, gw_r) + gb[0:1, :])
    gi = _sigmoid(_dot(xb16, gw_i) + gb[1:2, :])
    z = -lam
    softplus = jnp.maximum(z, 0.0) + jnp.log1p(jnp.exp(-jnp.abs(z)))
    log_a = -LRU_C * r * softplus
    a = jnp.exp(log_a)
    y = -jnp.tanh(log_a) * (1.0 + a * a)
    return a, jnp.where(y > 0.0, y * lax.rsqrt(y), 0.0) * gi * xb


def _lru_step_kernel(x_ref, gy_ref, cprev_ref, hprev_ref, cw_ref, cb_ref, gw_ref, gb_ref, lam_ref,
                     hg_ref, nconv_ref, h_ref, *, bw, nbp):
    hist = CONV_W - 1
    x = x_ref[...]
    conv = cprev_ref[0] * cw_ref[0:1, :]
    for k in range(1, hist):
        conv = conv + cprev_ref[k] * cw_ref[k:k + 1, :]
    xc = cb_ref[...] + (conv + x * cw_ref[hist:hist + 1, :])
    for blk in range(nbp):
        cols = slice(blk * bw, (blk + 1) * bw)
        a, b = _lru_coeffs(xc[:, cols], gw_ref[0, blk].astype(BF16), gw_ref[1, blk].astype(BF16),
                           gb_ref[:, cols], lam_ref[:, cols])
        h = a * hprev_ref[:, cols] + b
        h_ref[:, cols] = h
        hg_ref[:, cols] = (h * gy_ref[:, cols].astype(F32)).astype(hg_ref.dtype)
    for k in range(hist - 1):
        nconv_ref[k] = cprev_ref[k + 1]
    nconv_ref[hist - 1] = x


def _lru_step(x, gy, conv_prev, h_prev, conv_w, conv_b, gate_w, gate_b, lam):
    b, w = x.shape
    nblk, bw = gate_w.shape[1], gate_w.shape[2]
    nbp = _pick(nblk, (4, 2, 1))
    cw = nbp * bw
    hist = CONV_W - 1
    rows = pl.BlockSpec((b, cw), lambda c: (0, c))
    hrows = pl.BlockSpec((hist, b, cw), lambda c: (0, 0, c))
    chan = lambda r: pl.BlockSpec((r, cw), lambda c: (0, c))
    return pl.pallas_call(
        functools.partial(_lru_step_kernel, bw=bw, nbp=nbp),
        grid=(w // cw,),
        in_specs=[rows, rows, hrows, rows, chan(CONV_W), chan(1),
                  pl.BlockSpec((2, nbp, bw, bw), lambda c: (0, c, 0, 0)), chan(2), chan(1)],
        out_specs=[rows, hrows, rows],
        out_shape=[jax.ShapeDtypeStruct((b, w), gy.dtype), jax.ShapeDtypeStruct((hist, b, w), F32),
                   jax.ShapeDtypeStruct((b, w), F32)],
        compiler_params=_cparams(("parallel",)),
        name="rglru_step",
    )(x, gy, conv_prev, h_prev, conv_w, conv_b.reshape(1, w), gate_w, gate_b, lam.reshape(1, w))


def _lru_kernel(x_ref, gy_ref, cprev_ref, hprev_ref, cw_ref, cb_ref, gw_ref, gb_ref, lam_ref,
                hg_ref, nconv_ref, hlast_ref,
                xbuf, gw_scr, a_scr, b_scr, hs_scr, h_scr, *, tt, nt, bw, nbp):
    t = pl.program_id(2)
    hist = CONV_W - 1
    base = 8

    @pl.when(t == 0)
    def _():
        xbuf[base - hist:base, :] = cprev_ref[0]
        h_scr[...] = hprev_ref[0]
        gw_scr[...] = gw_ref[...].astype(BF16)

    @pl.when(t > 0)
    def _():
        xbuf[base - hist:base, :] = xbuf[base + tt - hist:base + tt, :]

    xbuf[base:base + tt, :] = x_ref[0]
    conv = xbuf[base - hist:base - hist + tt, :] * cw_ref[0:1, :]
    for k in range(1, CONV_W):
        conv = conv + xbuf[base - hist + k:base - hist + k + tt, :] * cw_ref[k:k + 1, :]
    xc = cb_ref[...] + conv

    for blk in range(nbp):
        cols = slice(blk * bw, (blk + 1) * bw)
        a_scr[:, cols], b_scr[:, cols] = _lru_coeffs(
            xc[:, cols], gw_scr[0, blk], gw_scr[1, blk], gb_ref[:, cols], lam_ref[:, cols])

    def step(i, h):
        h = a_scr[pl.ds(i, 1), :] * h + b_scr[pl.ds(i, 1), :]
        hs_scr[pl.ds(i, 1), :] = h
        return h

    h_scr[...] = lax.fori_loop(0, tt, step, h_scr[...], unroll=min(tt, 8))
    hg_ref[0] = (hs_scr[...] * gy_ref[0].astype(F32)).astype(hg_ref.dtype)

    @pl.when(t == nt - 1)
    def _():
        nconv_ref[0] = xbuf[base + tt - hist:base + tt, :]
        hlast_ref[0] = h_scr[...]


def _lru(x, gy, conv_prev, h_prev, conv_w, conv_b, gate_w, gate_b, lam):
    b, t, w = x.shape
    nblk, bw = gate_w.shape[1], gate_w.shape[2]
    nbp = _pick(nblk, (8, 4, 2, 1))
    cw = nbp * bw
    tt = _pick(t, (512, 256, 128, 64, 32, 16, 8))
    nt = t // tt
    hist = CONV_W - 1
    kern = functools.partial(_lru_kernel, tt=tt, nt=nt, bw=bw, nbp=nbp)
    row = lambda bi, c, ti: (bi, ti, c)
    fixed = lambda bi, c, ti: (bi, 0, c)
    chan = lambda bi, c, ti: (0, c)
    hg, nconv, hlast = pl.pallas_call(
        kern,
        grid=(b, w // cw, nt),
        in_specs=[pl.BlockSpec((1, tt, cw), row), pl.BlockSpec((1, tt, cw), row),
                  pl.BlockSpec((1, hist, cw), fixed), pl.BlockSpec((1, 1, cw), fixed),
                  pl.BlockSpec((CONV_W, cw), chan), pl.BlockSpec((1, cw), chan),
                  pl.BlockSpec((2, nbp, bw, bw), lambda bi, c, ti: (0, c, 0, 0)),
                  pl.BlockSpec((2, cw), chan), pl.BlockSpec((1, cw), chan)],
        out_specs=[pl.BlockSpec((1, tt, cw), row), pl.BlockSpec((1, hist, cw), fixed),
                   pl.BlockSpec((1, 1, cw), fixed)],
        out_shape=[jax.ShapeDtypeStruct((b, t, w), BF16),
                   jax.ShapeDtypeStruct((b, hist, w), F32),
                   jax.ShapeDtypeStruct((b, 1, w), F32)],
        scratch_shapes=[pltpu.VMEM((8 + tt, cw), F32), pltpu.VMEM((2, nbp, bw, bw), BF16),
                        pltpu.VMEM((tt, cw), F32), pltpu.VMEM((tt, cw), F32),
                        pltpu.VMEM((tt, cw), F32), pltpu.VMEM((1, cw), F32)],
        compiler_params=_cparams(("parallel", "parallel", "arbitrary")),
        name="rglru",
    )(x, gy, conv_prev, h_prev.reshape(b, 1, w), conv_w, conv_b.reshape(1, w), gate_w, gate_b,
      lam.reshape(1, w))
    return hg, nconv, hlast.reshape(b, w)


def _cmp_w_pq(w1_ref, kv):
    half = CMP_STRIDE * HEAD_DIM
    return jnp.concatenate([w1_ref[kv, :half, :], w1_ref[kv, half:, :]], axis=1).astype(BF16)


def _cmp_pq_kernel(x_ref, w1_ref, o_ref, slab):
    nc = slab.shape[0] // CMP_STRIDE
    for kv in range(2):
        rows = []
        for g in range(N_KV):
            off = (kv * N_KV + g) * HEAD_DIM
            slab[...] = x_ref[0, :, off:off + HEAD_DIM]
            rows.append(jnp.concatenate(
                [slab[pl.ds(r, nc, stride=CMP_STRIDE), :].astype(BF16) for r in range(CMP_STRIDE)], axis=1))
        pq = _dot(jnp.concatenate(rows, axis=0), _cmp_w_pq(w1_ref, kv))
        for g in range(N_KV):
            c0 = (kv * N_KV + g) * 2 * HEAD_DIM
            o_ref[0, :, c0:c0 + 2 * HEAD_DIM] = pq[g * nc:(g + 1) * nc, :]


def _cmp_pq_paged_kernel(pt_ref, *refs, n_src):
    src, (w1_ref, o_ref, p_scr, q_scr) = refs[:n_src], refs[n_src:]
    slots = 2 * N_KV
    cpp = PAGE_SIZE // CMP_STRIDE
    chunk_rows = CMP_STRIDE * slots
    lhs = jnp.concatenate(
        [jnp.concatenate([s[c * chunk_rows + r * slots:c * chunk_rows + (r + 1) * slots, :]
                          for r in range(CMP_STRIDE)], axis=1)
         for s in src for c in range(cpp)], axis=0).astype(BF16)
    both = _dot(lhs, jnp.concatenate([_cmp_w_pq(w1_ref, 0), _cmp_w_pq(w1_ref, 1)], axis=1))
    is_v = (lax.broadcasted_iota(jnp.int32, (both.shape[0], 1), 0) & (slots - 1)) >= N_KV
    pq = jnp.where(is_v, both[:, 2 * HEAD_DIM:], both[:, :2 * HEAD_DIM])
    p_scr[...] = pq[:, :HEAD_DIM]
    q_scr[...] = pq[:, HEAD_DIM:]
    nc = n_src * cpp
    for slot in range(slots):
        c0 = slot * 2 * HEAD_DIM
        o_ref[0, :, c0:c0 + HEAD_DIM] = p_scr[pl.ds(slot, nc, stride=slots), :]
        o_ref[0, :, c0 + HEAD_DIM:c0 + 2 * HEAD_DIM] = q_scr[pl.ds(slot, nc, stride=slots), :]


def _cmp_pq(rows, w1, *, page_table=None, pages_per_step=32):
    out_cols = 2 * N_KV * 2 * HEAD_DIM
    if page_table is None:
        nb, t_len, _ = rows.shape
        nch = t_len // CMP_STRIDE
        tc = _pick(nch, (128, 64, 32, 16, 8))
        return pl.pallas_call(
            _cmp_pq_kernel,
            grid=(nb, nch // tc),
            in_specs=[pl.BlockSpec((1, tc * CMP_STRIDE, KV_ROW), lambda b, c: (b, c, 0)),
                      pl.BlockSpec(w1.shape, lambda b, c: (0, 0, 0))],
            out_specs=pl.BlockSpec((1, tc, out_cols), lambda b, c: (b, c, 0)),
            out_shape=jax.ShapeDtypeStruct((nb, nch, out_cols), F32),
            scratch_shapes=[pltpu.VMEM((tc * CMP_STRIDE, HEAD_DIM), F32)],
            compiler_params=_cparams(("parallel", "parallel")),
            name="cmp_pq",
        )(rows, w1)
    nb, n_pages = page_table.shape
    page_rows = PAGE_SIZE * 2 * N_KV
    cpp = PAGE_SIZE // CMP_STRIDE
    pps = _pick(n_pages, (pages_per_step, 8, 4, 2, 1))
    src_specs = [pl.BlockSpec((page_rows, HEAD_DIM),
                              functools.partial(lambda b, s, pt, p: (pt[b, s * pps + p], 0), p=p))
                 for p in range(pps)]
    grid_spec = pltpu.PrefetchScalarGridSpec(
        num_scalar_prefetch=1,
        grid=(nb, n_pages // pps),
        in_specs=src_specs + [pl.BlockSpec(w1.shape, lambda b, s, pt: (0, 0, 0))],
        out_specs=pl.BlockSpec((1, pps * cpp, out_cols), lambda b, s, pt: (b, s, 0)),
        scratch_shapes=[pltpu.VMEM((pps * cpp * 2 * N_KV, HEAD_DIM), F32)] * 2,
    )
    return pl.pallas_call(
        functools.partial(_cmp_pq_paged_kernel, n_src=pps),
        grid_spec=grid_spec,
        out_shape=jax.ShapeDtypeStruct((nb, n_pages * cpp, out_cols), F32),
        compiler_params=_cparams(("parallel", "parallel")),
        name="cmp_pq_paged",
    )(page_table, *([rows] * pps), w1)


def _cmp_finish_kernel(pq_ref, pe_ref, w1_ref, b1_ref, w2_ref, b2_ref, kn_ref, kc_ref, vct_ref, *, nch):
    is_block = lax.broadcasted_iota(jnp.int32, (nch, 1), 0) < nch - 1
    for kv in range(2):
        pe8 = jnp.broadcast_to(pe_ref[kv], (8, pe_ref.shape[2])).astype(BF16)
        const = _dot(pe8, w1_ref[kv].astype(BF16))[0:1, :] + b1_ref[kv]
        w2 = w2_ref[kv].astype(BF16)
        for g in range(N_KV):
            c0 = (kv * N_KV + g) * 2 * HEAD_DIM
            nxt = jnp.concatenate([pq_ref[0, 1:nch, c0 + HEAD_DIM:c0 + 2 * HEAD_DIM],
                                   jnp.zeros((1, HEAD_DIM), F32)], axis=0)
            hid = pq_ref[0, :, c0:c0 + HEAD_DIM] + nxt + const
            out = _dot(jax.nn.gelu(hid).astype(BF16), w2) + b2_ref[kv]
            if kv == 0:
                out = out * lax.rsqrt(jnp.mean(out * out, axis=-1, keepdims=True) + EPS) * kn_ref[...]
            out = jnp.where(is_block, out, 0.0)
            if kv == 0:
                kc_ref[0, g] = out.astype(kc_ref.dtype)
            else:
                vct_ref[0, g] = out.T.astype(vct_ref.dtype)


def _cmp_finish(pq, pe, w1, b1, w2, b2, k_norm):
    nb, nch, cols = pq.shape
    full = lambda shape: pl.BlockSpec(shape, lambda b: (0,) * len(shape))
    return pl.pallas_call(
        functools.partial(_cmp_finish_kernel, nch=nch),
        grid=(nb,),
        in_specs=[pl.BlockSpec((1, nch, cols), lambda b: (b, 0, 0)),
                  full((2, 1, CMP_BLOCK * HEAD_DIM)), full(w1.shape), full((2, 1, HEAD_DIM)),
                  full(w2.shape), full((2, 1, HEAD_DIM)), full((1, HEAD_DIM))],
        out_specs=[pl.BlockSpec((1, N_KV, nch, HEAD_DIM), lambda b: (b, 0, 0, 0)),
                   pl.BlockSpec((1, N_KV, HEAD_DIM, nch), lambda b: (b, 0, 0, 0))],
        out_shape=[jax.ShapeDtypeStruct((nb, N_KV, nch, HEAD_DIM), BF16),
                   jax.ShapeDtypeStruct((nb, N_KV, HEAD_DIM, nch), BF16)],
        compiler_params=_cparams(("parallel",)),
        name="cmp_finish",
    )(pq, pe.reshape(2, 1, -1), w1, b1.reshape(2, 1, -1), w2, b2.reshape(2, 1, -1),
      k_norm.reshape(1, HEAD_DIM))


def _per_head(fn, x, hpg):
    tq = x.shape[1] // hpg
    return jnp.concatenate([fn(x[:, h * tq:(h + 1) * tq]) for h in range(hpg)], axis=1)


def _flash_update(s, m_scr, acc_scr, v_aug):
    m_old = m_scr[...]
    m_new = jnp.maximum(m_old, jnp.max(s, axis=0, keepdims=True))
    p = jnp.exp((s - m_new).astype(BF16))
    acc_scr[...] = jnp.exp(m_old - m_new) * acc_scr[...] + _dot(v_aug, p)
    m_scr[...] = m_new


def _nsa_prompt_kernel(q_ref, gt_ref, kc_ref, vct_ref, ks_ref, vs_ref, kw_ref, vw_ref, o_ref,
                       ks_scr, vst_scr, kw_scr, vwt_scr, qt_scr, score_scr, sel_scr,
                       bias_scr, wbias_scr, m_scr, acc_scr, o_scr, *, tq, t_len, hpg, n_cmp_pad):
    g = pl.program_id(1)
    qi = pl.program_id(2)
    n_chunks = t_len // KEY_CHUNK
    n_slc = t_len // SLC_BLOCK
    win_chunks = WINDOW // KEY_CHUNK
    n_wb = tq // KEY_CHUNK + win_chunks
    nl = hpg * tq

    @pl.when(qi == 0)
    def _():
        for c in range(n_chunks):
            rows = slice(c * KEY_CHUNK, (c + 1) * KEY_CHUNK)
            ks_scr[rows, :] = ks_ref[0, rows, :].astype(BF16)
            kw_scr[rows, :] = kw_ref[0, rows, :].astype(BF16)
            vst_scr[0:HEAD_DIM, rows] = vs_ref[0, rows, :].T.astype(BF16)
            vwt_scr[0:HEAD_DIM, rows] = vw_ref[0, rows, :].T.astype(BF16)
        ones = jnp.ones((ONES_ROWS, t_len), BF16)
        vst_scr[HEAD_DIM:, :] = ones
        vwt_scr[HEAD_DIM:, :] = ones
        kl = lax.broadcasted_iota(jnp.int32, (KEY_CHUNK, tq), 0)
        for c in range(n_wb):
            dlt = (win_chunks - c) * KEY_CHUNK + lax.broadcasted_iota(jnp.int32, (KEY_CHUNK, tq), 1) - kl
            wbias_scr[c] = jnp.where((dlt >= 0) & (dlt < WINDOW), 0.0, NEG)

    for h in range(hpg):
        qt_scr[:, h * tq:(h + 1) * tq] = (
            q_ref[:, h * HEAD_DIM:(h + 1) * HEAD_DIM].astype(F32).T.astype(BF16))
    qpos = qi * tq + lax.broadcasted_iota(jnp.int32, (1, tq), 1)

    def gate_row(branch):
        return jnp.concatenate(
            [gt_ref[pl.ds((g * hpg + h) * N_BRANCH + branch, 1), :] for h in range(hpg)], axis=1)

    cmp_idx = lax.broadcasted_iota(jnp.int32, (n_cmp_pad, tq), 0)
    cmp_mask = cmp_idx * CMP_STRIDE + (CMP_BLOCK - 1) <= qpos
    s = _per_head(lambda x: jnp.where(cmp_mask, x, NEG), _dot(kc_ref[0, 0], qt_scr[...]), hpg)
    e = _per_head(lambda x: jnp.where(cmp_mask, x, 0.0),
                  jnp.exp(s - jnp.max(s, axis=0, keepdims=True)), hpg)
    p = e / jnp.maximum(jnp.sum(e, axis=0, keepdims=True), TINY)
    o_scr[...] = gate_row(0) * _dot(vct_ref[0, 0], p.astype(BF16))
    p_grp = p[:, 0:tq]
    for h in range(1, hpg):
        p_grp = p_grp + p[:, h * tq:(h + 1) * tq]

    ratio = SLC_BLOCK // CMP_STRIDE
    jj = lax.broadcasted_iota(jnp.int32, (n_slc, n_cmp_pad), 0)
    mm = lax.broadcasted_iota(jnp.int32, (n_slc, n_cmp_pad), 1)
    pool = ((mm >= ratio * jj - 1) & (mm <= ratio * jj + ratio - 1)).astype(BF16)
    p_slc = sum(_dot(pool, piece) for piece in _split3(p_grp))
    blk = lax.broadcasted_iota(jnp.int32, (n_slc, tq), 0)
    cur = lax.shift_right_arithmetic(qpos, SLC_SHIFT)
    forced = (blk == 0) | (blk == cur) | (blk == cur - 1)
    valid = blk * SLC_BLOCK <= qpos
    score = jnp.where(forced, FORCE, jnp.where(valid, p_slc, -1.0))
    score_scr[...] = score

    def rank_step(j, rank):
        other = score_scr[pl.ds(j, 1), :]
        beats = (other > score) | ((other == score) & (j < blk))
        return rank + beats.astype(jnp.int32)

    rank = lax.fori_loop(0, n_slc, rank_step, jnp.zeros((n_slc, tq), jnp.int32),
                         unroll=_pick(n_slc, (4, 2, 1)))
    sel_scr[...] = (rank < min(N_SEL, n_slc)).astype(F32)

    def slc_bias(c, _):
        for r in range(SLC_PER_CHUNK):
            sel_row = sel_scr[pl.ds(c * SLC_PER_CHUNK + r, 1), :]
            kpos = c * KEY_CHUNK + r * SLC_BLOCK + lax.broadcasted_iota(jnp.int32, (SLC_BLOCK, tq), 0)
            ok = (sel_row > 0.5) & (kpos <= qpos)
            bias_scr[pl.ds(pl.multiple_of(c * KEY_CHUNK + r * SLC_BLOCK, SLC_BLOCK), SLC_BLOCK), :] = (
                jnp.where(ok, 0.0, NEG))
        return 0

    n_live = (qi * tq + tq + KEY_CHUNK - 1) // KEY_CHUNK
    n_pairs = (n_live + 1) // 2
    lax.fori_loop(0, 2 * n_pairs, slc_bias, 0)

    def scores(k_scr, chunk, bias):
        keys = pl.ds(pl.multiple_of(chunk * KEY_CHUNK, KEY_CHUNK), KEY_CHUNK)
        return _per_head(lambda x: x + bias(keys), _dot(k_scr[keys, :], qt_scr[...]), hpg), keys

    def reset():
        m_scr[...] = jnp.full((1, nl), NEG, F32)
        acc_scr[...] = jnp.zeros((HEAD_DIM + ONES_ROWS, nl), F32)

    def add_branch(branch):
        inv = 1.0 / jnp.maximum(acc_scr[HEAD_DIM:HEAD_DIM + 1, :], TINY)
        o_scr[...] += (gate_row(branch) * inv) * acc_scr[0:HEAD_DIM, :]

    reset()

    def slc_pair(i, _):
        tiles = [scores(ks_scr, 2 * i + u, lambda keys: bias_scr[keys, :]) for u in range(2)]
        for s, keys in tiles:
            _flash_update(s, m_scr, acc_scr, vst_scr[:, keys])
        return 0

    lax.fori_loop(0, n_pairs, slc_pair, 0)
    add_branch(1)

    reset()
    first_key_chunk = qi * (tq // KEY_CHUNK) - win_chunks
    tiles = []
    for c in reversed(range(n_wb)):
        chunk = first_key_chunk + c
        band = jnp.where(chunk >= 0, wbias_scr[c], NEG)
        tiles.append(scores(kw_scr, jnp.maximum(chunk, 0), lambda keys: band))
    for s, keys in tiles:
        _flash_update(s, m_scr, acc_scr, vwt_scr[:, keys])
    add_branch(2)

    for h in range(hpg):
        o_ref[:, h * HEAD_DIM:(h + 1) * HEAD_DIM] = o_scr[:, h * tq:(h + 1) * tq].T.astype(o_ref.dtype)


def _nsa_prompt(q, gates_t, kc, vct, slc_rows, win_rows, bsz, t_len):
    m, qcols = q.shape
    hpg = qcols // (N_KV * HEAD_DIM)
    gw = hpg * HEAD_DIM
    tq = KEY_CHUNK
    assert t_len % (2 * KEY_CHUNK) == 0 and WINDOW % KEY_CHUNK == 0 and tq % KEY_CHUNK == 0
    nq = t_len // tq
    n_cmp_pad = kc.shape[2]
    n_slc = t_len // SLC_BLOCK
    n_wb = tq // KEY_CHUNK + WINDOW // KEY_CHUNK
    kern = functools.partial(_nsa_prompt_kernel, tq=tq, t_len=t_len, hpg=hpg, n_cmp_pad=n_cmp_pad)
    kv_k = lambda b, g, i: (b, 0, g)
    kv_v = lambda b, g, i: (b, 0, N_KV + g)
    return pl.pallas_call(
        kern,
        grid=(bsz, N_KV, nq),
        in_specs=[pl.BlockSpec((tq, gw), lambda b, g, i: (b * nq + i, g)),
                  pl.BlockSpec((gates_t.shape[0], tq), lambda b, g, i: (0, b * nq + i)),
                  pl.BlockSpec((1, 1, n_cmp_pad, HEAD_DIM), lambda b, g, i: (b, g, 0, 0)),
                  pl.BlockSpec((1, 1, HEAD_DIM, n_cmp_pad), lambda b, g, i: (b, g, 0, 0)),
                  pl.BlockSpec((1, t_len, HEAD_DIM), kv_k), pl.BlockSpec((1, t_len, HEAD_DIM), kv_v),
                  pl.BlockSpec((1, t_len, HEAD_DIM), kv_k), pl.BlockSpec((1, t_len, HEAD_DIM), kv_v)],
        out_specs=pl.BlockSpec((tq, gw), lambda b, g, i: (b * nq + i, g)),
        out_shape=jax.ShapeDtypeStruct((m, qcols), BF16),
        scratch_shapes=[pltpu.VMEM((t_len, HEAD_DIM), BF16), pltpu.VMEM((HEAD_DIM + ONES_ROWS, t_len), BF16),
                        pltpu.VMEM((t_len, HEAD_DIM), BF16), pltpu.VMEM((HEAD_DIM + ONES_ROWS, t_len), BF16),
                        pltpu.VMEM((HEAD_DIM, hpg * tq), BF16),
                        pltpu.VMEM((n_slc, tq), F32), pltpu.VMEM((n_slc, tq), F32),
                        pltpu.VMEM((t_len, tq), F32), pltpu.VMEM((n_wb, KEY_CHUNK, tq), F32),
                        pltpu.VMEM((1, hpg * tq), F32),
                        pltpu.VMEM((HEAD_DIM + ONES_ROWS, hpg * tq), F32),
                        pltpu.VMEM((HEAD_DIM, hpg * tq), F32)],
        compiler_params=_cparams(("parallel", "parallel", "arbitrary")),
        name="nsa_prompt",
    )(q, gates_t, kc, vct, slc_rows, slc_rows, win_rows, win_rows)


def _nsa_sample_select_kernel(q_ref, kc_ref, vct_ref, ocmp_ref, idx_ref, *, qpos, n_slc, nsp):
    q = q_ref[0, 0].astype(BF16)
    ncp = kc_ref.shape[2]
    s = _dot_nt(q, kc_ref[0, 0])
    m_idx = lax.broadcasted_iota(jnp.int32, (1, ncp), 1)
    mask = m_idx * CMP_STRIDE + (CMP_BLOCK - 1) <= qpos
    s = jnp.where(mask, s, NEG)
    e = jnp.where(mask, jnp.exp(s - jnp.max(s, axis=-1, keepdims=True)), 0.0)
    p = e / jnp.maximum(jnp.sum(e, axis=-1, keepdims=True), TINY)
    ocmp_ref[0, 0] = _dot_nt(p.astype(BF16), vct_ref[0, 0])

    ratio = SLC_BLOCK // CMP_STRIDE
    p_grp = jnp.broadcast_to(jnp.sum(p, axis=0, keepdims=True), (8, ncp))
    mm = lax.broadcasted_iota(jnp.int32, (ncp, nsp), 0)
    jj = lax.broadcasted_iota(jnp.int32, (ncp, nsp), 1)
    pool_t = ((mm >= ratio * jj - 1) & (mm <= ratio * jj + ratio - 1)).astype(BF16)
    p_slc = sum(_dot(piece, pool_t) for piece in _split3(p_grp))[0:1, :]
    blk = lax.broadcasted_iota(jnp.int32, (1, nsp), 1)
    cur = qpos // SLC_BLOCK
    forced = (blk == 0) | (blk == cur) | (blk == cur - 1)
    score = jnp.where(forced, FORCE, jnp.where(blk * SLC_BLOCK <= qpos, p_slc, -1.0))
    score = jnp.where(blk < n_slc, score, -2.0)

    ii = lax.broadcasted_iota(jnp.int32, (nsp, nsp), 0)
    jx = lax.broadcasted_iota(jnp.int32, (nsp, nsp), 1)
    mine = jnp.broadcast_to(score, (nsp, nsp))
    other = mine.T
    beats = (other > mine) | ((other == mine) & (ii < jx))
    rank = jnp.sum(beats.astype(F32), axis=0, keepdims=True)
    sel = (rank < float(N_SEL)).astype(F32)
    sel_other = jnp.broadcast_to(sel, (nsp, nsp)).T
    pos = jnp.sum(jnp.where(ii < jx, sel_other, 0.0), axis=0, keepdims=True)
    kk = lax.broadcasted_iota(jnp.int32, (N_SEL, nsp), 0).astype(F32)
    jrow = lax.broadcasted_iota(jnp.int32, (N_SEL, nsp), 1).astype(F32)
    hit = (sel > 0.5) & (pos == kk)
    idx = jnp.sum(jnp.where(hit, jrow, 0.0), axis=1, keepdims=True)
    idx_ref[0, 0] = jnp.broadcast_to(idx, (N_SEL, HEAD_DIM)).astype(jnp.int32)


def _nsa_sample_select(q, kc, vct, qpos, n_slc):
    dbs, _, hpg, _ = q.shape
    ncp = kc.shape[2]
    nsp = -(-n_slc // HEAD_DIM) * HEAD_DIM
    assert n_slc >= N_SEL
    blk4 = lambda s2, s3: pl.BlockSpec((1, 1, s2, s3), lambda b, g: (b, g, 0, 0))
    ocmp, idx = pl.pallas_call(
        functools.partial(_nsa_sample_select_kernel, qpos=qpos, n_slc=n_slc, nsp=nsp),
        grid=(dbs, N_KV),
        in_specs=[blk4(hpg, HEAD_DIM), blk4(ncp, HEAD_DIM), blk4(HEAD_DIM, ncp)],
        out_specs=[blk4(hpg, HEAD_DIM), blk4(N_SEL, HEAD_DIM)],
        out_shape=[jax.ShapeDtypeStruct((dbs, N_KV, hpg, HEAD_DIM), F32),
                   jax.ShapeDtypeStruct((dbs, N_KV, N_SEL, HEAD_DIM), jnp.int32)],
        compiler_params=_cparams(("parallel", "parallel")),
        name="nsa_sample_select",
    )(q, kc, vct)
    return ocmp, idx[..., 0]


def _nsa_sample_attend_kernel(pt_ref, idx_ref, q_ref, gate_ref, ocmp_ref, snew_ref, wnew_ref,
                              win_ref, *rest, qpos, n_cached, win_buf):
    blocks, o_ref = rest[:N_SEL], rest[N_SEL]
    b, g = pl.program_id(0), pl.program_id(1)
    slots = 2 * N_KV

    def group_rows(ref, n, col):
        return ref[pl.ds(col * N_KV + g, n, stride=slots), :]

    q = q_ref[0, 0].astype(BF16)
    qf = q.astype(F32)

    def attend(s, mask, v, k_new, v_new):
        s_self = jnp.sum(qf * k_new.astype(BF16).astype(F32), axis=-1, keepdims=True)
        s = jnp.where(mask, s, NEG)
        m = jnp.maximum(jnp.max(s, axis=-1, keepdims=True), s_self)
        e = jnp.where(mask, jnp.exp(s - m), 0.0)
        e_self = jnp.exp(s_self - m)
        l = jnp.sum(e, axis=-1, keepdims=True) + e_self
        acc = _dot(e.astype(BF16), v) + e_self.astype(BF16).astype(F32) * v_new.astype(BF16).astype(F32)
        return acc / jnp.maximum(l, TINY)

    n_keys = N_SEL * SLC_BLOCK
    k_all = jnp.concatenate([group_rows(blk, SLC_BLOCK, 0) for blk in blocks], axis=0).astype(BF16)
    v_all = jnp.concatenate([group_rows(blk, SLC_BLOCK, 1) for blk in blocks], axis=0).astype(BF16)
    lane = lax.broadcasted_iota(jnp.int32, (1, n_keys), 1)
    slot = lax.shift_right_arithmetic(lane, SLC_SHIFT)
    blk_id = jnp.zeros((1, n_keys), jnp.int32)
    for k in range(N_SEL):
        blk_id = jnp.where(slot == k, idx_ref[(b * N_KV + g) * N_SEL + k], blk_id)
    kpos = blk_id * SLC_BLOCK + (lane & (SLC_BLOCK - 1))
    slc_mask = (blk_id < n_cached) & (kpos <= qpos)
    o_slc = attend(_dot_nt(q, k_all), slc_mask, v_all,
                   snew_ref[0, pl.ds(g, 1), :], snew_ref[0, pl.ds(N_KV + g, 1), :])

    wi = lax.broadcasted_iota(jnp.int32, (1, win_buf), 1)
    dlt = win_buf - wi
    win_mask = (dlt >= 0) & (dlt < WINDOW) & (qpos - dlt >= 0)
    o_win = attend(_dot_nt(q, group_rows(win_ref, win_buf, 0).astype(BF16)), win_mask,
                   group_rows(win_ref, win_buf, 1).astype(BF16),
                   wnew_ref[0, pl.ds(g, 1), :], wnew_ref[0, pl.ds(N_KV + g, 1), :])

    gate = gate_ref[0, 0]
    o_ref[0, 0] = gate[:, 0:1] * ocmp_ref[0, 0] + gate[:, 1:2] * o_slc + gate[:, 2:3] * o_win


def _nsa_sample_attend(q, gates, ocmp, idx, page_table, slc_cache, slc_new, win_cache, win_new, qpos):
    dbs, _, hpg, _ = q.shape
    slots = 2 * N_KV
    win_buf = win_cache.shape[0] // (dbs * slots)
    n_cached = page_table.shape[1] * PAGE_SIZE // SLC_BLOCK
    bpp = PAGE_SIZE // SLC_BLOCK

    def blk_map(b, g, pt, ix, *, k):
        blk = jnp.minimum(ix[(b * N_KV + g) * N_SEL + k], n_cached - 1)
        return pt[b, blk // bpp] * bpp + blk % bpp, 0

    blk4 = lambda s2, s3: pl.BlockSpec((1, 1, s2, s3), lambda b, g, pt, ix: (b, g, 0, 0))
    new_spec = pl.BlockSpec((1, slots, HEAD_DIM), lambda b, g, pt, ix: (b, 0, 0))
    win_spec = pl.BlockSpec((win_buf * slots, HEAD_DIM), lambda b, g, pt, ix: (b, 0))
    gathered = [pl.BlockSpec((SLC_BLOCK * slots, HEAD_DIM), functools.partial(blk_map, k=k))
                for k in range(N_SEL)]
    grid_spec = pltpu.PrefetchScalarGridSpec(
        num_scalar_prefetch=2,
        grid=(dbs, N_KV),
        in_specs=[blk4(hpg, HEAD_DIM), blk4(hpg, N_BRANCH), blk4(hpg, HEAD_DIM), new_spec, new_spec,
                  win_spec] + gathered,
        out_specs=blk4(hpg, HEAD_DIM),
    )
    return pl.pallas_call(
        functools.partial(_nsa_sample_attend_kernel, qpos=qpos, n_cached=n_cached, win_buf=win_buf),
        grid_spec=grid_spec,
        out_shape=jax.ShapeDtypeStruct((dbs, N_KV, hpg, HEAD_DIM), F32),
        compiler_params=_cparams(("parallel", "parallel")),
        name="nsa_sample_attend",
    )(page_table, idx.reshape(-1), q, gates, ocmp,
      slc_new.reshape(dbs, slots, HEAD_DIM), win_new.reshape(dbs, slots, HEAD_DIM),
      win_cache, *([slc_cache] * N_SEL))


def kernel(x_prompt, x_sample, c_prompt, c_sample, state_lru_h, state_conv, cache_cmp_kv, cache_slc_kv, cache_win_kv, page_table, ada_w, ada_b, norm1_w, norm2_w, lru_w_in, lru_b_in, lru_conv_w, lru_conv_b, lru_gate_w, lru_gate_b, lru_lambda, lru_w_out, lru_b_out, ffn_w13, ffn_w2, kv_ada_w, kv_ada_b, kv_norm_w, w_kv, k_norm_w, cmp_pe, cmp_w1, cmp_b1, cmp_w2, cmp_b2, w_qg, q_norm_w, w_o):
    bsz, t_len, d = x_prompt.shape
    dbs, dec_seq, _ = x_sample.shape
    depth = ada_w.shape[0]
    n_a = lru_w_in.shape[0]
    assert dec_seq == 1 and depth == 2 and n_a == 1 and w_qg.shape[0] == 1
    lw = lru_w_in.shape[2] // 2
    d_ff = ffn_w2.shape[1]
    n_pages = page_table.shape[1]
    past_len = n_pages * PAGE_SIZE
    qcols = w_o.shape[1]
    hpg = qcols // (N_KV * HEAD_DIM)
    hist = CONV_W - 1
    branch_cols = KV_ROW

    n_c = bsz + dbs
    c_all = jnp.pad(jnp.concatenate([c_prompt, c_sample], axis=0), ((0, (-n_c) % 8), (0, 0)))
    mods = [_matmul(c_all, ada_w, layer=l, bias=ada_b[l], a_silu=True, tn=1024, name="ada")
            for l in range(depth)]
    kv_mod = _matmul(c_all, kv_ada_w, bias=kv_ada_b, a_silu=True, tn=1024, name="kv_ada")

    def split_mod(mat, n, prompt):
        parts = [mat[:, i * d:(i + 1) * d] for i in range(n)]
        if prompt:
            return [p[:bsz].reshape(bsz, 1, d) for p in parts]
        return [p[bsz:n_c].reshape(1, dbs, d) for p in parts]

    w13 = ffn_w13
    w2 = ffn_w2.astype(BF16)
    tk2 = _pick(d_ff, (d_ff // 2,)) if d_ff > 4096 else d_ff
    w_in, w_out, wq, wo = lru_w_in, lru_w_out, w_qg, w_o
    wg_t = w_qg[0, :, qcols:].T
    pe = cmp_pe.reshape(2, -1)

    def mm2(a_p, a_s, w, *, res=(None, None), gate=(None, None), **kw):
        return _matmul(a_p, w, res=res[0], gate=gate[0], small=(a_s, res[1], gate[1]), **kw)

    def norm2(x, w, mod, shift_i, scale_i):
        return (_norm_mod(x[0], w, mod[0][shift_i], mod[0][scale_i], BF16),
                _norm_mod(x[1], w, mod[1][shift_i], mod[1][scale_i], F32))

    def ffn(x, l, mod):
        u = norm2(x, norm2_w[l], mod, 3, 4)
        act = mm2(*u, w13, layer=l, col0=0, col0_b=d_ff, ncols=d_ff, epi="swiglu", tn=256,
                  out_dtype=BF16, name="ffn_up")
        return mm2(*act, w2, layer=l, tk=tk2, epi="residual", res=x, gate=(mod[0][5], mod[1][5]),
                   name="ffn_down")

    mod_l = [(split_mod(m, 6, True), split_mod(m, 6, False)) for m in mods]
    mod_kv = (split_mod(kv_mod, 2, True), split_mod(kv_mod, 2, False))
    wide = dict(tm=min(1024, t_len), tn=512)
    x = (x_prompt.reshape(bsz * t_len, d), x_sample.reshape(dbs, d))

    u = norm2(x, norm1_w[0], mod_l[0], 0, 1)
    gy = mm2(*u, w_in, col0=0, ncols=lw, bias=lru_b_in[0], epi="gelu", out_dtype=BF16, name="lru_in_y",
             **wide)
    xb = mm2(*u, w_in, col0=lw, ncols=lw, bias=lru_b_in[0], name="lru_in_x", **wide)
    lru_args = (lru_conv_w[0], lru_conv_b[0], lru_gate_w[0], lru_gate_b[0], lru_lambda[0])
    hg_p, conv_p, h_p = _lru(xb[0].reshape(bsz, t_len, lw), gy[0].reshape(bsz, t_len, lw),
                             jnp.zeros((bsz, hist, lw), F32), jnp.zeros((bsz, lw), F32), *lru_args)
    hg_s, conv_s, h_s = _lru_step(xb[1], gy[1], jnp.swapaxes(state_conv[0], 0, 1), state_lru_h[0],
                                  *lru_args)
    x = mm2(hg_p.reshape(bsz * t_len, lw), hg_s, w_out, bias=lru_b_out[0], epi="residual", res=x,
            gate=(mod_l[0][0][2], mod_l[0][1][2]), name="lru_out", **wide)
    x = ffn(x, 0, mod_l[0])

    s = norm2(x, kv_norm_w, mod_kv, 0, 1)
    kv_rows = [mm2(*s, w_kv, col0=0, ncols=branch_cols, name="kv_cmp", **wide)]
    for br in range(1, N_BRANCH):
        kv_rows.append(mm2(*s, w_kv, col0=br * branch_cols, ncols=branch_cols, epi="headnorm",
                           hn_w=k_norm_w[br], n_norm_tiles=branch_cols // 2 // wide["tn"],
                           name="kv_norm", **wide))
    (cmp_p, cmp_s), (slc_p, slc_s), (win_p, win_s) = kv_rows

    u = norm2(x, norm1_w[1], mod_l[1], 0, 1)
    q_p, q_s = mm2(*u, wq, col0=0, ncols=qcols, epi="headnorm", hn_w=q_norm_w[0],
                   n_norm_tiles=qcols // wide["tn"], post_scale=HEAD_DIM ** -0.5, out_dtype=BF16,
                   name="nsa_q", **wide)
    gt_p, gt_s = _branch_gates_t(u[0], wg_t), _branch_gates_t(u[1], wg_t)

    pq_p = _cmp_pq(cmp_p.reshape(bsz, t_len, KV_ROW), cmp_w1)
    kc_p, vct_p = _cmp_finish(pq_p, pe, cmp_w1, cmp_b1, cmp_w2, cmp_b2, k_norm_w[0])
    o_p = _nsa_prompt(q_p, gt_p, kc_p, vct_p, slc_p.reshape(bsz, t_len, KV_ROW),
                      win_p.reshape(bsz, t_len, KV_ROW), bsz, t_len)

    pq_s = _cmp_pq(cache_cmp_kv.reshape(-1, HEAD_DIM), cmp_w1, page_table=page_table)
    kc_s, vct_s = _cmp_finish(pq_s, pe, cmp_w1, cmp_b1, cmp_w2, cmp_b2, k_norm_w[0])
    n_slc_s = -(-(past_len + dec_seq) // SLC_BLOCK)
    q_s4 = q_s.reshape(dbs, N_KV, hpg, HEAD_DIM)
    ocmp_s, idx_s = _nsa_sample_select(q_s4, kc_s, vct_s, past_len, n_slc_s)
    gates_s = gt_s.T.reshape(dbs, N_KV, hpg, N_BRANCH)
    o_s = _nsa_sample_attend(q_s4, gates_s, ocmp_s, idx_s, page_table,
                             cache_slc_kv.reshape(-1, HEAD_DIM), slc_s,
                             cache_win_kv.reshape(-1, HEAD_DIM), win_s, past_len)

    x = mm2(o_p, o_s.reshape(dbs, qcols), wo, epi="residual", res=x,
            gate=(mod_l[1][0][2], mod_l[1][1][2]), name="nsa_out", **wide)
    x = ffn(x, 1, mod_l[1])
    y_prompt, y_sample = x[0].reshape(bsz, t_len, d), x[1].reshape(dbs, dec_seq, d)

    kv5 = lambda a, n, t: a.reshape(n, t, 2, N_KV, HEAD_DIM)
    win_buf = cache_win_kv.shape[1]
    win_s5 = kv5(win_s, dbs, 1)
    new_win_s = jnp.concatenate([cache_win_kv, win_s5], axis=1)[:, -win_buf:]
    return (y_prompt, y_sample, h_p[None], h_s[None], conv_p[None],
            jnp.swapaxes(conv_s, 0, 1)[None],
            kv5(cmp_p, bsz, t_len), kv5(cmp_s, dbs, 1), kv5(slc_p, bsz, t_len), kv5(slc_s, dbs, 1),
            kv5(win_p, bsz, t_len)[:, -WINDOW:], new_win_s)
```
